```python
import math
import jax, jax.numpy as jnp
from jax import lax
import numpy as np

D_MODEL = 1024
BATCH = 4
SEQ = 4096
DEPTH = 4

HEAD_DIM = 64
Q_BLOCK = 128
RMS_EPS = 1e-6
NEG_INF = -1e30
A_HEADS = 8
A_LATENT = 256
IDX_HEADS = 8
IDX_DIM = 64
TOPK_MAX = 256
B_HEADS = 4
C_HEADS = 16
REL_BUCKETS = 32
REL_MAX_DIST = 128
REL_HEADS = A_HEADS + B_HEADS
PEER_HEADS = 8
PEER_QDIM = 256
PEER_NKEYS = 128
PEER_EXPERTS = PEER_NKEYS * PEER_NKEYS
PEER_TOPK = 16
PEER_CHUNK = 128
EVEN_SIZES = (A_HEADS * HEAD_DIM, A_LATENT, IDX_HEADS * IDX_DIM, IDX_DIM, IDX_HEADS,
              B_HEADS * 2 * HEAD_DIM, B_HEADS * 2 * HEAD_DIM, B_HEADS * 2 * HEAD_DIM)
EVEN_IN = A_HEADS * HEAD_DIM + A_LATENT + IDX_HEADS * IDX_DIM + IDX_DIM + IDX_HEADS + 3 * B_HEADS * 2 * HEAD_DIM
EVEN_MIX = A_HEADS * HEAD_DIM + B_HEADS * 2 * HEAD_DIM
ODD_SIZES = (C_HEADS * HEAD_DIM, C_HEADS * HEAD_DIM, C_HEADS * HEAD_DIM, C_HEADS)
ODD_IN = 3 * C_HEADS * HEAD_DIM + C_HEADS
ODD_MIX = C_HEADS * HEAD_DIM

kernel_name = "hybrid_dsa_diff_fox_peer_trunk"


def rmsnorm(x, g):
    xf = x.astype(jnp.float32)
    y = xf * lax.rsqrt(jnp.mean(xf * xf, axis=-1, keepdims=True) + RMS_EPS)
    return (y * g.astype(jnp.float32)).astype(x.dtype)


def split_cols(p, sizes):
    points = np.cumsum(np.array(sizes))[:-1].tolist()
    return jnp.split(p, points, axis=-1)


def rel_bucket(dist):
    max_exact = REL_BUCKETS // 2
    d = jnp.maximum(dist, 0)
    df = jnp.maximum(d, 1).astype(jnp.float32)
    large = max_exact + (jnp.log(df / max_exact) / math.log(REL_MAX_DIST / max_exact)
                         * (REL_BUCKETS - max_exact)).astype(jnp.int32)
    large = jnp.minimum(large, REL_BUCKETS - 1)
    return jnp.where(d < max_exact, d, large)


def sweep_blocks(fn, batch, seq_len):
    n = seq_len // Q_BLOCK
    out = lax.map(fn, jnp.arange(n, dtype=jnp.int32) * Q_BLOCK)
    out = jnp.moveaxis(out, 0, 1)
    return out.reshape((batch, seq_len) + out.shape[3:])


def dsa_attention(q, ckv, q_idx, k_idx, w_idx, w_uk, w_uv, bias_a):
    B, S = q.shape[0], q.shape[1]
    topk = min(TOPK_MAX, S // 4)
    key_pos = jnp.arange(S, dtype=jnp.int32)
    scale = HEAD_DIM ** -0.5
    idx_scale = IDX_DIM ** -0.5
    q_lat = jnp.einsum('bthd,chd->bthc', q, w_uk)

    def block(start):
        qpos = start + jnp.arange(Q_BLOCK, dtype=jnp.int32)
        qi = lax.dynamic_slice_in_dim(q_idx, start, Q_BLOCK, axis=1)
        wi = lax.dynamic_slice_in_dim(w_idx, start, Q_BLOCK, axis=1)
        ql = lax.dynamic_slice_in_dim(q_lat, start, Q_BLOCK, axis=1)
        li = jnp.einsum('bthd,bsd->bths', qi, k_idx).astype(jnp.float32) * idx_scale
        iscore = jnp.einsum('bths,bth->bts', jax.nn.relu(li), wi.astype(jnp.float32))
        causal = key_pos[None, :] <= qpos[:, None]
        iscore = jnp.where(causal[None], iscore, -jnp.inf)
        _, sel = lax.top_k(iscore, topk)
        kv_sel = jax.vmap(lambda lat, ix: lat[ix])(ckv, sel)
        dist = qpos[None, :, None] - sel
        s = jnp.einsum('bthc,btkc->bthk', ql, kv_sel).astype(jnp.float32) * scale
        s = s + jnp.moveaxis(bias_a[rel_bucket(dist)], -1, 2).astype(jnp.float32)
        s = jnp.where((dist >= 0)[:, :, None, :], s, NEG_INF)
        p = jax.nn.softmax(s, axis=-1).astype(ckv.dtype)
        o_lat = jnp.einsum('bthk,btkc->bthc', p, kv_sel)
        o = jnp.einsum('bthc,chd->bthd', o_lat, w_uv)
        return o.reshape(B, Q_BLOCK, A_HEADS * HEAD_DIM)

    return sweep_blocks(block, B, S)


def diff_attention(q1, q2, k1, k2, v, lam, lam_init, subln_g, bias_b):
    B, S = q1.shape[0], q1.shape[1]
    key_pos = jnp.arange(S, dtype=jnp.int32)
    scale = HEAD_DIM ** -0.5

    def block(start):
        qpos = start + jnp.arange(Q_BLOCK, dtype=jnp.int32)
        causal = key_pos[None, :] <= qpos[:, None]
        bias = jnp.moveaxis(bias_b[rel_bucket(qpos[:, None] - key_pos[None, :])], -1, 0)
        bias = bias.astype(jnp.float32)

        def probs(q, k):
            qb = lax.dynamic_slice_in_dim(q, start, Q_BLOCK, axis=1)
            s = jnp.einsum('bthd,bshd->bhts', qb, k).astype(jnp.float32) * scale + bias
            return jax.nn.softmax(jnp.where(causal, s, NEG_INF), axis=-1)

        a = probs(q1, k1) - lam * probs(q2, k2)
        return jnp.einsum('bhts,bshe->bthe', a.astype(v.dtype), v)

    o = sweep_blocks(block, B, S)
    o = rmsnorm(o, subln_g) * (1.0 - lam_init)
    return o.reshape(B, S, B_HEADS * 2 * HEAD_DIM)


def forgetting_attention(q, k, v, f_logit):
    B, S = q.shape[0], q.shape[1]
    key_pos = jnp.arange(S, dtype=jnp.int32)
    scale = HEAD_DIM ** -0.5
    cum = jnp.moveaxis(jnp.cumsum(jax.nn.log_sigmoid(f_logit.astype(jnp.float32)), axis=1), -1, 1)

    def block(start):
        qpos = start + jnp.arange(Q_BLOCK, dtype=jnp.int32)
        causal = key_pos[None, :] <= qpos[:, None]
        qb = lax.dynamic_slice_in_dim(q, start, Q_BLOCK, axis=1)
        cq = lax.dynamic_slice_in_dim(cum, start, Q_BLOCK, axis=2)
        s = jnp.einsum('bthd,bshd->bhts', qb, k).astype(jnp.float32) * scale
        s = s + cq[..., :, None] - cum[..., None, :]
        p = jax.nn.softmax(jnp.where(causal, s, NEG_INF), axis=-1).astype(v.dtype)
        return jnp.einsum('bhts,bshd->bthd', p, v)

    return sweep_blocks(block, B, S).reshape(B, S, C_HEADS * HEAD_DIM)


def even_mixer(h, w_in, w_out, kv_norm, w_uk, w_uv, lam_vec, subln_g, rel_bias, lam_init):
    B, S, _ = h.shape
    q_a, ckv, q_i, k_i, w_i, q_b, k_b, v_b = split_cols(h @ w_in, EVEN_SIZES)
    q_a = q_a.reshape(B, S, A_HEADS, HEAD_DIM)
    ckv = rmsnorm(ckv, kv_norm)
    q_i = q_i.reshape(B, S, IDX_HEADS, IDX_DIM)
    w_i = w_i * (IDX_HEADS ** -0.5)
    o_a = dsa_attention(q_a, ckv, q_i, k_i, w_i, w_uk, w_uv, rel_bias[:, :A_HEADS])
    q_b = q_b.reshape(B, S, B_HEADS, 2, HEAD_DIM)
    k_b = k_b.reshape(B, S, B_HEADS, 2, HEAD_DIM)
    v_b = v_b.reshape(B, S, B_HEADS, 2 * HEAD_DIM)
    lv = lam_vec.astype(jnp.float32)
    lam = jnp.exp(jnp.sum(lv[0] * lv[1])) - jnp.exp(jnp.sum(lv[2] * lv[3])) + lam_init
    o_b = diff_attention(q_b[..., 0, :], q_b[..., 1, :], k_b[..., 0, :], k_b[..., 1, :],
                         v_b, lam, lam_init, subln_g, rel_bias[:, A_HEADS:])
    return jnp.concatenate([o_a, o_b], axis=-1) @ w_out


def odd_mixer(h, w_in, b_forget, w_out):
    B, S, _ = h.shape
    q, k, v, f = split_cols(h @ w_in, ODD_SIZES)
    shp = (B, S, C_HEADS, HEAD_DIM)
    o = forgetting_attention(q.reshape(shp), k.reshape(shp), v.reshape(shp), f + b_forget)
    return o @ w_out


def peer_ffn(h, w_q, sub_keys, u, v):
    B, S, D = h.shape
    T = B * S
    ht = h.reshape(T, D)
    q = (ht @ w_q).reshape(T, PEER_HEADS, 2, PEER_QDIM // 2)
    s = jnp.einsum('thpd,pnd->thpn', q, sub_keys).astype(jnp.float32)
    top_s, top_i = lax.top_k(s, PEER_TOPK)
    cand_s = top_s[:, :, 0, :, None] + top_s[:, :, 1, None, :]
    cand_i = top_i[:, :, 0, :, None] * PEER_NKEYS + top_i[:, :, 1, None, :]
    best_s, best_j = lax.top_k(cand_s.reshape(T, PEER_HEADS, -1), PEER_TOPK)
    experts = jnp.take_along_axis(cand_i.reshape(T, PEER_HEADS, -1), best_j, axis=-1)
    gates = jax.nn.softmax(best_s, axis=-1)
    n_chunks = T // PEER_CHUNK
    hk = PEER_HEADS * PEER_TOPK

    def chunk(args):
        xc, ec, gc = args
        act = jax.nn.gelu(jnp.einsum('cd,ced->ce', xc, u[ec]))
        return jnp.einsum('ce,ced->cd', gc.astype(xc.dtype) * act, v[ec])

    out = lax.map(chunk, (ht.reshape(n_chunks, PEER_CHUNK, D),
                          experts.reshape(n_chunks, PEER_CHUNK, hk),
                          gates.reshape(n_chunks, PEER_CHUNK, hk)))
    return out.reshape(B, S, D)


def setup_inputs(seed: int = 0) -> dict:
    key = jax.random.key(seed)
    ks = jax.random.split(key, 24)
    n_even = (DEPTH + 1) // 2
    n_odd = DEPTH // 2

    def nrm(k, shape, scale):
        return jax.random.normal(k, shape, jnp.float32) * scale

    return {
        "x": nrm(ks[0], (BATCH, SEQ, D_MODEL), 1.0),
        "c": nrm(ks[1], (BATCH, D_MODEL), 1.0),
        "rel_bias": nrm(ks[2], (REL_BUCKETS, REL_HEADS), 0.5),
        "ada_w": nrm(ks[3], (DEPTH, D_MODEL, 6 * D_MODEL), 0.5 * D_MODEL ** -0.5),
        "ada_b": nrm(ks[4], (DEPTH, 6 * D_MODEL), 0.02),
        "norm_mix": 1.0 + nrm(ks[5], (DEPTH, D_MODEL), 0.02),
        "norm_ffn": 1.0 + nrm(ks[6], (DEPTH, D_MODEL), 0.02),
        "norm_final": 1.0 + nrm(ks[7], (D_MODEL,), 0.02),
        "even_w_in": nrm(ks[8], (n_even, D_MODEL, EVEN_IN), D_MODEL ** -0.5),
        "even_w_out": nrm(ks[9], (n_even, EVEN_MIX, D_MODEL), EVEN_MIX ** -0.5),
        "a_kv_norm": 1.0 + nrm(ks[10], (n_even, A_LATENT), 0.02),
        "a_w_uk": nrm(ks[11], (n_even, A_LATENT, A_HEADS, HEAD_DIM), A_LATENT ** -0.5),
        "a_w_uv": nrm(ks[12], (n_even, A_LATENT, A_HEADS, HEAD_DIM), A_LATENT ** -0.5),
        "b_lambda": nrm(ks[13], (n_even, 4, HEAD_DIM), 0.1),
        "b_subln": 1.0 + nrm(ks[14], (n_even, 2 * HEAD_DIM), 0.02),
        "odd_w_in": nrm(ks[15], (n_odd, D_MODEL, ODD_IN), D_MODEL ** -0.5),
        "odd_b_forget": 2.0 + nrm(ks[16], (n_odd, C_HEADS), 0.5),
        "odd_w_out": nrm(ks[17], (n_odd, ODD_MIX, D_MODEL), ODD_MIX ** -0.5),
        "peer_w_q": nrm(ks[18], (DEPTH, D_MODEL, PEER_HEADS * PEER_QDIM), D_MODEL ** -0.5),
        "peer_sub_keys": nrm(ks[19], (DEPTH, 2, PEER_NKEYS, PEER_QDIM // 2), (PEER_QDIM // 2) ** -0.5),
        "peer_u": nrm(ks[20], (DEPTH, PEER_EXPERTS, D_MODEL), D_MODEL ** -0.5),
        "peer_v": nrm(ks[21], (DEPTH, PEER_EXPERTS, D_MODEL), PEER_HEADS ** -0.5),
    }


def reference(x, c, rel_bias, ada_w, ada_b, norm_mix, norm_ffn, norm_final,
              even_w_in, even_w_out, a_kv_norm, a_w_uk, a_w_uv, b_lambda, b_subln,
              odd_w_in, odd_b_forget, odd_w_out,
              peer_w_q, peer_sub_keys, peer_u, peer_v):
    c_act = jax.nn.silu(c)
    for layer in range(DEPTH):
        mod = (c_act @ ada_w[layer] + ada_b[layer])[:, None, :]
        sh1, sc1, g1, sh2, sc2, g2 = jnp.split(mod, 6, axis=-1)
        h = rmsnorm(x, norm_mix[layer]) * (1.0 + sc1) + sh1
        if layer % 2 == 0:
            e = layer // 2
            lam_init = 0.8 - 0.6 * math.exp(-0.3 * layer)
            y = even_mixer(h, even_w_in[e], even_w_out[e], a_kv_norm[e], a_w_uk[e], a_w_uv[e],
                           b_lambda[e], b_subln[e], rel_bias, lam_init)
        else:
            o = layer // 2
            y = odd_mixer(h, odd_w_in[o], odd_b_forget[o], odd_w_out[o])
        x = x + g1 * y
        h = rmsnorm(x, norm_ffn[layer]) * (1.0 + sc2) + sh2
        x = x + g2 * peer_ffn(h, peer_w_q[layer], peer_sub_keys[layer], peer_u[layer], peer_v[layer])
    return rmsnorm(x, norm_final)
```

```python
import functools
import math

import numpy as np
import jax
import jax.numpy as jnp
from jax import lax
from jax.experimental import pallas as pl
from jax.experimental.pallas import tpu as pltpu

F32 = jnp.float32
BF16 = jnp.bfloat16
I32 = jnp.int32

HEAD_DIM = 64
RMS_EPS = 1e-6
NEG_INF = -1e30
A_HEADS = 8
A_LATENT = 256
IDX_HEADS = 8
IDX_DIM = 64
TOPK_MAX = 256
B_HEADS = 4
C_HEADS = 16
REL_BUCKETS = 32
REL_MAX_DIST = 128
PEER_HEADS = 8
PEER_TOPK = 16

LANES = 128
ATT_BLOCK = 256
ROW_TILE = 512
PEER_TOKEN_TILE = 512
PEER_EXPERT_TILE = 1024
ROUTER_TILE = 512
VMEM_LIMIT = 56 * 1024 * 1024
INT_MIN = -2 ** 31

_NT = (((1,), (1,)), ((), ()))


def _params(sem):
    return pltpu.CompilerParams(dimension_semantics=sem, vmem_limit_bytes=VMEM_LIMIT)


def _rms(x, g):
    return x * lax.rsqrt(jnp.mean(x * x, axis=-1, keepdims=True) + RMS_EPS) * g


def _ada_kernel(c_ref, w_ref, b_ref, o_ref):
    c = c_ref[...]
    ca = c * jax.nn.sigmoid(c)
    o_ref[0] = jnp.dot(ca, w_ref[0], precision=lax.Precision.HIGHEST,
                       preferred_element_type=F32) + b_ref[0]


def _ada(c, ada_w, ada_b):
    depth, d, n = ada_w.shape
    b = c.shape[0]
    bp = 8
    cp = jnp.zeros((bp, d), F32).at[:b].set(c)
    tn = 1536
    out = pl.pallas_call(
        _ada_kernel,
        grid=(depth, n // tn),
        in_specs=[pl.BlockSpec((bp, d), lambda l, j: (0, 0)),
                  pl.BlockSpec((1, d, tn), lambda l, j: (l, 0, j)),
                  pl.BlockSpec((1, 1, tn), lambda l, j: (l, 0, j))],
        out_specs=pl.BlockSpec((1, bp, tn), lambda l, j: (l, 0, j)),
        out_shape=jax.ShapeDtypeStruct((depth, bp, n), F32),
        compiler_params=_params(("arbitrary", "arbitrary")),
        name="ada_mod",
    )(cp, ada_w, ada_b.reshape(depth, 1, n))
    return out[:, :b]


def _emit_segments(hb, w_ref, segs, outs, kvn_ref):
    k = 0
    for (a, b, post, dtypes) in segs:
        r = jnp.dot(hb, w_ref[:, a:b], preferred_element_type=F32)
        if post == "kvnorm":
            r = _rms(r, kvn_ref[...])
        for dt in dtypes:
            outs[k][...] = r.astype(dt)
            k += 1


def _normproj_kernel(*refs, segs, has_kvn, emit_h):
    x_ref, gain_ref, sc_ref, sh_ref, w_ref = refs[:5]
    pos = 5
    kvn_ref = None
    if has_kvn:
        kvn_ref = refs[pos]
        pos += 1
    outs = list(refs[pos:])
    h = _rms(x_ref[...], gain_ref[...]) * (1.0 + sc_ref[0]) + sh_ref[0]
    hb = h.astype(BF16)
    if emit_h:
        outs[0][...] = hb
        outs = outs[1:]
    _emit_segments(hb, w_ref, segs, outs, kvn_ref)


def _resproj_kernel(*refs, n_mix, segs):
    x_ref = refs[0]
    mix_refs = refs[1:1 + n_mix]
    wo_ref, g_ref, gain_ref, sc_ref, sh_ref, w_ref = refs[1 + n_mix:7 + n_mix]
    outs = list(refs[7 + n_mix:])
    y = None
    off = 0
    for m in mix_refs:
        kdim = m.shape[1]
        t = jnp.dot(m[...], wo_ref[off:off + kdim, :], preferred_element_type=F32)
        y = t if y is None else y + t
        off += kdim
    xn = x_ref[...] + g_ref[0] * y
    outs[0][...] = xn
    h = _rms(xn, gain_ref[...]) * (1.0 + sc_ref[0]) + sh_ref[0]
    hb = h.astype(BF16)
    outs[1][...] = hb
    _emit_segments(hb, w_ref, segs, outs[2:], None)


def _seg_out_shapes(t, segs):
    shapes, specs = [], []
    for (a, b, _, dtypes) in segs:
        for dt in dtypes:
            shapes.append(jax.ShapeDtypeStruct((t, b - a), dt))
            specs.append(pl.BlockSpec((ROW_TILE, b - a), lambda i: (i, 0)))
    return shapes, specs


def _normproj(x, gain, sc, sh, w, segs, seq, kvn=None, emit_h=False, name="normproj"):
    t, d = x.shape
    tm = ROW_TILE
    per_b = seq // tm
    n = w.shape[1]
    mod_spec = pl.BlockSpec((1, 1, d), lambda i: (i // per_b, 0, 0))
    in_specs = [pl.BlockSpec((tm, d), lambda i: (i, 0)),
                pl.BlockSpec((1, d), lambda i: (0, 0)), mod_spec, mod_spec,
                pl.BlockSpec((d, n), lambda i: (0, 0))]
    args = [x, gain.reshape(1, d), sc, sh, w]
    if kvn is not None:
        in_specs.append(pl.BlockSpec((1, kvn.shape[-1]), lambda i: (0, 0)))
        args.append(kvn.reshape(1, -1))
    shapes, specs = _seg_out_shapes(t, segs)
    if emit_h:
        shapes = [jax.ShapeDtypeStruct((t, d), BF16)] + shapes
        specs = [pl.BlockSpec((tm, d), lambda i: (i, 0))] + specs
    return pl.pallas_call(
        functools.partial(_normproj_kernel, segs=segs, has_kvn=kvn is not None, emit_h=emit_h),
        grid=(t // tm,), in_specs=in_specs, out_specs=specs, out_shape=shapes,
        compiler_params=_params(("arbitrary",)), name=name,
    )(*args)


def _resproj(x, mixes, w_out, g, gain, sc, sh, w, segs, seq, name="resproj"):
    t, d = x.shape
    tm = ROW_TILE
    per_b = seq // tm
    n = w.shape[1]
    mod_spec = pl.BlockSpec((1, 1, d), lambda i: (i // per_b, 0, 0))
    in_specs = [pl.BlockSpec((tm, d), lambda i: (i, 0))]
    in_specs += [pl.BlockSpec((tm, m.shape[1]), lambda i: (i, 0)) for m in mixes]
    in_specs += [pl.BlockSpec(w_out.shape, lambda i: (0, 0)), mod_spec,
                 pl.BlockSpec((1, d), lambda i: (0, 0)), mod_spec, mod_spec,
                 pl.BlockSpec((d, n), lambda i: (0, 0))]
    shapes, specs = _seg_out_shapes(t, segs)
    shapes = [jax.ShapeDtypeStruct((t, d), F32), jax.ShapeDtypeStruct((t, d), BF16)] + shapes
    specs = [pl.BlockSpec((tm, d), lambda i: (i, 0)), pl.BlockSpec((tm, d), lambda i: (i, 0))] + specs
    return pl.pallas_call(
        functools.partial(_resproj_kernel, n_mix=len(mixes), segs=segs),
        grid=(t // tm,), in_specs=in_specs, out_specs=specs, out_shape=shapes,
        compiler_params=_params(("arbitrary",)), name=name,
    )(x, *mixes, w_out, g, gain.reshape(1, d), sc, sh, w)


def _bucket_table(n):
    max_exact = REL_BUCKETS // 2
    d = np.arange(n)
    df = np.maximum(d, 1).astype(np.float32)
    large = max_exact + (np.log(df / max_exact) / math.log(REL_MAX_DIST / max_exact)
                         * (REL_BUCKETS - max_exact)).astype(np.int32)
    large = np.minimum(large, REL_BUCKETS - 1)
    return np.where(d < max_exact, d, large).astype(np.int32)


def _bias_tiles_kernel(rb_ref, bk_ref, o_ref):
    h = pl.program_id(0)
    for t in range(2):
        bt = bk_ref[t]
        acc = jnp.zeros(bt.shape, F32)
        for b in range(REL_BUCKETS):
            acc = jnp.where(bt == b, rb_ref[b, h], acc)
        o_ref[0, t] = acc


def _bias_tiles(rel_bias):
    tb = ATT_BLOCK
    nh = rel_bias.shape[1]
    table = _bucket_table(2 * tb)
    r = np.arange(tb)[:, None]
    c = np.arange(tb)[None, :]
    bk = np.stack([table[np.maximum(r - c, 0)], table[tb + r - c]]).astype(np.int32)
    return pl.pallas_call(
        _bias_tiles_kernel,
        grid=(nh,),
        in_specs=[pl.BlockSpec(memory_space=pltpu.SMEM),
                  pl.BlockSpec((2, tb, tb), lambda h: (0, 0, 0))],
        out_specs=pl.BlockSpec((1, 2, tb, tb), lambda h: (h, 0, 0, 0)),
        out_shape=jax.ShapeDtypeStruct((nh, 2, tb, tb), F32),
        compiler_params=_params(("arbitrary",)), name="bias_tiles",
    )(rel_bias, jnp.asarray(bk))


def _softmax_step(s, vb, carry):
    m, l, acc = carry
    m_new = jnp.maximum(m, jnp.max(s, axis=-1, keepdims=True))
    alpha = jnp.exp(m - m_new)
    p = jnp.exp(s - m_new)
    l = alpha * l + jnp.sum(p, axis=-1, keepdims=True)
    acc = alpha * acc + jnp.dot(p.astype(BF16), vb, preferred_element_type=F32)
    return m_new, l, acc


def _softmax_init(tq, dv):
    return (jnp.full((tq, 1), NEG_INF, F32), jnp.zeros((tq, 1), F32), jnp.zeros((tq, dv), F32))


def _causal(tq, tk):
    row = lax.broadcasted_iota(I32, (tq, tk), 0)
    col = lax.broadcasted_iota(I32, (tq, tk), 1)
    return col <= row


def _cumgate_kernel(f_ref, b_ref, fo_ref, ft_ref, *, tb):
    nblk = f_ref.shape[0] // tb
    row = lax.broadcasted_iota(I32, (tb, tb), 0)
    col = lax.broadcasted_iota(I32, (tb, tb), 1)
    tri = jnp.where(col <= row, 1.0, 0.0).astype(F32)

    def body(j, carry):
        z = f_ref[pl.ds(j * tb, tb), :] + b_ref[...]
        ls = -(jnp.maximum(-z, 0.0) + jnp.log(1.0 + jnp.exp(-jnp.abs(z))))
        cs = jnp.dot(tri, ls, precision=lax.Precision.HIGHEST, preferred_element_type=F32) + carry
        fo_ref[pl.ds(j * tb, tb), :] = cs
        ft_ref[0, j] = cs.T[:C_HEADS, :]
        return cs[tb - 1:tb, :]

    lax.fori_loop(0, nblk, body, jnp.zeros((1, LANES), F32))


def _cumgate(f, b_forget, batch, seq):
    tb = ATT_BLOCK
    nblk = seq // tb
    bpad = jnp.zeros((1, LANES), F32).at[0, :C_HEADS].set(b_forget)
    return pl.pallas_call(
        functools.partial(_cumgate_kernel, tb=tb),
        grid=(batch,),
        in_specs=[pl.BlockSpec((seq, LANES), lambda b: (b, 0)),
                  pl.BlockSpec((1, LANES), lambda b: (0, 0))],
        out_specs=[pl.BlockSpec((seq, LANES), lambda b: (b, 0)),
                   pl.BlockSpec((1, nblk, C_HEADS, tb), lambda b: (b, 0, 0, 0))],
        out_shape=[jax.ShapeDtypeStruct((batch * seq, LANES), F32),
                   jax.ShapeDtypeStruct((batch, nblk, C_HEADS, tb), F32)],
        compiler_params=_params(("arbitrary",)), name="fox_cumgate",
    )(f, bpad)


def _fox_kernel(q_ref, k_ref, v_ref, fq_ref, ft_ref, o_ref, *, tb):
    g = pl.program_id(1)
    qi = pl.program_id(2)
    hd = HEAD_DIM
    lane = lax.broadcasted_iota(I32, (tb, LANES), 1)
    causal = _causal(tb, tb)
    for hh in range(2):
        head = g * 2 + hh
        fq = jnp.sum(jnp.where(lane == head, fq_ref[...], 0.0), axis=-1, keepdims=True)
        q = q_ref[:, hh * hd:(hh + 1) * hd] * 0.125

        def step(j, carry, masked, q=q, fq=fq, hh=hh):
            kb = k_ref[pl.ds(j * tb, tb), hh * hd:(hh + 1) * hd]
            vb = v_ref[pl.ds(j * tb, tb), hh * hd:(hh + 1) * hd]
            s = lax.dot_general(q, kb, _NT, preferred_element_type=F32)
            s = s + (fq - ft_ref[0, 0, j, hh:hh + 1, :])
            if masked:
                s = jnp.where(causal, s, NEG_INF)
            return _softmax_step(s, vb, carry)

        carry = lax.fori_loop(0, qi, lambda j, c: step(j, c, False), _softmax_init(tb, hd))
        m, l, acc = step(qi, carry, True)
        o_ref[:, hh * hd:(hh + 1) * hd] = (acc / l).astype(o_ref.dtype)


def _fox_attention(q, k, v, fq, ft, batch, seq):
    tb = ATT_BLOCK
    nq = seq // tb
    pairs = C_HEADS // 2
    ftp = ft.reshape(batch, nq, pairs, 2, tb).transpose(0, 2, 1, 3, 4)
    ftp = jnp.pad(ftp, ((0, 0), (0, 0), (0, 0), (0, 6), (0, 0)))
    return pl.pallas_call(
        functools.partial(_fox_kernel, tb=tb),
        grid=(batch, pairs, nq),
        in_specs=[pl.BlockSpec((tb, LANES), lambda b, g, i: (b * nq + i, g)),
                  pl.BlockSpec((seq, LANES), lambda b, g, i: (b, g)),
                  pl.BlockSpec((seq, LANES), lambda b, g, i: (b, g)),
                  pl.BlockSpec((tb, LANES), lambda b, g, i: (b * nq + i, 0)),
                  pl.BlockSpec((1, 1, nq, 8, tb), lambda b, g, i: (b, g, 0, 0, 0))],
        out_specs=pl.BlockSpec((tb, LANES), lambda b, g, i: (b * nq + i, g)),
        out_shape=jax.ShapeDtypeStruct((batch * seq, C_HEADS * HEAD_DIM), BF16),
        compiler_params=_params(("arbitrary", "arbitrary", "arbitrary")), name="fox_attention",
    )(q, k, v, fq, ftp)


def _diff_kernel(rb_ref, q_ref, k_ref, v_ref, tiles_ref, lam_ref, g_ref, o_ref, *, tb, lam_init):
    h = pl.program_id(1)
    qi = pl.program_id(2)
    hd = HEAD_DIM
    causal = _causal(tb, tb)
    cfar = rb_ref[REL_BUCKETS - 1, A_HEADS + h]
    lv = lam_ref[...]
    lam = (jnp.exp(jnp.sum(lv[0:1] * lv[1:2], axis=-1, keepdims=True))
           - jnp.exp(jnp.sum(lv[2:3] * lv[3:4], axis=-1, keepdims=True)) + lam_init)
    res = []
    for s_ in range(2):
        q = q_ref[:, s_ * hd:(s_ + 1) * hd] * 0.125

        def step(j, carry, bias, masked, q=q, s_=s_):
            kb = k_ref[pl.ds(j * tb, tb), s_ * hd:(s_ + 1) * hd]
            vb = v_ref[pl.ds(j * tb, tb), :]
            s = lax.dot_general(q, kb, _NT, preferred_element_type=F32) + bias
            if masked:
                s = jnp.where(causal, s, NEG_INF)
            return _softmax_step(s, vb, carry)

        carry = lax.fori_loop(0, jnp.maximum(qi - 1, 0), lambda j, c: step(j, c, cfar, False),
                              _softmax_init(tb, 2 * hd))
        carry = lax.cond(qi >= 1, lambda c: step(qi - 1, c, tiles_ref[0, 1], False), lambda c: c, carry)
        m, l, acc = step(qi, carry, tiles_ref[0, 0], True)
        res.append(acc / l)
    o = res[0] - lam * res[1]
    o = _rms(o, g_ref[...]) * (1.0 - lam_init)
    o_ref[...] = o.astype(o_ref.dtype)


def _diff_attention(rel_bias, q, k, v, tiles, lam_vec, subln, batch, seq, lam_init):
    tb = ATT_BLOCK
    nq = seq // tb
    return pl.pallas_call(
        functools.partial(_diff_kernel, tb=tb, lam_init=lam_init),
        grid=(batch, B_HEADS, nq),
        in_specs=[pl.BlockSpec(memory_space=pltpu.SMEM),
                  pl.BlockSpec((tb, LANES), lambda b, h, i: (b * nq + i, h)),
                  pl.BlockSpec((seq, LANES), lambda b, h, i: (b, h)),
                  pl.BlockSpec((seq, LANES), lambda b, h, i: (b, h)),
                  pl.BlockSpec((1, 2, tb, tb), lambda b, h, i: (A_HEADS + h, 0, 0, 0)),
                  pl.BlockSpec((4, HEAD_DIM), lambda b, h, i: (0, 0)),
                  pl.BlockSpec((1, 2 * HEAD_DIM), lambda b, h, i: (0, 0))],
        out_specs=pl.BlockSpec((tb, LANES), lambda b, h, i: (b * nq + i, h)),
        out_shape=jax.ShapeDtypeStruct((batch * seq, B_HEADS * 2 * HEAD_DIM), BF16),
        compiler_params=_params(("arbitrary", "arbitrary", "arbitrary")), name="diff_attention",
    )(rel_bias, q, k, v, tiles, lam_vec, subln.reshape(1, -1))


def _dsa_kernel(rb_ref, qa_ref, qi_ref, wi_ref, ki_ref, ckv_ref, wuk_ref, wuv_ref, tiles_ref, o_ref,
                keys_ref, selb_ref, qlat_ref, *, tb, topk):
    qb = pl.program_id(1)
    nblk = qb + 1
    hd = HEAD_DIM
    causal = _causal(tb, tb)

    for h in range(A_HEADS):
        ql = jnp.dot(qa_ref[:, h * hd:(h + 1) * hd], wuk_ref[h], preferred_element_type=F32)
        qlat_ref[h] = (ql * 0.125).astype(BF16)

    wcols = [wi_ref[:, IDX_DIM + h:IDX_DIM + h + 1] * (IDX_HEADS ** -0.5) * 0.125
             for h in range(IDX_HEADS)]

    def index_keys(j, masked):
        kb = ki_ref[pl.ds(j * tb, tb), 0:IDX_DIM]
        isc = jnp.zeros((tb, tb), F32)
        for h in range(IDX_HEADS):
            li = lax.dot_general(qi_ref[:, h * IDX_DIM:(h + 1) * IDX_DIM], kb, _NT,
                                 preferred_element_type=F32)
            isc = isc + jnp.maximum(li, 0.0) * wcols[h]
        isc = jnp.where(isc == 0.0, 0.0, isc)
        bits = pltpu.bitcast(isc, I32)
        key = bits ^ ((bits >> 31) & 0x7FFFFFFF)
        if masked:
            key = jnp.where(causal, key, INT_MIN)
        keys_ref[j] = key

    def p1(j, c):
        index_keys(j, False)
        return c

    lax.fori_loop(0, qb, p1, 0)
    index_keys(qb, True)

    def count(pred):
        def body(j, acc):
            return acc + jnp.where(pred(keys_ref[j]), 1.0, 0.0)
        acc = lax.fori_loop(0, nblk, body, jnp.zeros((tb, tb), F32))
        return jnp.sum(acc, axis=-1, keepdims=True)

    kth = jnp.where(count(lambda k: k >= 0) >= topk, 0, INT_MIN).astype(I32)

    def bs(i, kth):
        cand = kth | lax.shift_left(jnp.int32(1), 30 - i)
        return jnp.where(count(lambda k: k >= cand) >= topk, cand, kth)

    kth = lax.fori_loop(0, 31, bs, kth)
    need = topk - count(lambda k: k > kth)

    r_ = lax.broadcasted_iota(I32, (tb, tb), 0)
    c_ = lax.broadcasted_iota(I32, (tb, tb), 1)
    upper = jnp.where(r_ <= c_, 1.0, 0.0).astype(BF16)

    def mask_block(j, seen, masked):
        key = keys_ref[j]
        eq = key == kth
        pre = jnp.dot(jnp.where(eq, 1.0, 0.0).astype(BF16), upper, preferred_element_type=F32)
        sel = (key > kth) | (eq & (pre + seen <= need))
        if masked:
            sel = sel & causal
        selb_ref[j] = jnp.where(sel, 0.0, NEG_INF)
        return seen + pre[:, tb - 1:tb]

    seen = lax.fori_loop(0, qb, lambda j, s: mask_block(j, s, False), jnp.zeros((tb, 1), F32))
    mask_block(qb, seen, True)

    for h in range(A_HEADS):
        q = qlat_ref[h]
        cfar = rb_ref[REL_BUCKETS - 1, h]

        def step(j, carry, bias, q=q):
            kvb = ckv_ref[pl.ds(j * tb, tb), :]
            s = lax.dot_general(q, kvb, _NT, preferred_element_type=F32) + bias + selb_ref[j]
            return _softmax_step(s, kvb, carry)

        carry = lax.fori_loop(0, jnp.maximum(qb - 1, 0), lambda j, c: step(j, c, cfar),
                              _softmax_init(tb, A_LATENT))
        carry = lax.cond(qb >= 1, lambda c, h=h: step(qb - 1, c, tiles_ref[h, 1]), lambda c: c, carry)
        m, l, acc = step(qb, carry, tiles_ref[h, 0])
        o_lat = (acc / l).astype(BF16)
        o_ref[:, h * hd:(h + 1) * hd] = jnp.dot(o_lat, wuv_ref[h], preferred_element_type=F32).astype(o_ref.dtype)


def _dsa_attention(rel_bias, q_a, q_i, kw_f32, kw_bf16, ckv, wuk, wuv, tiles, batch, seq):
    tb = ATT_BLOCK
    nq = seq // tb
    topk = min(TOPK_MAX, seq // 4)
    nqa = A_HEADS * HEAD_DIM
    nqi = IDX_HEADS * IDX_DIM
    return pl.pallas_call(
        functools.partial(_dsa_kernel, tb=tb, topk=topk),
        grid=(batch, nq),
        in_specs=[pl.BlockSpec(memory_space=pltpu.SMEM),
                  pl.BlockSpec((tb, nqa), lambda b, i: (b * nq + i, 0)),
                  pl.BlockSpec((tb, nqi), lambda b, i: (b * nq + i, 0)),
                  pl.BlockSpec((tb, LANES), lambda b, i: (b * nq + i, 0)),
                  pl.BlockSpec((seq, LANES), lambda b, i: (b, 0)),
                  pl.BlockSpec((seq, A_LATENT), lambda b, i: (b, 0)),
                  pl.BlockSpec(wuk.shape, lambda b, i: (0, 0, 0)),
                  pl.BlockSpec(wuv.shape, lambda b, i: (0, 0, 0)),
                  pl.BlockSpec((A_HEADS, 2, tb, tb), lambda b, i: (0, 0, 0, 0))],
        out_specs=pl.BlockSpec((tb, nqa), lambda b, i: (b * nq + i, 0)),
        out_shape=jax.ShapeDtypeStruct((batch * seq, nqa), BF16),
        scratch_shapes=[pltpu.VMEM((nq, tb, tb), I32), pltpu.VMEM((nq, tb, tb), F32),
                        pltpu.VMEM((A_HEADS, tb, A_LATENT), BF16)],
        compiler_params=_params(("arbitrary", "arbitrary")), name="dsa_attention",
    )(rel_bias, q_a, q_i, kw_f32, kw_bf16, ckv, wuk, wuv, tiles)


def _top_sorted(x, k):
    rows = lax.broadcasted_iota(I32, (k, x.shape[1]), 0)
    out = jnp.zeros((k, x.shape[1]), F32)
    rank = jnp.full(x.shape, 127.0, F32)
    for r in range(k):
        m = jnp.max(x, axis=0, keepdims=True)
        hit = x == m
        out = jnp.where(rows == r, m, out)
        rank = jnp.where(hit, float(r), rank)
        x = jnp.where(hit, -jnp.inf, x)
    return out, rank


def _router_kernel(q_ref, keys_ref, c1_ref, a_ref, r2_ref, w_ref, *, tr):
    k = PEER_TOPK
    nk = keys_ref.shape[1]
    row8 = lax.broadcasted_iota(I32, (8, LANES), 0)
    for h in range(PEER_HEADS):
        for tc in range(tr // LANES):
            tok = slice(tc * LANES, (tc + 1) * LANES)
            qh = q_ref[tok, :]
            s1 = lax.dot_general(keys_ref[0], qh[:, (2 * h) * nk:(2 * h + 1) * nk], _NT,
                                 preferred_element_type=F32)
            s2 = lax.dot_general(keys_ref[1], qh[:, (2 * h + 1) * nk:(2 * h + 2) * nk], _NT,
                                 preferred_element_type=F32)
            a, rank1 = _top_sorted(s1, k)
            b, rank2 = _top_sorted(s2, k)
            b8 = b[0:8]
            parts = [a[0:1] + b, a[1:2] + b8]
            for i, lim in ((2, 5), (3, 4), (4, 3), (5, 2), (6, 2), (7, 2)):
                parts.append(jnp.where(row8 < lim, a[i:i + 1] + b8, -jnp.inf))
            parts.append(a[8:16] + b[0:1])
            cand = jnp.concatenate(parts, axis=0)
            x = cand
            thr = None
            for _ in range(k):
                thr = jnp.max(x, axis=0, keepdims=True)
                x = jnp.where(x == thr, -jnp.inf, x)
            mx = a[0:1] + b[0:1]
            z = jnp.sum(jnp.where(cand >= thr, jnp.exp(cand - mx), 0.0), axis=0, keepdims=True)
            c1 = jnp.zeros((nk, LANES), F32)
            for r in range(k):
                cnt = jnp.sum(jnp.where(a[r:r + 1] + b >= thr, 1.0, 0.0), axis=0, keepdims=True)
                c1 = jnp.where(rank1 == float(r), cnt, c1)
            c1_ref[h, :, tok] = c1
            a_ref[h, :, tok] = jnp.exp(s1 - a[0:1]) / z
            r2_ref[h, :, tok] = rank2
            w_ref[h, :, tok] = jnp.exp(s2 - b[0:1])


def _router(q, sub_keys):
    t = q.shape[0]
    tr = ROUTER_TILE
    nk = sub_keys.shape[1]
    shp = jax.ShapeDtypeStruct((PEER_HEADS, nk, t), F32)
    spec = pl.BlockSpec((PEER_HEADS, nk, tr), lambda i: (0, 0, i))
    return pl.pallas_call(
        functools.partial(_router_kernel, tr=tr),
        grid=(t // tr,),
        in_specs=[pl.BlockSpec((tr, q.shape[1]), lambda i: (i, 0)),
                  pl.BlockSpec(sub_keys.shape, lambda i: (0, 0, 0))],
        out_specs=[spec] * 4, out_shape=[shp] * 4,
        compiler_params=_params(("arbitrary",)), name="peer_router",
    )(q, sub_keys)


def _peer_kernel(h_ref, u_ref, vt_ref, c1_ref, a_ref, r2_ref, w_ref, x_ref, g_ref, *rest,
                 tm, te, nk, final):
    if final:
        gf_ref, o_ref, p_ref, acc_ref = rest
    else:
        o_ref, p_ref, acc_ref = rest
    e = pl.program_id(1)
    rows_per = te // nk

    @pl.when(e == 0)
    def _():
        acc_ref[...] = jnp.zeros_like(acc_ref)

    act = lax.dot_general(u_ref[...], h_ref[...], _NT, preferred_element_type=F32)
    for ii in range(rows_per):
        for tc in range(tm // LANES):
            tok = slice(tc * LANES, (tc + 1) * LANES)
            gate = jnp.zeros((nk, LANES), F32)
            for h in range(PEER_HEADS):
                sel = r2_ref[h, :, tok] < c1_ref[h, ii:ii + 1, tok]
                gate = gate + jnp.where(sel, w_ref[h, :, tok] * a_ref[h, ii:ii + 1, tok], 0.0)
            p = gate * jax.nn.gelu(act[ii * nk:(ii + 1) * nk, tok])
            p_ref[ii * nk:(ii + 1) * nk, tok] = p.astype(BF16)
    acc_ref[...] += jnp.dot(vt_ref[...], p_ref[...], preferred_element_type=F32)

    @pl.when(e == pl.num_programs(1) - 1)
    def _():
        xn = x_ref[...] + g_ref[0] * acc_ref[...].T
        if final:
            xn = _rms(xn, gf_ref[...])
        o_ref[...] = xn


def _peer(h, u, vt, c1, a, r2, w, x, g2, seq, final_gain=None):
    t, d = x.shape
    tm, te = PEER_TOKEN_TILE, PEER_EXPERT_TILE
    n_exp = u.shape[0]
    nk = r2.shape[1]
    rows_per = te // nk
    per_b = seq // tm
    final = final_gain is not None
    in_specs = [pl.BlockSpec((tm, d), lambda i, e: (i, 0)),
                pl.BlockSpec((te, d), lambda i, e: (e, 0)),
                pl.BlockSpec((d, te), lambda i, e: (0, e)),
                pl.BlockSpec((PEER_HEADS, rows_per, tm), lambda i, e: (0, e, i)),
                pl.BlockSpec((PEER_HEADS, rows_per, tm), lambda i, e: (0, e, i)),
                pl.BlockSpec((PEER_HEADS, nk, tm), lambda i, e: (0, 0, i)),
                pl.BlockSpec((PEER_HEADS, nk, tm), lambda i, e: (0, 0, i)),
                pl.BlockSpec((tm, d), lambda i, e: (i, 0)),
                pl.BlockSpec((1, 1, d), lambda i, e: (i // per_b, 0, 0))]
    args = [h, u, vt, c1, a, r2, w, x, g2]
    if final:
        in_specs.append(pl.BlockSpec((1, d), lambda i, e: (0, 0)))
        args.append(final_gain.reshape(1, d))
    return pl.pallas_call(
        functools.partial(_peer_kernel, tm=tm, te=te, nk=nk, final=final),
        grid=(t // tm, n_exp // te),
        in_specs=in_specs,
        out_specs=pl.BlockSpec((tm, d), lambda i, e: (i, 0)),
        out_shape=jax.ShapeDtypeStruct((t, d), F32),
        scratch_shapes=[pltpu.VMEM((te, tm), BF16), pltpu.VMEM((d, tm), F32)],
        compiler_params=_params(("arbitrary", "arbitrary")), name="peer_experts",
    )(*args)


def _pad_cols(w, n):
    return jnp.pad(w, ((0, 0), (0, n - w.shape[1])))


def _even_w_in(w):
    na, ni, nb = A_HEADS * HEAD_DIM, IDX_HEADS * IDX_DIM, B_HEADS * 2 * HEAD_DIM
    o = np.cumsum([0, na, A_LATENT, ni, IDX_DIM, IDX_HEADS, nb, nb, nb])
    kw = _pad_cols(w[:, o[3]:o[5]], LANES)
    return jnp.concatenate([w[:, o[0]:o[3]], kw, w[:, o[5]:o[8]]], axis=1).astype(BF16)


def kernel(x, c, rel_bias, ada_w, ada_b, norm_mix, norm_ffn, norm_final, even_w_in, even_w_out,
           a_kv_norm, a_w_uk, a_w_uv, b_lambda, b_subln, odd_w_in, odd_b_forget, odd_w_out,
           peer_w_q, peer_sub_keys, peer_u, peer_v):
    batch, seq, d = x.shape
    depth = ada_w.shape[0]
    t = batch * seq
    assert seq % ROW_TILE == 0 and seq % ATT_BLOCK == 0 and t % PEER_TOKEN_TILE == 0

    mod = _ada(c, ada_w, ada_b)
    tiles = _bias_tiles(rel_bias)

    na, ni, nb = A_HEADS * HEAD_DIM, IDX_HEADS * IDX_DIM, B_HEADS * 2 * HEAD_DIM
    nc = C_HEADS * HEAD_DIM
    o = np.cumsum([0, na, A_LATENT, ni, LANES, nb, nb, nb])
    even_segs = [(o[0], o[1], None, [BF16]), (o[1], o[2], "kvnorm", [BF16]), (o[2], o[3], None, [BF16]),
                 (o[3], o[4], None, [F32, BF16]), (o[4], o[5], None, [BF16]), (o[5], o[6], None, [BF16]),
                 (o[6], o[7], None, [BF16])]
    odd_segs = [(0, nc, None, [BF16]), (nc, 2 * nc, None, [BF16]), (2 * nc, 3 * nc, None, [BF16]),
                (3 * nc, 3 * nc + LANES, None, [F32])]
    nq_peer = peer_w_q.shape[2]
    peer_segs = [(0, nq_peer, None, [BF16])]

    xt = x.reshape(t, d)
    for layer in range(depth):
        m6 = mod[layer].reshape(batch, 6, 1, d)
        sh1, sc1, g1, sh2, sc2, g2 = [m6[:, i] for i in range(6)]
        if layer % 2 == 0:
            e = layer // 2
            lam_init = 0.8 - 0.6 * math.exp(-0.3 * layer)
            q_a, ckv, q_i, kw32, kw16, q_b, k_b, v_b = _normproj(
                xt, norm_mix[layer], sc1, sh1, _even_w_in(even_w_in[e]), even_segs, seq,
                kvn=a_kv_norm[e], name="even_in_proj")
            wuk = jnp.transpose(a_w_uk[e], (1, 2, 0)).astype(BF16)
            wuv = jnp.transpose(a_w_uv[e], (1, 0, 2)).astype(BF16)
            o_a = _dsa_attention(rel_bias, q_a, q_i, kw32, kw16, ckv, wuk, wuv, tiles, batch, seq)
            o_b = _diff_attention(rel_bias, q_b, k_b, v_b, tiles, b_lambda[e], b_subln[e], batch, seq, lam_init)
            mixes = [o_a, o_b]
            w_out = even_w_out[e].astype(BF16)
        else:
            od = layer // 2
            w_in = _pad_cols(odd_w_in[od], 3 * nc + LANES).astype(BF16)
            q, k, v, f = _normproj(xt, norm_mix[layer], sc1, sh1, w_in, odd_segs, seq, name="odd_in_proj")
            fq, ft = _cumgate(f, odd_b_forget[od], batch, seq)
            mixes = [_fox_attention(q, k, v, fq, ft, batch, seq)]
            w_out = odd_w_out[od].astype(BF16)
        xt, h2, qp = _resproj(xt, mixes, w_out, g1, norm_ffn[layer], sc2, sh2,
                              peer_w_q[layer].astype(BF16), peer_segs, seq)
        c1, a, r2, w = _router(qp, peer_sub_keys[layer].astype(BF16))
        xt = _peer(h2, peer_u[layer].astype(BF16), peer_v[layer].T.astype(BF16), c1, a, r2, w,
                   xt, g2, seq, final_gain=norm_final if layer == depth - 1 else None)
    return xt.reshape(batch, seq, d)
```

```python
import functools
import math

import numpy as np
import jax
import jax.numpy as jnp
from jax import lax
from jax.experimental import pallas as pl
from jax.experimental.pallas import tpu as pltpu

F32 = jnp.float32
BF16 = jnp.bfloat16
I32 = jnp.int32

HEAD_DIM = 64
RMS_EPS = 1e-6
NEG_INF = -1e30
A_HEADS = 8
A_LATENT = 256
IDX_HEADS = 8
IDX_DIM = 64
TOPK_MAX = 256
B_HEADS = 4
C_HEADS = 16
REL_BUCKETS = 32
REL_MAX_DIST = 128
PEER_HEADS = 8
PEER_TOPK = 16

LANES = 128
BF16_ROWS = 16
ATT_BLOCK = 256
ROW_TILE = 512
PEER_TOKEN_TILE = 512
PEER_EXPERT_TILE = 1024
ROUTER_TILE = 512
VMEM_LIMIT = 56 * 1024 * 1024
INT_MIN = -2 ** 31

_NT = (((1,), (1,)), ((), ()))


def _params(sem):
    return pltpu.CompilerParams(dimension_semantics=sem, vmem_limit_bytes=VMEM_LIMIT)


def _rms(x, g):
    return x * lax.rsqrt(jnp.mean(x * x, axis=-1, keepdims=True) + RMS_EPS) * g


def _ada_kernel(c_ref, w_ref, b_ref, o_ref):
    c = c_ref[...]
    ca = c * jax.nn.sigmoid(c)
    o_ref[0] = jnp.dot(ca, w_ref[0], precision=lax.Precision.HIGHEST,
                       preferred_element_type=F32) + b_ref[0]


def _ada(c, ada_w, ada_b):
    depth, d, n = ada_w.shape
    b = c.shape[0]
    bp = 8
    cp = jnp.zeros((bp, d), F32).at[:b].set(c)
    tn = 1536
    out = pl.pallas_call(
        _ada_kernel,
        grid=(depth, n // tn),
        in_specs=[pl.BlockSpec((bp, d), lambda l, j: (0, 0)),
                  pl.BlockSpec((1, d, tn), lambda l, j: (l, 0, j)),
                  pl.BlockSpec((1, 1, tn), lambda l, j: (l, 0, j))],
        out_specs=pl.BlockSpec((1, bp, tn), lambda l, j: (l, 0, j)),
        out_shape=jax.ShapeDtypeStruct((depth, bp, n), F32),
        compiler_params=_params(("arbitrary", "arbitrary")),
        name="ada_mod",
    )(cp, ada_w, ada_b.reshape(depth, 1, n))
    return out[:, :b]


def _emit_segments(hb, w_ref, segs, outs, kvn_ref):
    k = 0
    for (a, b, post, dtypes) in segs:
        r = jnp.dot(hb, w_ref[:, a:b], preferred_element_type=F32)
        if post == "kvnorm":
            r = _rms(r, kvn_ref[...])
        for dt in dtypes:
            outs[k][...] = r.astype(dt)
            k += 1


def _normproj_kernel(*refs, segs, has_kvn, emit_h):
    x_ref, gain_ref, sc_ref, sh_ref, w_ref = refs[:5]
    pos = 5
    kvn_ref = None
    if has_kvn:
        kvn_ref = refs[pos]
        pos += 1
    outs = list(refs[pos:])
    h = _rms(x_ref[...], gain_ref[...]) * (1.0 + sc_ref[0]) + sh_ref[0]
    hb = h.astype(BF16)
    if emit_h:
        outs[0][...] = hb
        outs = outs[1:]
    _emit_segments(hb, w_ref, segs, outs, kvn_ref)


def _resproj_kernel(*refs, n_mix, segs):
    x_ref = refs[0]
    mix_refs = refs[1:1 + n_mix]
    wo_ref, g_ref, gain_ref, sc_ref, sh_ref, w_ref = refs[1 + n_mix:7 + n_mix]
    outs = list(refs[7 + n_mix:])
    y = None
    off = 0
    for m in mix_refs:
        kdim = m.shape[1]
        t = jnp.dot(m[...], wo_ref[off:off + kdim, :], preferred_element_type=F32)
        y = t if y is None else y + t
        off += kdim
    xn = x_ref[...] + g_ref[0] * y
    outs[0][...] = xn
    h = _rms(xn, gain_ref[...]) * (1.0 + sc_ref[0]) + sh_ref[0]
    hb = h.astype(BF16)
    outs[1][...] = hb
    _emit_segments(hb, w_ref, segs, outs[2:], None)


def _seg_out_shapes(t, segs):
    shapes, specs = [], []
    for (a, b, _, dtypes) in segs:
        for dt in dtypes:
            shapes.append(jax.ShapeDtypeStruct((t, b - a), dt))
            specs.append(pl.BlockSpec((ROW_TILE, b - a), lambda i: (i, 0)))
    return shapes, specs


def _normproj(x, gain, sc, sh, w, segs, seq, kvn=None, emit_h=False, name="normproj"):
    t, d = x.shape
    tm = ROW_TILE
    per_b = seq // tm
    n = w.shape[1]
    mod_spec = pl.BlockSpec((1, 1, d), lambda i: (i // per_b, 0, 0))
    in_specs = [pl.BlockSpec((tm, d), lambda i: (i, 0)),
                pl.BlockSpec((1, d), lambda i: (0, 0)), mod_spec, mod_spec,
                pl.BlockSpec((d, n), lambda i: (0, 0))]
    args = [x, gain.reshape(1, d), sc, sh, w]
    if kvn is not None:
        in_specs.append(pl.BlockSpec((1, kvn.shape[-1]), lambda i: (0, 0)))
        args.append(kvn.reshape(1, -1))
    shapes, specs = _seg_out_shapes(t, segs)
    if emit_h:
        shapes = [jax.ShapeDtypeStruct((t, d), BF16)] + shapes
        specs = [pl.BlockSpec((tm, d), lambda i: (i, 0))] + specs
    return pl.pallas_call(
        functools.partial(_normproj_kernel, segs=segs, has_kvn=kvn is not None, emit_h=emit_h),
        grid=(t // tm,), in_specs=in_specs, out_specs=specs, out_shape=shapes,
        compiler_params=_params(("arbitrary",)), name=name,
    )(*args)


def _resproj(x, mixes, w_out, g, gain, sc, sh, w, segs, seq, name="resproj"):
    t, d = x.shape
    tm = ROW_TILE
    per_b = seq // tm
    n = w.shape[1]
    mod_spec = pl.BlockSpec((1, 1, d), lambda i: (i // per_b, 0, 0))
    in_specs = [pl.BlockSpec((tm, d), lambda i: (i, 0))]
    in_specs += [pl.BlockSpec((tm, m.shape[1]), lambda i: (i, 0)) for m in mixes]
    in_specs += [pl.BlockSpec(w_out.shape, lambda i: (0, 0)), mod_spec,
                 pl.BlockSpec((1, d), lambda i: (0, 0)), mod_spec, mod_spec,
                 pl.BlockSpec((d, n), lambda i: (0, 0))]
    shapes, specs = _seg_out_shapes(t, segs)
    shapes = [jax.ShapeDtypeStruct((t, d), F32), jax.ShapeDtypeStruct((t, d), BF16)] + shapes
    specs = [pl.BlockSpec((tm, d), lambda i: (i, 0)), pl.BlockSpec((tm, d), lambda i: (i, 0))] + specs
    return pl.pallas_call(
        functools.partial(_resproj_kernel, n_mix=len(mixes), segs=segs),
        grid=(t // tm,), in_specs=in_specs, out_specs=specs, out_shape=shapes,
        compiler_params=_params(("arbitrary",)), name=name,
    )(x, *mixes, w_out, g, gain.reshape(1, d), sc, sh, w)


def _bucket_table(n):
    max_exact = REL_BUCKETS // 2
    d = np.arange(n)
    df = np.maximum(d, 1).astype(np.float32)
    large = max_exact + (np.log(df / max_exact) / math.log(REL_MAX_DIST / max_exact)
                         * (REL_BUCKETS - max_exact)).astype(np.int32)
    large = np.minimum(large, REL_BUCKETS - 1)
    return np.where(d < max_exact, d, large).astype(np.int32)


def _bias_tiles_kernel(rb_ref, bk_ref, o_ref):
    h = pl.program_id(0)
    for t in range(2):
        bt = bk_ref[t]
        acc = jnp.zeros(bt.shape, F32)
        for b in range(REL_BUCKETS):
            acc = jnp.where(bt == b, rb_ref[b, h], acc)
        o_ref[0, t] = acc


def _bias_tiles(rel_bias):
    tb = ATT_BLOCK
    nh = rel_bias.shape[1]
    table = _bucket_table(2 * tb)
    r = np.arange(tb)[:, None]
    c = np.arange(tb)[None, :]
    bk = np.stack([table[np.maximum(r - c, 0)], table[tb + r - c]]).astype(np.int32)
    return pl.pallas_call(
        _bias_tiles_kernel,
        grid=(nh,),
        in_specs=[pl.BlockSpec(memory_space=pltpu.SMEM),
                  pl.BlockSpec((2, tb, tb), lambda h: (0, 0, 0))],
        out_specs=pl.BlockSpec((1, 2, tb, tb), lambda h: (h, 0, 0, 0)),
        out_shape=jax.ShapeDtypeStruct((nh, 2, tb, tb), F32),
        compiler_params=_params(("arbitrary",)), name="bias_tiles",
    )(rel_bias, jnp.asarray(bk))


def _softmax_step(s, vb, carry):
    m, l, acc = carry
    m_new = jnp.maximum(m, jnp.max(s, axis=-1, keepdims=True))
    alpha = jnp.exp(m - m_new)
    p = jnp.exp(s - m_new)
    l = alpha * l + jnp.sum(p, axis=-1, keepdims=True)
    acc = alpha * acc + jnp.dot(p.astype(BF16), vb, preferred_element_type=F32)
    return m_new, l, acc


def _softmax_init(tq, dv):
    return (jnp.full((tq, 1), NEG_INF, F32), jnp.zeros((tq, 1), F32), jnp.zeros((tq, dv), F32))


def _causal(tq, tk):
    row = lax.broadcasted_iota(I32, (tq, tk), 0)
    col = lax.broadcasted_iota(I32, (tq, tk), 1)
    return col <= row


def _cumgate_kernel(f_ref, b_ref, fo_ref, ft_ref, *, tb):
    nblk = f_ref.shape[0] // tb
    row = lax.broadcasted_iota(I32, (tb, tb), 0)
    col = lax.broadcasted_iota(I32, (tb, tb), 1)
    tri = jnp.where(col <= row, 1.0, 0.0).astype(F32)

    def body(j, carry):
        z = f_ref[pl.ds(j * tb, tb), :] + b_ref[...]
        ls = -(jnp.maximum(-z, 0.0) + jnp.log(1.0 + jnp.exp(-jnp.abs(z))))
        cs = jnp.dot(tri, ls, precision=lax.Precision.HIGHEST, preferred_element_type=F32) + carry
        fo_ref[pl.ds(j * tb, tb), :] = cs
        ft_ref[0, j] = cs.T[:C_HEADS, :]
        return cs[tb - 1:tb, :]

    lax.fori_loop(0, nblk, body, jnp.zeros((1, LANES), F32))


def _cumgate(f, b_forget, batch, seq):
    tb = ATT_BLOCK
    nblk = seq // tb
    bpad = jnp.zeros((1, LANES), F32).at[0, :C_HEADS].set(b_forget)
    return pl.pallas_call(
        functools.partial(_cumgate_kernel, tb=tb),
        grid=(batch,),
        in_specs=[pl.BlockSpec((seq, LANES), lambda b: (b, 0)),
                  pl.BlockSpec((1, LANES), lambda b: (0, 0))],
        out_specs=[pl.BlockSpec((seq, LANES), lambda b: (b, 0)),
                   pl.BlockSpec((1, nblk, C_HEADS, tb), lambda b: (b, 0, 0, 0))],
        out_shape=[jax.ShapeDtypeStruct((batch * seq, LANES), F32),
                   jax.ShapeDtypeStruct((batch, nblk, C_HEADS, tb), F32)],
        compiler_params=_params(("arbitrary",)), name="fox_cumgate",
    )(f, bpad)


def _fox_kernel(q_ref, k_ref, v_ref, fq_ref, ft_ref, o_ref, *, tb):
    g = pl.program_id(1)
    qi = pl.program_id(2)
    hd = HEAD_DIM
    lane = lax.broadcasted_iota(I32, (tb, LANES), 1)
    causal = _causal(tb, tb)
    for hh in range(2):
        head = g * 2 + hh
        fq = jnp.sum(jnp.where(lane == head, fq_ref[...], 0.0), axis=-1, keepdims=True)
        q = q_ref[:, hh * hd:(hh + 1) * hd] * 0.125

        def step(j, carry, masked, q=q, fq=fq, hh=hh):
            kb = k_ref[pl.ds(j * tb, tb), hh * hd:(hh + 1) * hd]
            vb = v_ref[pl.ds(j * tb, tb), hh * hd:(hh + 1) * hd]
            s = lax.dot_general(q, kb, _NT, preferred_element_type=F32)
            s = s + (fq - ft_ref[0, 0, j, hh:hh + 1, :])
            if masked:
                s = jnp.where(causal, s, NEG_INF)
            return _softmax_step(s, vb, carry)

        carry = lax.fori_loop(0, qi, lambda j, c: step(j, c, False), _softmax_init(tb, hd))
        m, l, acc = step(qi, carry, True)
        o_ref[:, hh * hd:(hh + 1) * hd] = (acc / l).astype(o_ref.dtype)


def _fox_attention(q, k, v, fq, ft, batch, seq):
    tb = ATT_BLOCK
    nq = seq // tb
    pairs = C_HEADS // 2
    ftp = ft.reshape(batch, nq, pairs, 2, tb).transpose(0, 2, 1, 3, 4)
    ftp = jnp.pad(ftp, ((0, 0), (0, 0), (0, 0), (0, 6), (0, 0)))
    return pl.pallas_call(
        functools.partial(_fox_kernel, tb=tb),
        grid=(batch, pairs, nq),
        in_specs=[pl.BlockSpec((tb, LANES), lambda b, g, i: (b * nq + i, g)),
                  pl.BlockSpec((seq, LANES), lambda b, g, i: (b, g)),
                  pl.BlockSpec((seq, LANES), lambda b, g, i: (b, g)),
                  pl.BlockSpec((tb, LANES), lambda b, g, i: (b * nq + i, 0)),
                  pl.BlockSpec((1, 1, nq, 8, tb), lambda b, g, i: (b, g, 0, 0, 0))],
        out_specs=pl.BlockSpec((tb, LANES), lambda b, g, i: (b * nq + i, g)),
        out_shape=jax.ShapeDtypeStruct((batch * seq, C_HEADS * HEAD_DIM), BF16),
        compiler_params=_params(("arbitrary", "arbitrary", "arbitrary")), name="fox_attention",
    )(q, k, v, fq, ftp)


def _diff_kernel(rb_ref, q_ref, k_ref, v_ref, tiles_ref, lam_ref, g_ref, o_ref, *, tb, lam_init):
    h = pl.program_id(1)
    qi = pl.program_id(2)
    hd = HEAD_DIM
    causal = _causal(tb, tb)
    cfar = rb_ref[REL_BUCKETS - 1, A_HEADS + h]
    lv = lam_ref[...]
    lam = (jnp.exp(jnp.sum(lv[0:1] * lv[1:2], axis=-1, keepdims=True))
           - jnp.exp(jnp.sum(lv[2:3] * lv[3:4], axis=-1, keepdims=True)) + lam_init)
    res = []
    for s_ in range(2):
        q = q_ref[:, s_ * hd:(s_ + 1) * hd] * 0.125

        def step(j, carry, bias, masked, q=q, s_=s_):
            kb = k_ref[pl.ds(j * tb, tb), s_ * hd:(s_ + 1) * hd]
            vb = v_ref[pl.ds(j * tb, tb), :]
            s = lax.dot_general(q, kb, _NT, preferred_element_type=F32) + bias
            if masked:
                s = jnp.where(causal, s, NEG_INF)
            return _softmax_step(s, vb, carry)

        carry = lax.fori_loop(0, jnp.maximum(qi - 1, 0), lambda j, c: step(j, c, cfar, False),
                              _softmax_init(tb, 2 * hd))
        carry = lax.cond(qi >= 1, lambda c: step(qi - 1, c, tiles_ref[0, 1], False), lambda c: c, carry)
        m, l, acc = step(qi, carry, tiles_ref[0, 0], True)
        res.append(acc / l)
    o = res[0] - lam * res[1]
    o = _rms(o, g_ref[...]) * (1.0 - lam_init)
    o_ref[...] = o.astype(o_ref.dtype)


def _diff_attention(rel_bias, q, k, v, tiles, lam_vec, subln, batch, seq, lam_init):
    tb = ATT_BLOCK
    nq = seq // tb
    return pl.pallas_call(
        functools.partial(_diff_kernel, tb=tb, lam_init=lam_init),
        grid=(batch, B_HEADS, nq),
        in_specs=[pl.BlockSpec(memory_space=pltpu.SMEM),
                  pl.BlockSpec((tb, LANES), lambda b, h, i: (b * nq + i, h)),
                  pl.BlockSpec((seq, LANES), lambda b, h, i: (b, h)),
                  pl.BlockSpec((seq, LANES), lambda b, h, i: (b, h)),
                  pl.BlockSpec((1, 2, tb, tb), lambda b, h, i: (A_HEADS + h, 0, 0, 0)),
                  pl.BlockSpec((4, HEAD_DIM), lambda b, h, i: (0, 0)),
                  pl.BlockSpec((1, 2 * HEAD_DIM), lambda b, h, i: (0, 0))],
        out_specs=pl.BlockSpec((tb, LANES), lambda b, h, i: (b * nq + i, h)),
        out_shape=jax.ShapeDtypeStruct((batch * seq, B_HEADS * 2 * HEAD_DIM), BF16),
        compiler_params=_params(("arbitrary", "arbitrary", "arbitrary")), name="diff_attention",
    )(rel_bias, q, k, v, tiles, lam_vec, subln.reshape(1, -1))


def _dsa_kernel(rb_ref, qa_ref, qi_ref, wi_ref, ki_ref, ckv_ref, wuk_ref, wuv_ref, tiles_ref, o_ref,
                keys_ref, selb_ref, qlat_ref, *, tb, topk):
    qb = pl.program_id(1)
    nblk = qb + 1
    hd = HEAD_DIM
    causal = _causal(tb, tb)

    for h in range(A_HEADS):
        ql = jnp.dot(qa_ref[:, h * hd:(h + 1) * hd], wuk_ref[h], preferred_element_type=F32)
        qlat_ref[h] = (ql * 0.125).astype(BF16)

    wcols = [wi_ref[:, IDX_DIM + h:IDX_DIM + h + 1] * (IDX_HEADS ** -0.5) * 0.125
             for h in range(IDX_HEADS)]

    def index_keys(j, masked):
        kb = ki_ref[pl.ds(j * tb, tb), 0:IDX_DIM]
        isc = jnp.zeros((tb, tb), F32)
        for h in range(IDX_HEADS):
            li = lax.dot_general(qi_ref[:, h * IDX_DIM:(h + 1) * IDX_DIM], kb, _NT,
                                 preferred_element_type=F32)
            isc = isc + jnp.maximum(li, 0.0) * wcols[h]
        isc = jnp.where(isc == 0.0, 0.0, isc)
        bits = pltpu.bitcast(isc, I32)
        key = bits ^ ((bits >> 31) & 0x7FFFFFFF)
        if masked:
            key = jnp.where(causal, key, INT_MIN)
        keys_ref[j] = key

    def p1(j, c):
        index_keys(j, False)
        return c

    lax.fori_loop(0, qb, p1, 0)
    index_keys(qb, True)

    def count(pred):
        def body(j, acc):
            return acc + jnp.where(pred(keys_ref[j]), 1.0, 0.0)
        acc = lax.fori_loop(0, nblk, body, jnp.zeros((tb, tb), F32))
        return jnp.sum(acc, axis=-1, keepdims=True)

    kth = jnp.where(count(lambda k: k >= 0) >= topk, 0, INT_MIN).astype(I32)

    def bs(i, kth):
        cand = kth | lax.shift_left(jnp.int32(1), 30 - i)
        return jnp.where(count(lambda k: k >= cand) >= topk, cand, kth)

    kth = lax.fori_loop(0, 31, bs, kth)
    need = topk - count(lambda k: k > kth)

    r_ = lax.broadcasted_iota(I32, (tb, tb), 0)
    c_ = lax.broadcasted_iota(I32, (tb, tb), 1)
    upper = jnp.where(r_ <= c_, 1.0, 0.0).astype(BF16)

    def mask_block(j, seen, masked):
        key = keys_ref[j]
        eq = key == kth
        pre = jnp.dot(jnp.where(eq, 1.0, 0.0).astype(BF16), upper, preferred_element_type=F32)
        sel = (key > kth) | (eq & (pre + seen <= need))
        if masked:
            sel = sel & causal
        selb_ref[j] = jnp.where(sel, 0.0, NEG_INF)
        return seen + pre[:, tb - 1:tb]

    seen = lax.fori_loop(0, qb, lambda j, s: mask_block(j, s, False), jnp.zeros((tb, 1), F32))
    mask_block(qb, seen, True)

    for h in range(A_HEADS):
        q = qlat_ref[h]
        cfar = rb_ref[REL_BUCKETS - 1, h]

        def step(j, carry, bias, q=q):
            kvb = ckv_ref[pl.ds(j * tb, tb), :]
            s = lax.dot_general(q, kvb, _NT, preferred_element_type=F32) + bias + selb_ref[j]
            return _softmax_step(s, kvb, carry)

        carry = lax.fori_loop(0, jnp.maximum(qb - 1, 0), lambda j, c: step(j, c, cfar),
                              _softmax_init(tb, A_LATENT))
        carry = lax.cond(qb >= 1, lambda c, h=h: step(qb - 1, c, tiles_ref[h, 1]), lambda c: c, carry)
        m, l, acc = step(qb, carry, tiles_ref[h, 0])
        o_lat = (acc / l).astype(BF16)
        o_ref[:, h * hd:(h + 1) * hd] = jnp.dot(o_lat, wuv_ref[h], preferred_element_type=F32).astype(o_ref.dtype)


def _dsa_attention(rel_bias, q_a, q_i, kw_f32, kw_bf16, ckv, wuk, wuv, tiles, batch, seq):
    tb = ATT_BLOCK
    nq = seq // tb
    topk = min(TOPK_MAX, seq // 4)
    nqa = A_HEADS * HEAD_DIM
    nqi = IDX_HEADS * IDX_DIM
    return pl.pallas_call(
        functools.partial(_dsa_kernel, tb=tb, topk=topk),
        grid=(batch, nq),
        in_specs=[pl.BlockSpec(memory_space=pltpu.SMEM),
                  pl.BlockSpec((tb, nqa), lambda b, i: (b * nq + i, 0)),
                  pl.BlockSpec((tb, nqi), lambda b, i: (b * nq + i, 0)),
                  pl.BlockSpec((tb, LANES), lambda b, i: (b * nq + i, 0)),
                  pl.BlockSpec((seq, LANES), lambda b, i: (b, 0)),
                  pl.BlockSpec((seq, A_LATENT), lambda b, i: (b, 0)),
                  pl.BlockSpec(wuk.shape, lambda b, i: (0, 0, 0)),
                  pl.BlockSpec(wuv.shape, lambda b, i: (0, 0, 0)),
                  pl.BlockSpec((A_HEADS, 2, tb, tb), lambda b, i: (0, 0, 0, 0))],
        out_specs=pl.BlockSpec((tb, nqa), lambda b, i: (b * nq + i, 0)),
        out_shape=jax.ShapeDtypeStruct((batch * seq, nqa), BF16),
        scratch_shapes=[pltpu.VMEM((nq, tb, tb), I32), pltpu.VMEM((nq, tb, tb), F32),
                        pltpu.VMEM((A_HEADS, tb, A_LATENT), BF16)],
        compiler_params=_params(("arbitrary", "arbitrary")), name="dsa_attention",
    )(rel_bias, q_a, q_i, kw_f32, kw_bf16, ckv, wuk, wuv, tiles)


def _top_sorted(x, k):
    rows = lax.broadcasted_iota(I32, (k, x.shape[1]), 0)
    out = jnp.zeros((k, x.shape[1]), F32)
    rank = jnp.full(x.shape, 127.0, F32)
    for r in range(k):
        m = jnp.max(x, axis=0, keepdims=True)
        hit = x == m
        out = jnp.where(rows == r, m, out)
        rank = jnp.where(hit, float(r), rank)
        x = jnp.where(hit, -jnp.inf, x)
    return out, rank


def _router_kernel(q_ref, keys_ref, c1_ref, a_ref, r2_ref, w_ref, *, tr):
    k = PEER_TOPK
    nk = keys_ref.shape[1]
    row8 = lax.broadcasted_iota(I32, (8, LANES), 0)
    for h in range(PEER_HEADS):
        for tc in range(tr // LANES):
            tok = slice(tc * LANES, (tc + 1) * LANES)
            qh = q_ref[tok, :]
            s1 = lax.dot_general(keys_ref[0], qh[:, (2 * h) * nk:(2 * h + 1) * nk], _NT,
                                 preferred_element_type=F32)
            s2 = lax.dot_general(keys_ref[1], qh[:, (2 * h + 1) * nk:(2 * h + 2) * nk], _NT,
                                 preferred_element_type=F32)
            a, rank1 = _top_sorted(s1, k)
            b, rank2 = _top_sorted(s2, k)
            b8 = b[0:8]
            parts = [a[0:1] + b, a[1:2] + b8]
            for i, lim in ((2, 5), (3, 4), (4, 3), (5, 2), (6, 2), (7, 2)):
                parts.append(jnp.where(row8 < lim, a[i:i + 1] + b8, -jnp.inf))
            parts.append(a[8:16] + b[0:1])
            cand = jnp.concatenate(parts, axis=0)
            x = cand
            thr = None
            for _ in range(k):
                thr = jnp.max(x, axis=0, keepdims=True)
                x = jnp.where(x == thr, -jnp.inf, x)
            mx = a[0:1] + b[0:1]
            z = jnp.sum(jnp.where(cand >= thr, jnp.exp(cand - mx), 0.0), axis=0, keepdims=True)
            c1 = jnp.zeros((nk, LANES), F32)
            for r in range(k):
                cnt = jnp.sum(jnp.where(a[r:r + 1] + b >= thr, 1.0, 0.0), axis=0, keepdims=True)
                c1 = jnp.where(rank1 == float(r), cnt, c1)
            c1_ref[h, :, tok] = c1
            a_ref[h, :, tok] = jnp.exp(s1 - a[0:1]) / z
            r2_ref[h, :, tok] = rank2.astype(BF16)
            w_ref[h, :, tok] = jnp.exp(s2 - b[0:1]).astype(BF16)


def _router(q, sub_keys):
    t = q.shape[0]
    tr = ROUTER_TILE
    nk = sub_keys.shape[1]
    shp32 = jax.ShapeDtypeStruct((PEER_HEADS, nk, t), F32)
    shp16 = jax.ShapeDtypeStruct((PEER_HEADS, nk, t), BF16)
    spec = pl.BlockSpec((PEER_HEADS, nk, tr), lambda i: (0, 0, i))
    return pl.pallas_call(
        functools.partial(_router_kernel, tr=tr),
        grid=(t // tr,),
        in_specs=[pl.BlockSpec((tr, q.shape[1]), lambda i: (i, 0)),
                  pl.BlockSpec(sub_keys.shape, lambda i: (0, 0, 0))],
        out_specs=[spec] * 4, out_shape=[shp32, shp32, shp16, shp16],
        compiler_params=_params(("arbitrary",)), name="peer_router",
    )(q, sub_keys)


def _peer_kernel(h_ref, u_ref, vt_ref, c1_ref, a_ref, r2_ref, w_ref, x_ref, g_ref, *rest,
                 tm, te, nk, ne, final):
    if final:
        gf_ref, o_ref, act0_ref, act1_ref, p_ref, acc_ref, cb_ref, ab_ref, r2s_ref, ws_ref = rest
    else:
        o_ref, act0_ref, act1_ref, p_ref, acc_ref, cb_ref, ab_ref, r2s_ref, ws_ref = rest
    s = pl.program_id(0)
    e0 = jnp.maximum(s - 1, 0) % ne
    rows_per = te // nk
    half = rows_per // 2

    @pl.when(s == 0)
    def _():
        act1_ref[...] = jnp.zeros_like(act1_ref)

    @pl.when(e0 == 0)
    def _():
        acc_ref[...] = jnp.zeros_like(acc_ref)
        r2s_ref[...] = r2_ref[...]
        ws_ref[...] = w_ref[...]

    for h in range(PEER_HEADS):
        for ii in range(rows_per):
            cb_ref[h, ii] = jnp.broadcast_to(c1_ref[h, ii:ii + 1, :], (BF16_ROWS, tm)).astype(BF16)
            ab_ref[h, ii] = jnp.broadcast_to(a_ref[h, ii:ii + 1, :], (BF16_ROWS, tm)).astype(BF16)

    def main(cur_ref, prev_ref):
        def gate_rows(ii_list):
            for tc in range(tm // LANES):
                tok = slice(tc * LANES, (tc + 1) * LANES)
                shape3 = (nk // BF16_ROWS, BF16_ROWS, LANES)
                gates = [jnp.zeros(shape3, BF16) for _ in ii_list]
                for h in range(PEER_HEADS):
                    r2 = r2s_ref[h, :, tok].reshape(shape3)
                    w2 = ws_ref[h, :, tok].reshape(shape3)
                    for n, ii in enumerate(ii_list):
                        c = cb_ref[h, ii, :, tok]
                        a = ab_ref[h, ii, :, tok]
                        gates[n] = gates[n] + jnp.where(r2 < c[None], w2 * a[None], 0)
                for n, ii in enumerate(ii_list):
                    x = prev_ref[ii * nk:(ii + 1) * nk, tok].reshape(shape3)
                    p_ref[ii * nk:(ii + 1) * nk, tok] = (gates[n] * jax.nn.gelu(x)).reshape(nk, LANES)

        def second(lo, hi):
            acc_ref[...] += jnp.dot(vt_ref[:, lo * nk:hi * nk], p_ref[lo * nk:hi * nk, :],
                                    preferred_element_type=F32)

        def first(lo, hi):
            cur_ref[lo * nk:hi * nk, :] = lax.dot_general(u_ref[lo * nk:hi * nk, :], h_ref[...], _NT,
                                                          preferred_element_type=F32).astype(BF16)

        gate_rows(list(range(0, half)))
        first(0, half)
        second(0, half)
        gate_rows(list(range(half, rows_per)))
        first(half, rows_per)
        second(half, rows_per)

    @pl.when(s % 2 == 0)
    def _():
        main(act0_ref, act1_ref)

    @pl.when(s % 2 == 1)
    def _():
        main(act1_ref, act0_ref)

    @pl.when((e0 == ne - 1) & (s > 0))
    def _():
        xn = x_ref[...] + g_ref[0] * acc_ref[...].T
        if final:
            xn = _rms(xn, gf_ref[...])
        o_ref[...] = xn


def _peer(h, u, vt, c1, a, r2, w, x, g2, seq, final_gain=None):
    t, d = x.shape
    tm, te = PEER_TOKEN_TILE, PEER_EXPERT_TILE
    ne = u.shape[0] // te
    nk = r2.shape[1]
    rows_per = te // nk
    per_b = seq // tm
    final = final_gain is not None
    steps = (t // tm) * ne + 1

    def cur(s):
        s1 = jnp.minimum(s, steps - 2)
        return s1 // ne, s1 % ne

    def prev(s):
        s0 = jnp.maximum(s - 1, 0)
        return s0 // ne, s0 % ne

    in_specs = [pl.BlockSpec((tm, d), lambda s: (cur(s)[0], 0)),
                pl.BlockSpec((te, d), lambda s: (cur(s)[1], 0)),
                pl.BlockSpec((d, te), lambda s: (0, prev(s)[1])),
                pl.BlockSpec((PEER_HEADS, rows_per, tm), lambda s: (0, prev(s)[1], prev(s)[0])),
                pl.BlockSpec((PEER_HEADS, rows_per, tm), lambda s: (0, prev(s)[1], prev(s)[0])),
                pl.BlockSpec((PEER_HEADS, nk, tm), lambda s: (0, 0, prev(s)[0])),
                pl.BlockSpec((PEER_HEADS, nk, tm), lambda s: (0, 0, prev(s)[0])),
                pl.BlockSpec((tm, d), lambda s: (prev(s)[0], 0)),
                pl.BlockSpec((1, 1, d), lambda s: (prev(s)[0] // per_b, 0, 0))]
    args = [h, u, vt, c1, a, r2, w, x, g2]
    if final:
        in_specs.append(pl.BlockSpec((1, d), lambda s: (0, 0)))
        args.append(final_gain.reshape(1, d))
    return pl.pallas_call(
        functools.partial(_peer_kernel, tm=tm, te=te, nk=nk, ne=ne, final=final),
        grid=(steps,),
        in_specs=in_specs,
        out_specs=pl.BlockSpec((tm, d), lambda s: (prev(s)[0], 0)),
        out_shape=jax.ShapeDtypeStruct((t, d), F32),
        scratch_shapes=[pltpu.VMEM((te, tm), BF16), pltpu.VMEM((te, tm), BF16),
                        pltpu.VMEM((te, tm), BF16), pltpu.VMEM((d, tm), F32),
                        pltpu.VMEM((PEER_HEADS, rows_per, BF16_ROWS, tm), BF16),
                        pltpu.VMEM((PEER_HEADS, rows_per, BF16_ROWS, tm), BF16),
                        pltpu.VMEM((PEER_HEADS, nk, tm), BF16), pltpu.VMEM((PEER_HEADS, nk, tm), BF16)],
        compiler_params=_params(("arbitrary",)), name="peer_experts",
    )(*args)


def _pad_cols(w, n):
    return jnp.pad(w, ((0, 0), (0, n - w.shape[1])))


def _even_w_in(w):
    na, ni, nb = A_HEADS * HEAD_DIM, IDX_HEADS * IDX_DIM, B_HEADS * 2 * HEAD_DIM
    o = np.cumsum([0, na, A_LATENT, ni, IDX_DIM, IDX_HEADS, nb, nb, nb])
    kw = _pad_cols(w[:, o[3]:o[5]], LANES)
    return jnp.concatenate([w[:, o[0]:o[3]], kw, w[:, o[5]:o[8]]], axis=1).astype(BF16)


def kernel(x, c, rel_bias, ada_w, ada_b, norm_mix, norm_ffn, norm_final, even_w_in, even_w_out,
           a_kv_norm, a_w_uk, a_w_uv, b_lambda, b_subln, odd_w_in, odd_b_forget, odd_w_out,
           peer_w_q, peer_sub_keys, peer_u, peer_v):
    batch, seq, d = x.shape
    depth = ada_w.shape[0]
    t = batch * seq
    assert seq % ROW_TILE == 0 and seq % ATT_BLOCK == 0 and t % PEER_TOKEN_TILE == 0

    mod = _ada(c, ada_w, ada_b)
    tiles = _bias_tiles(rel_bias)

    na, ni, nb = A_HEADS * HEAD_DIM, IDX_HEADS * IDX_DIM, B_HEADS * 2 * HEAD_DIM
    nc = C_HEADS * HEAD_DIM
    o = np.cumsum([0, na, A_LATENT, ni, LANES, nb, nb, nb])
    even_segs = [(o[0], o[1], None, [BF16]), (o[1], o[2], "kvnorm", [BF16]), (o[2], o[3], None, [BF16]),
                 (o[3], o[4], None, [F32, BF16]), (o[4], o[5], None, [BF16]), (o[5], o[6], None, [BF16]),
                 (o[6], o[7], None, [BF16])]
    odd_segs = [(0, nc, None, [BF16]), (nc, 2 * nc, None, [BF16]), (2 * nc, 3 * nc, None, [BF16]),
                (3 * nc, 3 * nc + LANES, None, [F32])]
    nq_peer = peer_w_q.shape[2]
    peer_segs = [(0, nq_peer, None, [BF16])]

    xt = x.reshape(t, d)
    for layer in range(depth):
        m6 = mod[layer].reshape(batch, 6, 1, d)
        sh1, sc1, g1, sh2, sc2, g2 = [m6[:, i] for i in range(6)]
        if layer % 2 == 0:
            e = layer // 2
            lam_init = 0.8 - 0.6 * math.exp(-0.3 * layer)
            q_a, ckv, q_i, kw32, kw16, q_b, k_b, v_b = _normproj(
                xt, norm_mix[layer], sc1, sh1, _even_w_in(even_w_in[e]), even_segs, seq,
                kvn=a_kv_norm[e], name="even_in_proj")
            wuk = jnp.transpose(a_w_uk[e], (1, 2, 0)).astype(BF16)
            wuv = jnp.transpose(a_w_uv[e], (1, 0, 2)).astype(BF16)
            o_a = _dsa_attention(rel_bias, q_a, q_i, kw32, kw16, ckv, wuk, wuv, tiles, batch, seq)
            o_b = _diff_attention(rel_bias, q_b, k_b, v_b, tiles, b_lambda[e], b_subln[e], batch, seq, lam_init)
            mixes = [o_a, o_b]
            w_out = even_w_out[e].astype(BF16)
        else:
            od = layer // 2
            w_in = _pad_cols(odd_w_in[od], 3 * nc + LANES).astype(BF16)
            q, k, v, f = _normproj(xt, norm_mix[layer], sc1, sh1, w_in, odd_segs, seq, name="odd_in_proj")
            fq, ft = _cumgate(f, odd_b_forget[od], batch, seq)
            mixes = [_fox_attention(q, k, v, fq, ft, batch, seq)]
            w_out = odd_w_out[od].astype(BF16)
        xt, h2, qp = _resproj(xt, mixes, w_out, g1, norm_ffn[layer], sc2, sh2,
                              peer_w_q[layer].astype(BF16), peer_segs, seq)
        c1, a, r2, w = _router(qp, peer_sub_keys[layer].astype(BF16))
        xt = _peer(h2, peer_u[layer].astype(BF16), peer_v[layer].T.astype(BF16), c1, a, r2, w,
                   xt, g2, seq, final_gain=norm_final if layer == depth - 1 else None)
    return xt.reshape(batch, seq, d)
```

```python
import functools
import math

import numpy as np
import jax
import jax.numpy as jnp
from jax import lax
from jax.experimental import pallas as pl
from jax.experimental.pallas import tpu as pltpu

F32 = jnp.float32
BF16 = jnp.bfloat16
I32 = jnp.int32

HEAD_DIM = 64
RMS_EPS = 1e-6
NEG_INF = -1e30
A_HEADS = 8
A_LATENT = 256
IDX_HEADS = 8
IDX_DIM = 64
TOPK_MAX = 256
B_HEADS = 4
C_HEADS = 16
REL_BUCKETS = 32
REL_MAX_DIST = 128
PEER_HEADS = 8
PEER_TOPK = 16

LANES = 128
BF16_ROWS = 16
ATT_BLOCK = 256
ROW_TILE = 512
PEER_TOKEN_TILE = 512
PEER_EXPERT_TILE = 1024
ROUTER_TILE = 512
FOX_GROUP = 4
DIFF_GROUP = 2
F_PIECES = 3
VMEM_LIMIT = 56 * 1024 * 1024
INT_MIN = -2 ** 31

_NT = (((1,), (1,)), ((), ()))


def _params(sem):
    return pltpu.CompilerParams(dimension_semantics=sem, vmem_limit_bytes=VMEM_LIMIT)


def _rms(x, g):
    return x * lax.rsqrt(jnp.mean(x * x, axis=-1, keepdims=True) + RMS_EPS) * g


def _ones_row(rows, cols):
    return jnp.where(lax.broadcasted_iota(I32, (rows, cols), 0) == 0, 1.0, 0.0).astype(BF16)


def _key_le_query(tb):
    return lax.broadcasted_iota(I32, (tb, tb), 0) <= lax.broadcasted_iota(I32, (tb, tb), 1)


def _ada_kernel(c_ref, w_ref, b_ref, o_ref):
    c = c_ref[...]
    ca = c * jax.nn.sigmoid(c)
    o_ref[0] = jnp.dot(ca, w_ref[0], precision=lax.Precision.HIGHEST,
                       preferred_element_type=F32) + b_ref[0]


def _ada(c, ada_w, ada_b):
    depth, d, n = ada_w.shape
    b = c.shape[0]
    bp = 8
    cp = jnp.zeros((bp, d), F32).at[:b].set(c)
    tn = 1536
    out = pl.pallas_call(
        _ada_kernel,
        grid=(depth, n // tn),
        in_specs=[pl.BlockSpec((bp, d), lambda l, j: (0, 0)),
                  pl.BlockSpec((1, d, tn), lambda l, j: (l, 0, j)),
                  pl.BlockSpec((1, 1, tn), lambda l, j: (l, 0, j))],
        out_specs=pl.BlockSpec((1, bp, tn), lambda l, j: (l, 0, j)),
        out_shape=jax.ShapeDtypeStruct((depth, bp, n), F32),
        compiler_params=_params(("arbitrary", "arbitrary")),
        name="ada_mod",
    )(cp, ada_w, ada_b.reshape(depth, 1, n))
    return out[:, :b]


def _store_transposed(out_ref, rt, dt):
    for c in range(rt.shape[1] // ATT_BLOCK):
        out_ref[c] = rt[:, c * ATT_BLOCK:(c + 1) * ATT_BLOCK].astype(dt)


def _emit_segments(hb, w_ref, wt_ref, segs, outs, kvn_ref):
    k = 0
    for (a, b, kind, dtypes) in segs:
        if kind in ("T", "Tq"):
            r = lax.dot_general(wt_ref[a:b, :], hb, _NT, preferred_element_type=F32)
            if kind == "Tq":
                r = r * (HEAD_DIM ** -0.5)
            for dt in dtypes:
                _store_transposed(outs[k], r, dt)
                k += 1
            continue
        r = jnp.dot(hb, w_ref[:, a:b], preferred_element_type=F32)
        if kind == "kvnorm":
            r = _rms(r, kvn_ref[...])
            outs[k][...] = r.astype(dtypes[0])
            _store_transposed(outs[k + 1], r.T, dtypes[0])
            k += 2
            continue
        for dt in dtypes:
            outs[k][...] = r.astype(dt)
            k += 1


def _normproj_kernel(*refs, segs, has_kvn, has_wt):
    x_ref, gain_ref, sc_ref, sh_ref, w_ref = refs[:5]
    pos = 5
    wt_ref = kvn_ref = None
    if has_wt:
        wt_ref = refs[pos]
        pos += 1
    if has_kvn:
        kvn_ref = refs[pos]
        pos += 1
    outs = list(refs[pos:])
    h = _rms(x_ref[...], gain_ref[...]) * (1.0 + sc_ref[0]) + sh_ref[0]
    _emit_segments(h.astype(BF16), w_ref, wt_ref, segs, outs, kvn_ref)


def _resproj_kernel(*refs, n_mix, segs):
    x_ref = refs[0]
    mix_refs = refs[1:1 + n_mix]
    wo_ref, g_ref, gain_ref, sc_ref, sh_ref, w_ref = refs[1 + n_mix:7 + n_mix]
    outs = list(refs[7 + n_mix:])
    y = None
    off = 0
    for m in mix_refs:
        kdim = m.shape[1]
        t = jnp.dot(m[...], wo_ref[off:off + kdim, :], preferred_element_type=F32)
        y = t if y is None else y + t
        off += kdim
    xn = x_ref[...] + g_ref[0] * y
    outs[0][...] = xn
    h = _rms(xn, gain_ref[...]) * (1.0 + sc_ref[0]) + sh_ref[0]
    hb = h.astype(BF16)
    outs[1][...] = hb
    _emit_segments(hb, w_ref, None, segs, outs[2:], None)


def _seg_out_shapes(t, segs):
    shapes, specs = [], []
    per = ROW_TILE // ATT_BLOCK

    def plain(n, dt):
        shapes.append(jax.ShapeDtypeStruct((t, n), dt))
        specs.append(pl.BlockSpec((ROW_TILE, n), lambda i: (i, 0)))

    def transposed(n, dt):
        shapes.append(jax.ShapeDtypeStruct((t // ATT_BLOCK, n, ATT_BLOCK), dt))
        specs.append(pl.BlockSpec((per, n, ATT_BLOCK), lambda i: (i, 0, 0)))

    for (a, b, kind, dtypes) in segs:
        if kind == "kvnorm":
            plain(b - a, dtypes[0])
            transposed(b - a, dtypes[0])
            continue
        for dt in dtypes:
            (transposed if kind in ("T", "Tq") else plain)(b - a, dt)
    return shapes, specs


def _normproj(x, gain, sc, sh, w, segs, seq, kvn=None, wt=None, name="normproj"):
    t, d = x.shape
    tm = ROW_TILE
    per_b = seq // tm
    n = w.shape[1]
    mod_spec = pl.BlockSpec((1, 1, d), lambda i: (i // per_b, 0, 0))
    in_specs = [pl.BlockSpec((tm, d), lambda i: (i, 0)),
                pl.BlockSpec((1, d), lambda i: (0, 0)), mod_spec, mod_spec,
                pl.BlockSpec((d, n), lambda i: (0, 0))]
    args = [x, gain.reshape(1, d), sc, sh, w]
    if wt is not None:
        in_specs.append(pl.BlockSpec((n, d), lambda i: (0, 0)))
        args.append(wt)
    if kvn is not None:
        in_specs.append(pl.BlockSpec((1, kvn.shape[-1]), lambda i: (0, 0)))
        args.append(kvn.reshape(1, -1))
    shapes, specs = _seg_out_shapes(t, segs)
    return pl.pallas_call(
        functools.partial(_normproj_kernel, segs=segs, has_kvn=kvn is not None, has_wt=wt is not None),
        grid=(t // tm,), in_specs=in_specs, out_specs=specs, out_shape=shapes,
        compiler_params=_params(("arbitrary",)), name=name,
    )(*args)


def _resproj(x, mixes, w_out, g, gain, sc, sh, w, segs, seq, name="resproj"):
    t, d = x.shape
    tm = ROW_TILE
    per_b = seq // tm
    n = w.shape[1]
    mod_spec = pl.BlockSpec((1, 1, d), lambda i: (i // per_b, 0, 0))
    in_specs = [pl.BlockSpec((tm, d), lambda i: (i, 0))]
    in_specs += [pl.BlockSpec((tm, m.shape[1]), lambda i: (i, 0)) for m in mixes]
    in_specs += [pl.BlockSpec(w_out.shape, lambda i: (0, 0)), mod_spec,
                 pl.BlockSpec((1, d), lambda i: (0, 0)), mod_spec, mod_spec,
                 pl.BlockSpec((d, n), lambda i: (0, 0))]
    shapes, specs = _seg_out_shapes(t, segs)
    shapes = [jax.ShapeDtypeStruct((t, d), F32), jax.ShapeDtypeStruct((t, d), BF16)] + shapes
    specs = [pl.BlockSpec((tm, d), lambda i: (i, 0)), pl.BlockSpec((tm, d), lambda i: (i, 0))] + specs
    return pl.pallas_call(
        functools.partial(_resproj_kernel, n_mix=len(mixes), segs=segs),
        grid=(t // tm,), in_specs=in_specs, out_specs=specs, out_shape=shapes,
        compiler_params=_params(("arbitrary",)), name=name,
    )(x, *mixes, w_out, g, gain.reshape(1, d), sc, sh, w)


def _bucket_table(n):
    max_exact = REL_BUCKETS // 2
    d = np.arange(n)
    df = np.maximum(d, 1).astype(np.float32)
    large = max_exact + (np.log(df / max_exact) / math.log(REL_MAX_DIST / max_exact)
                         * (REL_BUCKETS - max_exact)).astype(np.int32)
    large = np.minimum(large, REL_BUCKETS - 1)
    return np.where(d < max_exact, d, large).astype(np.int32)


def _bias_tiles_kernel(rb_ref, bk_ref, o_ref):
    h = pl.program_id(0)
    for t in range(2):
        bt = bk_ref[t]
        acc = jnp.zeros(bt.shape, F32)
        for b in range(REL_BUCKETS):
            acc = jnp.where(bt == b, rb_ref[b, h], acc)
        o_ref[0, t] = acc


def _bias_tiles(rel_bias):
    tb = ATT_BLOCK
    nh = rel_bias.shape[1]
    table = _bucket_table(2 * tb)
    s = np.arange(tb)[:, None]
    t = np.arange(tb)[None, :]
    bk = np.stack([table[np.maximum(t - s, 0)], table[tb + t - s]]).astype(np.int32)
    return pl.pallas_call(
        _bias_tiles_kernel,
        grid=(nh,),
        in_specs=[pl.BlockSpec(memory_space=pltpu.SMEM),
                  pl.BlockSpec((2, tb, tb), lambda h: (0, 0, 0))],
        out_specs=pl.BlockSpec((1, 2, tb, tb), lambda h: (h, 0, 0, 0)),
        out_shape=jax.ShapeDtypeStruct((nh, 2, tb, tb), F32),
        compiler_params=_params(("arbitrary",)), name="bias_tiles",
    )(rel_bias, jnp.asarray(bk))


def _flash_update(s, m, shift_const, acc_old, vaug):
    m_new = jnp.maximum(m, jnp.max(s, axis=0, keepdims=True) + shift_const)
    p = jnp.exp(s - (m_new - shift_const)).astype(BF16)
    acc = jnp.exp(m - m_new) * acc_old + jnp.dot(vaug, p, preferred_element_type=F32)
    return m_new, acc


def _split_bf16(x):
    pieces = []
    for _ in range(F_PIECES):
        p = x.astype(BF16)
        pieces.append(p)
        x = x - p.astype(F32)
    return pieces


def _fox_sel():
    selk = np.zeros((F_PIECES, LANES, C_HEADS * HEAD_DIM), np.float32)
    selq = np.zeros((F_PIECES, C_HEADS * HEAD_DIM, LANES), np.float32)
    onesk = np.zeros((1, C_HEADS * HEAD_DIM), np.float32)
    onesq = np.zeros((C_HEADS * HEAD_DIM, 1), np.float32)
    for h in range(C_HEADS):
        for p in range(F_PIECES):
            selk[p, h, h * HEAD_DIM + p] = -1.0
            selq[p, h * HEAD_DIM + F_PIECES + p, h] = 1.0
            onesk[0, h * HEAD_DIM + F_PIECES + p] = 1.0
            onesq[h * HEAD_DIM + p, 0] = 1.0
    return selk, selq, onesk, onesq


def _foxgate_kernel(f_ref, b_ref, selk_ref, selq_ref, onesk_ref, onesq_ref, kf_ref, qft_ref, *, tb):
    nblk = f_ref.shape[0] // tb
    row = lax.broadcasted_iota(I32, (tb, tb), 0)
    col = lax.broadcasted_iota(I32, (tb, tb), 1)
    tri = jnp.where(col <= row, 1.0, 0.0).astype(F32)

    def body(j, carry):
        z = f_ref[pl.ds(j * tb, tb), :] + b_ref[...]
        ls = -(jnp.maximum(-z, 0.0) + jnp.log(1.0 + jnp.exp(-jnp.abs(z))))
        cs = jnp.dot(tri, ls, precision=lax.Precision.HIGHEST, preferred_element_type=F32) + carry
        kf = onesk_ref[...]
        for p, piece in enumerate(_split_bf16(cs)):
            kf = kf + jnp.dot(piece, selk_ref[p], preferred_element_type=F32)
        kf_ref[pl.ds(j * tb, tb), :] = kf.astype(BF16)
        qf = onesq_ref[...]
        for p, piece in enumerate(_split_bf16(cs.T)):
            qf = qf + jnp.dot(selq_ref[p], piece, preferred_element_type=F32)
        qft_ref[j] = qf.astype(BF16)
        return cs[tb - 1:tb, :]

    lax.fori_loop(0, nblk, body, jnp.zeros((1, LANES), F32))


def _foxgate(f, b_forget, batch, seq):
    tb = ATT_BLOCK
    nblk = seq // tb
    n = C_HEADS * HEAD_DIM
    bpad = jnp.zeros((1, LANES), F32).at[0, :C_HEADS].set(b_forget)
    selk, selq, onesk, onesq = _fox_sel()
    return pl.pallas_call(
        functools.partial(_foxgate_kernel, tb=tb),
        grid=(batch,),
        in_specs=[pl.BlockSpec((seq, LANES), lambda b: (b, 0)),
                  pl.BlockSpec((1, LANES), lambda b: (0, 0)),
                  pl.BlockSpec(selk.shape, lambda b: (0, 0, 0)),
                  pl.BlockSpec(selq.shape, lambda b: (0, 0, 0)),
                  pl.BlockSpec(onesk.shape, lambda b: (0, 0)),
                  pl.BlockSpec(onesq.shape, lambda b: (0, 0))],
        out_specs=[pl.BlockSpec((seq, n), lambda b: (b, 0)),
                   pl.BlockSpec((nblk, n, tb), lambda b: (b, 0, 0))],
        out_shape=[jax.ShapeDtypeStruct((batch * seq, n), BF16),
                   jax.ShapeDtypeStruct((batch * nblk, n, tb), BF16)],
        compiler_params=_params(("arbitrary",)), name="fox_gates",
    )(f, bpad, jnp.asarray(selk, BF16), jnp.asarray(selq, BF16), jnp.asarray(onesk), jnp.asarray(onesq))


def _fox_kernel(qt_ref, qft_ref, k_ref, kf_ref, vt_ref, o_ref, kaug_ref, vaug_ref, *, tb, nq):
    qi = pl.program_id(2)
    hd = HEAD_DIM
    heads = range(FOX_GROUP)

    @pl.when(qi == 0)
    def _():
        ones_rows = _ones_row(hd, tb)
        for h in heads:
            kaug_ref[h, :, 0:hd] = k_ref[:, h * hd:(h + 1) * hd]
            kaug_ref[h, :, hd:2 * hd] = kf_ref[:, h * hd:(h + 1) * hd]

            def fill(j, c, h=h):
                vaug_ref[h, j, 0:hd, :] = vt_ref[j, h * hd:(h + 1) * hd, :]
                vaug_ref[h, j, hd:2 * hd, :] = ones_rows
                return c

            lax.fori_loop(0, nq, fill, 0)

    qa = [jnp.concatenate([qt_ref[0, h * hd:(h + 1) * hd, :], qft_ref[0, h * hd:(h + 1) * hd, :]], axis=0)
          for h in heads]
    keep = _key_le_query(tb)

    def step(j, carry, masked):
        scores = [jnp.dot(kaug_ref[h, pl.ds(j * tb, tb), :], qa[h], preferred_element_type=F32)
                  for h in heads]
        out = []
        for h in heads:
            m, acc = carry[h]
            s = jnp.where(keep, scores[h], NEG_INF) if masked else scores[h]
            out.append(_flash_update(s, m, 0.0, acc, vaug_ref[h, j]))
        return tuple(out)

    init = tuple((jnp.full((1, tb), NEG_INF, F32), jnp.zeros((2 * hd, tb), F32)) for _ in heads)
    carry = lax.fori_loop(0, qi, lambda j, c: step(j, c, False), init)
    carry = step(qi, carry, True)
    for pair in range(FOX_GROUP // 2):
        ot = jnp.concatenate([carry[h][1][0:hd] / carry[h][1][hd:hd + 1] for h in (2 * pair, 2 * pair + 1)], axis=0)
        o_ref[:, pair * 2 * hd:(pair + 1) * 2 * hd] = ot.T.astype(o_ref.dtype)


def _fox_attention(qt, qft, k, kf, vt, batch, seq):
    tb = ATT_BLOCK
    nq = seq // tb
    groups = C_HEADS // FOX_GROUP
    gw = FOX_GROUP * HEAD_DIM
    return pl.pallas_call(
        functools.partial(_fox_kernel, tb=tb, nq=nq),
        grid=(batch, groups, nq),
        in_specs=[pl.BlockSpec((1, gw, tb), lambda b, g, i: (b * nq + i, g, 0)),
                  pl.BlockSpec((1, gw, tb), lambda b, g, i: (b * nq + i, g, 0)),
                  pl.BlockSpec((seq, gw), lambda b, g, i: (b, g)),
                  pl.BlockSpec((seq, gw), lambda b, g, i: (b, g)),
                  pl.BlockSpec((nq, gw, tb), lambda b, g, i: (b, g, 0))],
        out_specs=pl.BlockSpec((tb, gw), lambda b, g, i: (b * nq + i, g)),
        out_shape=jax.ShapeDtypeStruct((batch * seq, C_HEADS * HEAD_DIM), BF16),
        scratch_shapes=[pltpu.VMEM((FOX_GROUP, seq, 2 * HEAD_DIM), BF16),
                        pltpu.VMEM((FOX_GROUP, nq, 2 * HEAD_DIM, tb), BF16)],
        compiler_params=_params(("arbitrary", "arbitrary", "arbitrary")), name="fox_attention",
    )(qt, qft, k, kf, vt)


def _diff_kernel(rb_ref, qt_ref, k_ref, vt_ref, tiles_ref, lam_ref, g_ref, o_ref,
                 ks_ref, vaug_ref, acc_ref, *, tb, nq, lam_init):
    g = pl.program_id(1)
    qi = pl.program_id(2)
    hd = HEAD_DIM
    dv = 2 * hd
    chains = [(hh, s_) for hh in range(DIFF_GROUP) for s_ in range(2)]

    @pl.when(qi == 0)
    def _():
        ones_rows = _ones_row(BF16_ROWS, tb)
        for c, (hh, s_) in enumerate(chains):
            ks_ref[c] = k_ref[:, hh * dv + s_ * hd:hh * dv + (s_ + 1) * hd]
        for hh in range(DIFF_GROUP):
            def fill(j, c, hh=hh):
                vaug_ref[hh, j, 0:dv, :] = vt_ref[j, hh * dv:(hh + 1) * dv, :]
                vaug_ref[hh, j, dv:dv + BF16_ROWS, :] = ones_rows
                return c

            lax.fori_loop(0, nq, fill, 0)

    qs = [qt_ref[0, hh * dv + s_ * hd:hh * dv + (s_ + 1) * hd, :] for (hh, s_) in chains]
    cfar = [rb_ref[REL_BUCKETS - 1, A_HEADS + g * DIFF_GROUP + hh] for hh in range(DIFF_GROUP)]
    keep = _key_le_query(tb)
    acc_ref[...] = jnp.zeros_like(acc_ref)

    def step(j, ms, kind):
        scores = [jnp.dot(ks_ref[c, pl.ds(j * tb, tb), :], qs[c], preferred_element_type=F32)
                  for c in range(len(chains))]
        out = []
        for c, (hh, s_) in enumerate(chains):
            if kind == "far":
                s, shift = scores[c], cfar[hh]
            else:
                s, shift = scores[c] + tiles_ref[hh, 0 if kind == "diag" else 1], 0.0
                if kind == "diag":
                    s = jnp.where(keep, s, NEG_INF)
            m_new, acc = _flash_update(s, ms[c], shift, acc_ref[c], vaug_ref[hh, j])
            acc_ref[c] = acc
            out.append(m_new)
        return tuple(out)

    ms = tuple(jnp.full((1, tb), NEG_INF, F32) for _ in chains)
    ms = lax.fori_loop(0, jnp.maximum(qi - 1, 0), lambda j, c: step(j, c, "far"), ms)
    ms = lax.cond(qi >= 1, lambda c: step(qi - 1, c, "near"), lambda c: c, ms)
    step(qi, ms, "diag")

    lv = lam_ref[...]
    lam = (jnp.exp(jnp.sum(lv[0:1] * lv[1:2], axis=-1, keepdims=True))
           - jnp.exp(jnp.sum(lv[2:3] * lv[3:4], axis=-1, keepdims=True)) + lam_init)
    for hh in range(DIFF_GROUP):
        a1 = acc_ref[2 * hh]
        a2 = acc_ref[2 * hh + 1]
        ot = a1[0:dv] / a1[dv:dv + 1] - lam * (a2[0:dv] / a2[dv:dv + 1])
        o = _rms(ot.T, g_ref[...]) * (1.0 - lam_init)
        o_ref[:, hh * dv:(hh + 1) * dv] = o.astype(o_ref.dtype)


def _diff_attention(rel_bias, qt, k, vt, tiles, lam_vec, subln, batch, seq, lam_init):
    tb = ATT_BLOCK
    nq = seq // tb
    groups = B_HEADS // DIFF_GROUP
    gw = DIFF_GROUP * 2 * HEAD_DIM
    dv = 2 * HEAD_DIM
    return pl.pallas_call(
        functools.partial(_diff_kernel, tb=tb, nq=nq, lam_init=lam_init),
        grid=(batch, groups, nq),
        in_specs=[pl.BlockSpec(memory_space=pltpu.SMEM),
                  pl.BlockSpec((1, gw, tb), lambda b, g, i: (b * nq + i, g, 0)),
                  pl.BlockSpec((seq, gw), lambda b, g, i: (b, g)),
                  pl.BlockSpec((nq, gw, tb), lambda b, g, i: (b, g, 0)),
                  pl.BlockSpec((DIFF_GROUP, 2, tb, tb), lambda b, g, i: (A_HEADS // DIFF_GROUP + g, 0, 0, 0)),
                  pl.BlockSpec((4, HEAD_DIM), lambda b, g, i: (0, 0)),
                  pl.BlockSpec((1, dv), lambda b, g, i: (0, 0))],
        out_specs=pl.BlockSpec((tb, gw), lambda b, g, i: (b * nq + i, g)),
        out_shape=jax.ShapeDtypeStruct((batch * seq, B_HEADS * dv), BF16),
        scratch_shapes=[pltpu.VMEM((2 * DIFF_GROUP, seq, HEAD_DIM), BF16),
                        pltpu.VMEM((DIFF_GROUP, nq, dv + BF16_ROWS, tb), BF16),
                        pltpu.VMEM((2 * DIFF_GROUP, dv + BF16_ROWS, tb), F32)],
        compiler_params=_params(("arbitrary", "arbitrary", "arbitrary")), name="diff_attention",
    )(rel_bias, qt, k, vt, tiles, lam_vec, subln.reshape(1, -1))


def _dsa_kernel(rb_ref, qat_ref, qit_ref, wit_ref, kw_ref, ckv_ref, ckvt_ref, wuk_ref, wuvt_ref, tiles_ref,
                o_ref, keys_ref, selb_ref, qlat_ref, vaug_ref, acc_ref, *, tb, topk, nq):
    qb = pl.program_id(1)
    nblk = qb + 1
    hd = HEAD_DIM
    keep = _key_le_query(tb)

    @pl.when(qb == 0)
    def _():
        ones_rows = _ones_row(BF16_ROWS, tb)

        def fill(j, c):
            vaug_ref[j, 0:A_LATENT, :] = ckvt_ref[j]
            vaug_ref[j, A_LATENT:A_LATENT + BF16_ROWS, :] = ones_rows
            return c

        lax.fori_loop(0, nq, fill, 0)

    for h in range(A_HEADS):
        ql = jnp.dot(wuk_ref[h], qat_ref[0, h * hd:(h + 1) * hd, :], preferred_element_type=F32)
        qlat_ref[h] = ql.astype(BF16)

    wrows = [wit_ref[0, IDX_DIM + h:IDX_DIM + h + 1, :] * (IDX_HEADS ** -0.5) for h in range(IDX_HEADS)]
    zpad = jnp.zeros((LANES - IDX_DIM, tb), BF16)
    qi_pad = [jnp.concatenate([qit_ref[0, h * IDX_DIM:(h + 1) * IDX_DIM, :], zpad], axis=0)
              for h in range(IDX_HEADS)]

    def index_keys(j, masked):
        kb = kw_ref[pl.ds(j * tb, tb), :]
        isc = jnp.zeros((tb, tb), F32)
        for h in range(IDX_HEADS):
            li = jnp.dot(kb, qi_pad[h], preferred_element_type=F32)
            isc = isc + jnp.maximum(li, 0.0) * wrows[h]
        isc = jnp.where(isc == 0.0, 0.0, isc)
        bits = pltpu.bitcast(isc, I32)
        key = bits ^ ((bits >> 31) & 0x7FFFFFFF)
        if masked:
            key = jnp.where(keep, key, INT_MIN)
        keys_ref[j] = key

    def p1(j, c):
        index_keys(j, False)
        return c

    lax.fori_loop(0, qb, p1, 0)
    index_keys(qb, True)

    def count(pred):
        def body(j, acc):
            ind = jnp.where(pred(keys_ref[j]), 1.0, 0.0)
            return acc + jnp.sum(ind.reshape(tb // 8, 8, tb), axis=0)
        acc = lax.fori_loop(0, nblk, body, jnp.zeros((8, tb), F32))
        return jnp.sum(acc, axis=0, keepdims=True)

    kth = jnp.where(count(lambda k: k >= 0) >= topk, 0, INT_MIN).astype(I32)

    def bs(i, kth):
        cand = kth | lax.shift_left(jnp.int32(1), 30 - i)
        return jnp.where(count(lambda k: k >= cand) >= topk, cand, kth)

    kth = lax.fori_loop(0, 31, bs, kth)
    need = topk - count(lambda k: k > kth)

    lower = jnp.where(lax.broadcasted_iota(I32, (tb, tb), 1) <= lax.broadcasted_iota(I32, (tb, tb), 0),
                      1.0, 0.0).astype(BF16)

    def mask_block(j, seen, masked):
        key = keys_ref[j]
        eq = key == kth
        pre = jnp.dot(lower, jnp.where(eq, 1.0, 0.0).astype(BF16), preferred_element_type=F32)
        sel = (key > kth) | (eq & (pre + seen <= need))
        if masked:
            sel = sel & keep
        selb_ref[j] = jnp.where(sel, 0.0, NEG_INF)
        return seen + pre[tb - 1:tb, :]

    seen = lax.fori_loop(0, qb, lambda j, s: mask_block(j, s, False), jnp.zeros((1, tb), F32))
    mask_block(qb, seen, True)

    cfar = [rb_ref[REL_BUCKETS - 1, h] for h in range(A_HEADS)]
    acc_ref[...] = jnp.zeros_like(acc_ref)

    def step(j, ms, kind):
        kvb = ckv_ref[pl.ds(j * tb, tb), :]
        scores = [jnp.dot(kvb, qlat_ref[h], preferred_element_type=F32) for h in range(A_HEADS)]
        sb = selb_ref[j]
        out = []
        for h in range(A_HEADS):
            if kind == "far":
                s, shift = scores[h] + sb, cfar[h]
            else:
                s, shift = scores[h] + (tiles_ref[h, 0 if kind == "diag" else 1] + sb), 0.0
            m_new, acc = _flash_update(s, ms[h], shift, acc_ref[h], vaug_ref[j])
            acc_ref[h] = acc
            out.append(m_new)
        return tuple(out)

    ms = tuple(jnp.full((1, tb), NEG_INF, F32) for _ in range(A_HEADS))
    ms = lax.fori_loop(0, jnp.maximum(qb - 1, 0), lambda j, c: step(j, c, "far"), ms)
    ms = lax.cond(qb >= 1, lambda c: step(qb - 1, c, "near"), lambda c: c, ms)
    step(qb, ms, "diag")

    outs = []
    for h in range(A_HEADS):
        a = acc_ref[h]
        o_lat = (a[0:A_LATENT] / a[A_LATENT:A_LATENT + 1]).astype(BF16)
        outs.append(jnp.dot(wuvt_ref[h], o_lat, preferred_element_type=F32))
    o_ref[...] = jnp.concatenate(outs, axis=0).T.astype(o_ref.dtype)


def _dsa_attention(rel_bias, qat, qit, wit, kw, ckv, ckvt, wuk, wuvt, tiles, batch, seq):
    tb = ATT_BLOCK
    nq = seq // tb
    topk = min(TOPK_MAX, seq // 4)
    nqa = A_HEADS * HEAD_DIM
    nqi = IDX_HEADS * IDX_DIM
    aug = A_LATENT + BF16_ROWS
    return pl.pallas_call(
        functools.partial(_dsa_kernel, tb=tb, topk=topk, nq=nq),
        grid=(batch, nq),
        in_specs=[pl.BlockSpec(memory_space=pltpu.SMEM),
                  pl.BlockSpec((1, nqa, tb), lambda b, i: (b * nq + i, 0, 0)),
                  pl.BlockSpec((1, nqi, tb), lambda b, i: (b * nq + i, 0, 0)),
                  pl.BlockSpec((1, LANES, tb), lambda b, i: (b * nq + i, 0, 0)),
                  pl.BlockSpec((seq, LANES), lambda b, i: (b, 0)),
                  pl.BlockSpec((seq, A_LATENT), lambda b, i: (b, 0)),
                  pl.BlockSpec((nq, A_LATENT, tb), lambda b, i: (b, 0, 0)),
                  pl.BlockSpec(wuk.shape, lambda b, i: (0, 0, 0)),
                  pl.BlockSpec(wuvt.shape, lambda b, i: (0, 0, 0)),
                  pl.BlockSpec((A_HEADS, 2, tb, tb), lambda b, i: (0, 0, 0, 0))],
        out_specs=pl.BlockSpec((tb, nqa), lambda b, i: (b * nq + i, 0)),
        out_shape=jax.ShapeDtypeStruct((batch * seq, nqa), BF16),
        scratch_shapes=[pltpu.VMEM((nq, tb, tb), I32), pltpu.VMEM((nq, tb, tb), F32),
                        pltpu.VMEM((A_HEADS, A_LATENT, tb), BF16),
                        pltpu.VMEM((nq, aug, tb), BF16),
                        pltpu.VMEM((A_HEADS, aug, tb), F32)],
        compiler_params=_params(("arbitrary", "arbitrary")), name="dsa_attention",
    )(rel_bias, qat, qit, wit, kw, ckv, ckvt, wuk, wuvt, tiles)


def _top_sorted(x, k):
    rows = lax.broadcasted_iota(I32, (k, x.shape[1]), 0)
    out = jnp.zeros((k, x.shape[1]), F32)
    rank = jnp.full(x.shape, 127.0, F32)
    for r in range(k):
        m = jnp.max(x, axis=0, keepdims=True)
        hit = x == m
        out = jnp.where(rows == r, m, out)
        rank = jnp.where(hit, float(r), rank)
        x = jnp.where(hit, -jnp.inf, x)
    return out, rank


def _router_kernel(q_ref, keys_ref, c1_ref, a_ref, r2_ref, w_ref, *, tr):
    k = PEER_TOPK
    nk = keys_ref.shape[1]
    row8 = lax.broadcasted_iota(I32, (8, LANES), 0)
    for h in range(PEER_HEADS):
        for tc in range(tr // LANES):
            tok = slice(tc * LANES, (tc + 1) * LANES)
            qh = q_ref[tok, :]
            s1 = lax.dot_general(keys_ref[0], qh[:, (2 * h) * nk:(2 * h + 1) * nk], _NT,
                                 preferred_element_type=F32)
            s2 = lax.dot_general(keys_ref[1], qh[:, (2 * h + 1) * nk:(2 * h + 2) * nk], _NT,
                                 preferred_element_type=F32)
            a, rank1 = _top_sorted(s1, k)
            b, rank2 = _top_sorted(s2, k)
            b8 = b[0:8]
            parts = [a[0:1] + b, a[1:2] + b8]
            for i, lim in ((2, 5), (3, 4), (4, 3), (5, 2), (6, 2), (7, 2)):
                parts.append(jnp.where(row8 < lim, a[i:i + 1] + b8, -jnp.inf))
            parts.append(a[8:16] + b[0:1])
            cand = jnp.concatenate(parts, axis=0)
            x = cand
            thr = None
            for _ in range(k):
                thr = jnp.max(x, axis=0, keepdims=True)
                x = jnp.where(x == thr, -jnp.inf, x)
            mx = a[0:1] + b[0:1]
            z = jnp.sum(jnp.where(cand >= thr, jnp.exp(cand - mx), 0.0), axis=0, keepdims=True)
            c1 = jnp.zeros((nk, LANES), F32)
            for r in range(k):
                cnt = jnp.sum(jnp.where(a[r:r + 1] + b >= thr, 1.0, 0.0), axis=0, keepdims=True)
                c1 = jnp.where(rank1 == float(r), cnt, c1)
            c1_ref[h, :, tok] = c1
            a_ref[h, :, tok] = jnp.exp(s1 - a[0:1]) / z
            r2_ref[h, :, tok] = rank2.astype(BF16)
            w_ref[h, :, tok] = jnp.exp(s2 - b[0:1]).astype(BF16)


def _router(q, sub_keys):
    t = q.shape[0]
    tr = ROUTER_TILE
    nk = sub_keys.shape[1]
    shp32 = jax.ShapeDtypeStruct((PEER_HEADS, nk, t), F32)
    shp16 = jax.ShapeDtypeStruct((PEER_HEADS, nk, t), BF16)
    spec = pl.BlockSpec((PEER_HEADS, nk, tr), lambda i: (0, 0, i))
    return pl.pallas_call(
        functools.partial(_router_kernel, tr=tr),
        grid=(t // tr,),
        in_specs=[pl.BlockSpec((tr, q.shape[1]), lambda i: (i, 0)),
                  pl.BlockSpec(sub_keys.shape, lambda i: (0, 0, 0))],
        out_specs=[spec] * 4, out_shape=[shp32, shp32, shp16, shp16],
        compiler_params=_params(("arbitrary",)), name="peer_router",
    )(q, sub_keys)


def _peer_kernel(h_ref, u_ref, vt_ref, c1_ref, a_ref, r2_ref, w_ref, x_ref, g_ref, *rest,
                 tm, te, nk, ne, final):
    if final:
        gf_ref, o_ref, act0_ref, act1_ref, p_ref, acc_ref, r2s_ref, ws_ref = rest
    else:
        o_ref, act0_ref, act1_ref, p_ref, acc_ref, r2s_ref, ws_ref = rest
    s = pl.program_id(0)
    e0 = jnp.maximum(s - 1, 0) % ne
    rows_per = te // nk
    half = rows_per // 2
    nsub = nk // BF16_ROWS

    @pl.when(s == 0)
    def _():
        act1_ref[...] = jnp.zeros_like(act1_ref)

    @pl.when(e0 == 0)
    def _():
        acc_ref[...] = jnp.zeros_like(acc_ref)
        r2s_ref[...] = r2_ref[...]
        ws_ref[...] = w_ref[...]

    def main(cur_ref, prev_ref):
        def gate_rows(ii_list):
            for tc in range(tm // LANES):
                tok = slice(tc * LANES, (tc + 1) * LANES)
                gates = [[None] * nsub for _ in ii_list]
                for h in range(PEER_HEADS):
                    r2 = [r2s_ref[h, k * BF16_ROWS:(k + 1) * BF16_ROWS, tok] for k in range(nsub)]
                    w2 = [ws_ref[h, k * BF16_ROWS:(k + 1) * BF16_ROWS, tok] for k in range(nsub)]
                    for n, ii in enumerate(ii_list):
                        c = jnp.broadcast_to(c1_ref[h, ii:ii + 1, tok], (BF16_ROWS, LANES)).astype(BF16)
                        a = jnp.broadcast_to(a_ref[h, ii:ii + 1, tok], (BF16_ROWS, LANES)).astype(BF16)
                        for k in range(nsub):
                            t = jnp.where(r2[k] < c, w2[k] * a, 0)
                            gates[n][k] = t if gates[n][k] is None else gates[n][k] + t
                for n, ii in enumerate(ii_list):
                    for k in range(nsub):
                        rows = slice(ii * nk + k * BF16_ROWS, ii * nk + (k + 1) * BF16_ROWS)
                        p_ref[rows, tok] = gates[n][k] * jax.nn.gelu(prev_ref[rows, tok])

        def second(lo, hi):
            acc_ref[...] += jnp.dot(vt_ref[:, lo * nk:hi * nk], p_ref[lo * nk:hi * nk, :],
                                    preferred_element_type=F32)

        def first(lo, hi):
            cur_ref[lo * nk:hi * nk, :] = lax.dot_general(u_ref[lo * nk:hi * nk, :], h_ref[...], _NT,
                                                          preferred_element_type=F32).astype(BF16)

        gate_rows(list(range(0, half)))
        first(0, half)
        second(0, half)
        gate_rows(list(range(half, rows_per)))
        first(half, rows_per)
        second(half, rows_per)

    @pl.when(s % 2 == 0)
    def _():
        main(act0_ref, act1_ref)

    @pl.when(s % 2 == 1)
    def _():
        main(act1_ref, act0_ref)

    @pl.when((e0 == ne - 1) & (s > 0))
    def _():
        xn = x_ref[...] + g_ref[0] * acc_ref[...].T
        if final:
            xn = _rms(xn, gf_ref[...])
        o_ref[...] = xn


def _peer(h, u, vt, c1, a, r2, w, x, g2, seq, final_gain=None):
    t, d = x.shape
    tm, te = PEER_TOKEN_TILE, PEER_EXPERT_TILE
    ne = u.shape[0] // te
    nk = r2.shape[1]
    rows_per = te // nk
    per_b = seq // tm
    final = final_gain is not None
    steps = (t // tm) * ne + 1

    def cur(s):
        s1 = jnp.minimum(s, steps - 2)
        return s1 // ne, s1 % ne

    def prev(s):
        s0 = jnp.maximum(s - 1, 0)
        return s0 // ne, s0 % ne

    in_specs = [pl.BlockSpec((tm, d), lambda s: (cur(s)[0], 0)),
                pl.BlockSpec((te, d), lambda s: (cur(s)[1], 0)),
                pl.BlockSpec((d, te), lambda s: (0, prev(s)[1])),
                pl.BlockSpec((PEER_HEADS, rows_per, tm), lambda s: (0, prev(s)[1], prev(s)[0])),
                pl.BlockSpec((PEER_HEADS, rows_per, tm), lambda s: (0, prev(s)[1], prev(s)[0])),
                pl.BlockSpec((PEER_HEADS, nk, tm), lambda s: (0, 0, prev(s)[0])),
                pl.BlockSpec((PEER_HEADS, nk, tm), lambda s: (0, 0, prev(s)[0])),
                pl.BlockSpec((tm, d), lambda s: (prev(s)[0], 0)),
                pl.BlockSpec((1, 1, d), lambda s: (prev(s)[0] // per_b, 0, 0))]
    args = [h, u, vt, c1, a, r2, w, x, g2]
    if final:
        in_specs.append(pl.BlockSpec((1, d), lambda s: (0, 0)))
        args.append(final_gain.reshape(1, d))
    return pl.pallas_call(
        functools.partial(_peer_kernel, tm=tm, te=te, nk=nk, ne=ne, final=final),
        grid=(steps,),
        in_specs=in_specs,
        out_specs=pl.BlockSpec((tm, d), lambda s: (prev(s)[0], 0)),
        out_shape=jax.ShapeDtypeStruct((t, d), F32),
        scratch_shapes=[pltpu.VMEM((te, tm), BF16), pltpu.VMEM((te, tm), BF16),
                        pltpu.VMEM((te, tm), BF16), pltpu.VMEM((d, tm), F32),
                        pltpu.VMEM((PEER_HEADS, nk, tm), BF16), pltpu.VMEM((PEER_HEADS, nk, tm), BF16)],
        compiler_params=_params(("arbitrary",)), name="peer_experts",
    )(*args)


def _pad_cols(w, n):
    return jnp.pad(w, ((0, 0), (0, n - w.shape[1])))


def _even_w_in(w):
    na, ni, nb = A_HEADS * HEAD_DIM, IDX_HEADS * IDX_DIM, B_HEADS * 2 * HEAD_DIM
    o = np.cumsum([0, na, A_LATENT, ni, IDX_DIM, IDX_HEADS, nb, nb, nb])
    kw = _pad_cols(w[:, o[3]:o[5]], LANES)
    return jnp.concatenate([w[:, o[0]:o[3]], kw, w[:, o[5]:o[8]]], axis=1).astype(BF16)


def kernel(x, c, rel_bias, ada_w, ada_b, norm_mix, norm_ffn, norm_final, even_w_in, even_w_out,
           a_kv_norm, a_w_uk, a_w_uv, b_lambda, b_subln, odd_w_in, odd_b_forget, odd_w_out,
           peer_w_q, peer_sub_keys, peer_u, peer_v):
    batch, seq, d = x.shape
    depth = ada_w.shape[0]
    t = batch * seq
    assert seq % ROW_TILE == 0 and seq % ATT_BLOCK == 0 and t % PEER_TOKEN_TILE == 0

    mod = _ada(c, ada_w, ada_b)
    tiles = _bias_tiles(rel_bias)

    na, ni, nb = A_HEADS * HEAD_DIM, IDX_HEADS * IDX_DIM, B_HEADS * 2 * HEAD_DIM
    nc = C_HEADS * HEAD_DIM
    o = np.cumsum([0, na, A_LATENT, ni, LANES, nb, nb, nb])
    even_segs = [(o[0], o[1], "Tq", [BF16]), (o[1], o[2], "kvnorm", [BF16]), (o[2], o[3], "Tq", [BF16]),
                 (o[3], o[4], None, [BF16]), (o[3], o[4], "T", [F32]), (o[4], o[5], "Tq", [BF16]),
                 (o[5], o[6], None, [BF16]), (o[6], o[7], "T", [BF16])]
    odd_segs = [(0, nc, "Tq", [BF16]), (nc, 2 * nc, None, [BF16]), (2 * nc, 3 * nc, "T", [BF16]),
                (3 * nc, 3 * nc + LANES, None, [F32])]
    nq_peer = peer_w_q.shape[2]
    peer_segs = [(0, nq_peer, None, [BF16])]

    xt = x.reshape(t, d)
    for layer in range(depth):
        m6 = mod[layer].reshape(batch, 6, 1, d)
        sh1, sc1, g1, sh2, sc2, g2 = [m6[:, i] for i in range(6)]
        if layer % 2 == 0:
            e = layer // 2
            lam_init = 0.8 - 0.6 * math.exp(-0.3 * layer)
            w_in = _even_w_in(even_w_in[e])
            qat, ckv, ckvt, qit, kw, wit, qbt, k_b, vbt = _normproj(
                xt, norm_mix[layer], sc1, sh1, w_in, even_segs, seq, kvn=a_kv_norm[e], wt=w_in.T,
                name="even_in_proj")
            wuk = jnp.transpose(a_w_uk[e], (1, 0, 2)).astype(BF16)
            wuvt = jnp.transpose(a_w_uv[e], (1, 2, 0)).astype(BF16)
            o_a = _dsa_attention(rel_bias, qat, qit, wit, kw, ckv, ckvt, wuk, wuvt, tiles, batch, seq)
            o_b = _diff_attention(rel_bias, qbt, k_b, vbt, tiles, b_lambda[e], b_subln[e], batch, seq, lam_init)
            mixes = [o_a, o_b]
            w_out = even_w_out[e].astype(BF16)
        else:
            od = layer // 2
            w_in = _pad_cols(odd_w_in[od], 3 * nc + LANES).astype(BF16)
            qt, k, vt, f = _normproj(xt, norm_mix[layer], sc1, sh1, w_in, odd_segs, seq, wt=w_in.T,
                                     name="odd_in_proj")
            kf, qft = _foxgate(f, odd_b_forget[od], batch, seq)
            mixes = [_fox_attention(qt, qft, k, kf, vt, batch, seq)]
            w_out = odd_w_out[od].astype(BF16)
        xt, h2, qp = _resproj(xt, mixes, w_out, g1, norm_ffn[layer], sc2, sh2,
                              peer_w_q[layer].astype(BF16), peer_segs, seq)
        c1, a, r2, w = _router(qp, peer_sub_keys[layer].astype(BF16))
        xt = _peer(h2, peer_u[layer].astype(BF16), peer_v[layer].T.astype(BF16), c1, a, r2, w,
                   xt, g2, seq, final_gain=norm_final if layer == depth - 1 else None)
    return xt.reshape(batch, seq, d)
```

```python
import functools
import math

import numpy as np
import jax
import jax.numpy as jnp
from jax import lax
from jax.experimental import pallas as pl
from jax.experimental.pallas import tpu as pltpu

F32 = jnp.float32
BF16 = jnp.bfloat16
I32 = jnp.int32

HEAD_DIM = 64
RMS_EPS = 1e-6
NEG_INF = -1e30
A_HEADS = 8
A_LATENT = 256
IDX_HEADS = 8
IDX_DIM = 64
TOPK_MAX = 256
B_HEADS = 4
C_HEADS = 16
REL_BUCKETS = 32
REL_MAX_DIST = 128
PEER_HEADS = 8
PEER_TOPK = 16

LANES = 128
BF16_ROWS = 16
ATT_BLOCK = 256
ROW_TILE = 512
PEER_TOKEN_TILE = 512
PEER_EXPERT_TILE = 1024
ROUTER_TILE = 512
FOX_GROUP = 8
DIFF_GROUP = 4
F_PIECES = 3
VMEM_LIMIT = 56 * 1024 * 1024
INT_MIN = -2 ** 31

_NT = (((1,), (1,)), ((), ()))


def _params(sem):
    return pltpu.CompilerParams(dimension_semantics=sem, vmem_limit_bytes=VMEM_LIMIT)


def _rms(x, g):
    return x * lax.rsqrt(jnp.mean(x * x, axis=-1, keepdims=True) + RMS_EPS) * g


def _ones_row(rows, cols):
    return jnp.where(lax.broadcasted_iota(I32, (rows, cols), 0) == 0, 1.0, 0.0).astype(BF16)


def _key_le_query(tb):
    return lax.broadcasted_iota(I32, (tb, tb), 0) <= lax.broadcasted_iota(I32, (tb, tb), 1)


def _ada_kernel(c_ref, w_ref, b_ref, o_ref):
    c = c_ref[...]
    ca = c * jax.nn.sigmoid(c)
    o_ref[0] = jnp.dot(ca, w_ref[0], precision=lax.Precision.HIGHEST,
                       preferred_element_type=F32) + b_ref[0]


def _ada(c, ada_w, ada_b):
    depth, d, n = ada_w.shape
    b = c.shape[0]
    bp = 8
    cp = jnp.zeros((bp, d), F32).at[:b].set(c)
    tn = 1536
    out = pl.pallas_call(
        _ada_kernel,
        grid=(depth, n // tn),
        in_specs=[pl.BlockSpec((bp, d), lambda l, j: (0, 0)),
                  pl.BlockSpec((1, d, tn), lambda l, j: (l, 0, j)),
                  pl.BlockSpec((1, 1, tn), lambda l, j: (l, 0, j))],
        out_specs=pl.BlockSpec((1, bp, tn), lambda l, j: (l, 0, j)),
        out_shape=jax.ShapeDtypeStruct((depth, bp, n), F32),
        compiler_params=_params(("arbitrary", "arbitrary")),
        name="ada_mod",
    )(cp, ada_w, ada_b.reshape(depth, 1, n))
    return out[:, :b]


def _store_transposed(out_ref, rt, dt):
    for c in range(rt.shape[1] // ATT_BLOCK):
        out_ref[c] = rt[:, c * ATT_BLOCK:(c + 1) * ATT_BLOCK].astype(dt)


def _emit_segments(hb, w_ref, wt_ref, segs, outs, kvn_ref):
    k = 0
    for (a, b, kind, dtypes) in segs:
        if kind in ("T", "Tq"):
            r = lax.dot_general(wt_ref[a:b, :], hb, _NT, preferred_element_type=F32)
            if kind == "Tq":
                r = r * (HEAD_DIM ** -0.5)
            for dt in dtypes:
                _store_transposed(outs[k], r, dt)
                k += 1
            continue
        r = jnp.dot(hb, w_ref[:, a:b], preferred_element_type=F32)
        if kind == "kvnorm":
            r = _rms(r, kvn_ref[...])
            outs[k][...] = r.astype(dtypes[0])
            _store_transposed(outs[k + 1], r.T, dtypes[0])
            k += 2
            continue
        for dt in dtypes:
            outs[k][...] = r.astype(dt)
            k += 1


def _normproj_kernel(*refs, segs, has_kvn, has_wt):
    x_ref, gain_ref, sc_ref, sh_ref, w_ref = refs[:5]
    pos = 5
    wt_ref = kvn_ref = None
    if has_wt:
        wt_ref = refs[pos]
        pos += 1
    if has_kvn:
        kvn_ref = refs[pos]
        pos += 1
    outs = list(refs[pos:])
    h = _rms(x_ref[...], gain_ref[...]) * (1.0 + sc_ref[0]) + sh_ref[0]
    _emit_segments(h.astype(BF16), w_ref, wt_ref, segs, outs, kvn_ref)


def _resproj_kernel(*refs, n_mix, segs):
    x_ref = refs[0]
    mix_refs = refs[1:1 + n_mix]
    wo_ref, g_ref, gain_ref, sc_ref, sh_ref, w_ref = refs[1 + n_mix:7 + n_mix]
    outs = list(refs[7 + n_mix:])
    y = None
    off = 0
    for m in mix_refs:
        kdim = m.shape[1]
        t = jnp.dot(m[...], wo_ref[off:off + kdim, :], preferred_element_type=F32)
        y = t if y is None else y + t
        off += kdim
    xn = x_ref[...] + g_ref[0] * y
    outs[0][...] = xn
    h = _rms(xn, gain_ref[...]) * (1.0 + sc_ref[0]) + sh_ref[0]
    hb = h.astype(BF16)
    outs[1][...] = hb
    _emit_segments(hb, w_ref, None, segs, outs[2:], None)


def _seg_out_shapes(t, segs):
    shapes, specs = [], []
    per = ROW_TILE // ATT_BLOCK

    def plain(n, dt):
        shapes.append(jax.ShapeDtypeStruct((t, n), dt))
        specs.append(pl.BlockSpec((ROW_TILE, n), lambda i: (i, 0)))

    def transposed(n, dt):
        shapes.append(jax.ShapeDtypeStruct((t // ATT_BLOCK, n, ATT_BLOCK), dt))
        specs.append(pl.BlockSpec((per, n, ATT_BLOCK), lambda i: (i, 0, 0)))

    for (a, b, kind, dtypes) in segs:
        if kind == "kvnorm":
            plain(b - a, dtypes[0])
            transposed(b - a, dtypes[0])
            continue
        for dt in dtypes:
            (transposed if kind in ("T", "Tq") else plain)(b - a, dt)
    return shapes, specs


def _normproj(x, gain, sc, sh, w, segs, seq, kvn=None, wt=None, name="normproj"):
    t, d = x.shape
    tm = ROW_TILE
    per_b = seq // tm
    n = w.shape[1]
    mod_spec = pl.BlockSpec((1, 1, d), lambda i: (i // per_b, 0, 0))
    in_specs = [pl.BlockSpec((tm, d), lambda i: (i, 0)),
                pl.BlockSpec((1, d), lambda i: (0, 0)), mod_spec, mod_spec,
                pl.BlockSpec((d, n), lambda i: (0, 0))]
    args = [x, gain.reshape(1, d), sc, sh, w]
    if wt is not None:
        in_specs.append(pl.BlockSpec((n, d), lambda i: (0, 0)))
        args.append(wt)
    if kvn is not None:
        in_specs.append(pl.BlockSpec((1, kvn.shape[-1]), lambda i: (0, 0)))
        args.append(kvn.reshape(1, -1))
    shapes, specs = _seg_out_shapes(t, segs)
    return pl.pallas_call(
        functools.partial(_normproj_kernel, segs=segs, has_kvn=kvn is not None, has_wt=wt is not None),
        grid=(t // tm,), in_specs=in_specs, out_specs=specs, out_shape=shapes,
        compiler_params=_params(("arbitrary",)), name=name,
    )(*args)


def _resproj(x, mixes, w_out, g, gain, sc, sh, w, segs, seq, name="resproj"):
    t, d = x.shape
    tm = ROW_TILE
    per_b = seq // tm
    n = w.shape[1]
    mod_spec = pl.BlockSpec((1, 1, d), lambda i: (i // per_b, 0, 0))
    in_specs = [pl.BlockSpec((tm, d), lambda i: (i, 0))]
    in_specs += [pl.BlockSpec((tm, m.shape[1]), lambda i: (i, 0)) for m in mixes]
    in_specs += [pl.BlockSpec(w_out.shape, lambda i: (0, 0)), mod_spec,
                 pl.BlockSpec((1, d), lambda i: (0, 0)), mod_spec, mod_spec,
                 pl.BlockSpec((d, n), lambda i: (0, 0))]
    shapes, specs = _seg_out_shapes(t, segs)
    shapes = [jax.ShapeDtypeStruct((t, d), F32), jax.ShapeDtypeStruct((t, d), BF16)] + shapes
    specs = [pl.BlockSpec((tm, d), lambda i: (i, 0)), pl.BlockSpec((tm, d), lambda i: (i, 0))] + specs
    return pl.pallas_call(
        functools.partial(_resproj_kernel, n_mix=len(mixes), segs=segs),
        grid=(t // tm,), in_specs=in_specs, out_specs=specs, out_shape=shapes,
        compiler_params=_params(("arbitrary",)), name=name,
    )(x, *mixes, w_out, g, gain.reshape(1, d), sc, sh, w)


def _bucket_table(n):
    max_exact = REL_BUCKETS // 2
    d = np.arange(n)
    df = np.maximum(d, 1).astype(np.float32)
    large = max_exact + (np.log(df / max_exact) / math.log(REL_MAX_DIST / max_exact)
                         * (REL_BUCKETS - max_exact)).astype(np.int32)
    large = np.minimum(large, REL_BUCKETS - 1)
    return np.where(d < max_exact, d, large).astype(np.int32)


def _bias_tiles_kernel(rb_ref, bk_ref, o_ref):
    h = pl.program_id(0)
    for t in range(2):
        bt = bk_ref[t]
        acc = jnp.zeros(bt.shape, F32)
        for b in range(REL_BUCKETS):
            acc = jnp.where(bt == b, rb_ref[b, h], acc)
        o_ref[0, t] = acc


def _bias_tiles(rel_bias):
    tb = ATT_BLOCK
    nh = rel_bias.shape[1]
    table = _bucket_table(2 * tb)
    s = np.arange(tb)[:, None]
    t = np.arange(tb)[None, :]
    bk = np.stack([table[np.maximum(t - s, 0)], table[tb + t - s]]).astype(np.int32)
    return pl.pallas_call(
        _bias_tiles_kernel,
        grid=(nh,),
        in_specs=[pl.BlockSpec(memory_space=pltpu.SMEM),
                  pl.BlockSpec((2, tb, tb), lambda h: (0, 0, 0))],
        out_specs=pl.BlockSpec((1, 2, tb, tb), lambda h: (h, 0, 0, 0)),
        out_shape=jax.ShapeDtypeStruct((nh, 2, tb, tb), F32),
        compiler_params=_params(("arbitrary",)), name="bias_tiles",
    )(rel_bias, jnp.asarray(bk))


def _flash_update(s, m, shift_const, acc_old, vaug):
    m_new = jnp.maximum(m, jnp.max(s, axis=0, keepdims=True) + shift_const)
    p = jnp.exp(s - (m_new - shift_const)).astype(BF16)
    acc = jnp.exp(m - m_new) * acc_old + jnp.dot(vaug, p, preferred_element_type=F32)
    return m_new, acc


def _split_bf16(x):
    pieces = []
    for _ in range(F_PIECES):
        p = x.astype(BF16)
        pieces.append(p)
        x = x - p.astype(F32)
    return pieces


def _fox_sel():
    selk = np.zeros((F_PIECES, LANES, C_HEADS * HEAD_DIM), np.float32)
    selq = np.zeros((F_PIECES, C_HEADS * HEAD_DIM, LANES), np.float32)
    onesk = np.zeros((1, C_HEADS * HEAD_DIM), np.float32)
    onesq = np.zeros((C_HEADS * HEAD_DIM, 1), np.float32)
    for h in range(C_HEADS):
        for p in range(F_PIECES):
            selk[p, h, h * HEAD_DIM + p] = -1.0
            selq[p, h * HEAD_DIM + F_PIECES + p, h] = 1.0
            onesk[0, h * HEAD_DIM + F_PIECES + p] = 1.0
            onesq[h * HEAD_DIM + p, 0] = 1.0
    return selk, selq, onesk, onesq


def _foxgate_kernel(f_ref, b_ref, selk_ref, selq_ref, onesk_ref, onesq_ref, kf_ref, qft_ref, *, tb):
    nblk = f_ref.shape[0] // tb
    row = lax.broadcasted_iota(I32, (tb, tb), 0)
    col = lax.broadcasted_iota(I32, (tb, tb), 1)
    tri = jnp.where(col <= row, 1.0, 0.0).astype(F32)

    def body(j, carry):
        z = f_ref[pl.ds(j * tb, tb), :] + b_ref[...]
        ls = -(jnp.maximum(-z, 0.0) + jnp.log(1.0 + jnp.exp(-jnp.abs(z))))
        cs = jnp.dot(tri, ls, precision=lax.Precision.HIGHEST, preferred_element_type=F32) + carry
        kf = onesk_ref[...]
        for p, piece in enumerate(_split_bf16(cs)):
            kf = kf + jnp.dot(piece, selk_ref[p], preferred_element_type=F32)
        kf_ref[pl.ds(j * tb, tb), :] = kf.astype(BF16)
        qf = onesq_ref[...]
        for p, piece in enumerate(_split_bf16(cs.T)):
            qf = qf + jnp.dot(selq_ref[p], piece, preferred_element_type=F32)
        qft_ref[j] = qf.astype(BF16)
        return cs[tb - 1:tb, :]

    lax.fori_loop(0, nblk, body, jnp.zeros((1, LANES), F32))


def _foxgate(f, b_forget, batch, seq):
    tb = ATT_BLOCK
    nblk = seq // tb
    n = C_HEADS * HEAD_DIM
    bpad = jnp.zeros((1, LANES), F32).at[0, :C_HEADS].set(b_forget)
    selk, selq, onesk, onesq = _fox_sel()
    return pl.pallas_call(
        functools.partial(_foxgate_kernel, tb=tb),
        grid=(batch,),
        in_specs=[pl.BlockSpec((seq, LANES), lambda b: (b, 0)),
                  pl.BlockSpec((1, LANES), lambda b: (0, 0)),
                  pl.BlockSpec(selk.shape, lambda b: (0, 0, 0)),
                  pl.BlockSpec(selq.shape, lambda b: (0, 0, 0)),
                  pl.BlockSpec(onesk.shape, lambda b: (0, 0)),
                  pl.BlockSpec(onesq.shape, lambda b: (0, 0))],
        out_specs=[pl.BlockSpec((seq, n), lambda b: (b, 0)),
                   pl.BlockSpec((nblk, n, tb), lambda b: (b, 0, 0))],
        out_shape=[jax.ShapeDtypeStruct((batch * seq, n), BF16),
                   jax.ShapeDtypeStruct((batch * nblk, n, tb), BF16)],
        compiler_params=_params(("arbitrary",)), name="fox_gates",
    )(f, bpad, jnp.asarray(selk, BF16), jnp.asarray(selq, BF16), jnp.asarray(onesk), jnp.asarray(onesq))


def _fox_kernel(qt_ref, qft_ref, k_ref, kf_ref, vt_ref, o_ref, kaug_ref, vaug_ref, *, tb, nq):
    qi = pl.program_id(2)
    hd = HEAD_DIM
    heads = range(FOX_GROUP)

    @pl.when(qi == 0)
    def _():
        ones_rows = _ones_row(hd, tb)
        for h in heads:
            kaug_ref[h, :, 0:hd] = k_ref[:, h * hd:(h + 1) * hd]
            kaug_ref[h, :, hd:2 * hd] = kf_ref[:, h * hd:(h + 1) * hd]

            def fill(j, c, h=h):
                vaug_ref[h, j, 0:hd, :] = vt_ref[j, h * hd:(h + 1) * hd, :]
                vaug_ref[h, j, hd:2 * hd, :] = ones_rows
                return c

            lax.fori_loop(0, nq, fill, 0)

    qa = [jnp.concatenate([qt_ref[0, h * hd:(h + 1) * hd, :], qft_ref[0, h * hd:(h + 1) * hd, :]], axis=0)
          for h in heads]
    keep = _key_le_query(tb)

    def step(j, carry, masked):
        scores = [jnp.dot(kaug_ref[h, pl.ds(j * tb, tb), :], qa[h], preferred_element_type=F32)
                  for h in heads]
        out = []
        for h in heads:
            m, acc = carry[h]
            s = jnp.where(keep, scores[h], NEG_INF) if masked else scores[h]
            out.append(_flash_update(s, m, 0.0, acc, vaug_ref[h, j]))
        return tuple(out)

    init = tuple((jnp.full((1, tb), NEG_INF, F32), jnp.zeros((2 * hd, tb), F32)) for _ in heads)
    carry = lax.fori_loop(0, qi, lambda j, c: step(j, c, False), init)
    carry = step(qi, carry, True)
    for pair in range(FOX_GROUP // 2):
        ot = jnp.concatenate([carry[h][1][0:hd] / carry[h][1][hd:hd + 1] for h in (2 * pair, 2 * pair + 1)], axis=0)
        o_ref[:, pair * 2 * hd:(pair + 1) * 2 * hd] = ot.T.astype(o_ref.dtype)


def _fox_attention(qt, qft, k, kf, vt, batch, seq):
    tb = ATT_BLOCK
    nq = seq // tb
    groups = C_HEADS // FOX_GROUP
    gw = FOX_GROUP * HEAD_DIM
    return pl.pallas_call(
        functools.partial(_fox_kernel, tb=tb, nq=nq),
        grid=(batch, groups, nq),
        in_specs=[pl.BlockSpec((1, gw, tb), lambda b, g, i: (b * nq + i, g, 0)),
                  pl.BlockSpec((1, gw, tb), lambda b, g, i: (b * nq + i, g, 0)),
                  pl.BlockSpec((seq, gw), lambda b, g, i: (b, g)),
                  pl.BlockSpec((seq, gw), lambda b, g, i: (b, g)),
                  pl.BlockSpec((nq, gw, tb), lambda b, g, i: (b, g, 0))],
        out_specs=pl.BlockSpec((tb, gw), lambda b, g, i: (b * nq + i, g)),
        out_shape=jax.ShapeDtypeStruct((batch * seq, C_HEADS * HEAD_DIM), BF16),
        scratch_shapes=[pltpu.VMEM((FOX_GROUP, seq, 2 * HEAD_DIM), BF16),
                        pltpu.VMEM((FOX_GROUP, nq, 2 * HEAD_DIM, tb), BF16)],
        compiler_params=_params(("arbitrary", "arbitrary", "arbitrary")), name="fox_attention",
    )(qt, qft, k, kf, vt)


def _diff_kernel(rb_ref, qt_ref, k_ref, vt_ref, tiles_ref, lam_ref, g_ref, o_ref,
                 ks_ref, vaug_ref, acc_ref, *, tb, nq, lam_init):
    g = pl.program_id(1)
    qi = pl.program_id(2)
    hd = HEAD_DIM
    dv = 2 * hd
    chains = [(hh, s_) for hh in range(DIFF_GROUP) for s_ in range(2)]

    @pl.when(qi == 0)
    def _():
        ones_rows = _ones_row(BF16_ROWS, tb)
        for c, (hh, s_) in enumerate(chains):
            ks_ref[c] = k_ref[:, hh * dv + s_ * hd:hh * dv + (s_ + 1) * hd]
        for hh in range(DIFF_GROUP):
            def fill(j, c, hh=hh):
                vaug_ref[hh, j, 0:dv, :] = vt_ref[j, hh * dv:(hh + 1) * dv, :]
                vaug_ref[hh, j, dv:dv + BF16_ROWS, :] = ones_rows
                return c

            lax.fori_loop(0, nq, fill, 0)

    qs = [qt_ref[0, hh * dv + s_ * hd:hh * dv + (s_ + 1) * hd, :] for (hh, s_) in chains]
    cfar = [rb_ref[REL_BUCKETS - 1, A_HEADS + g * DIFF_GROUP + hh] for hh in range(DIFF_GROUP)]
    keep = _key_le_query(tb)
    acc_ref[...] = jnp.zeros_like(acc_ref)

    def step(j, ms, kind):
        scores = [jnp.dot(ks_ref[c, pl.ds(j * tb, tb), :], qs[c], preferred_element_type=F32)
                  for c in range(len(chains))]
        out = []
        for c, (hh, s_) in enumerate(chains):
            if kind == "far":
                s, shift = scores[c], cfar[hh]
            else:
                s, shift = scores[c] + tiles_ref[hh, 0 if kind == "diag" else 1], 0.0
                if kind == "diag":
                    s = jnp.where(keep, s, NEG_INF)
            m_new, acc = _flash_update(s, ms[c], shift, acc_ref[c], vaug_ref[hh, j])
            acc_ref[c] = acc
            out.append(m_new)
        return tuple(out)

    ms = tuple(jnp.full((1, tb), NEG_INF, F32) for _ in chains)
    ms = lax.fori_loop(0, jnp.maximum(qi - 1, 0), lambda j, c: step(j, c, "far"), ms)
    ms = lax.cond(qi >= 1, lambda c: step(qi - 1, c, "near"), lambda c: c, ms)
    step(qi, ms, "diag")

    lv = lam_ref[...]
    lam = (jnp.exp(jnp.sum(lv[0:1] * lv[1:2], axis=-1, keepdims=True))
           - jnp.exp(jnp.sum(lv[2:3] * lv[3:4], axis=-1, keepdims=True)) + lam_init)
    for hh in range(DIFF_GROUP):
        a1 = acc_ref[2 * hh]
        a2 = acc_ref[2 * hh + 1]
        ot = a1[0:dv] / a1[dv:dv + 1] - lam * (a2[0:dv] / a2[dv:dv + 1])
        o = _rms(ot.T, g_ref[...]) * (1.0 - lam_init)
        o_ref[:, hh * dv:(hh + 1) * dv] = o.astype(o_ref.dtype)


def _diff_attention(rel_bias, qt, k, vt, tiles, lam_vec, subln, batch, seq, lam_init):
    tb = ATT_BLOCK
    nq = seq // tb
    groups = B_HEADS // DIFF_GROUP
    gw = DIFF_GROUP * 2 * HEAD_DIM
    dv = 2 * HEAD_DIM
    return pl.pallas_call(
        functools.partial(_diff_kernel, tb=tb, nq=nq, lam_init=lam_init),
        grid=(batch, groups, nq),
        in_specs=[pl.BlockSpec(memory_space=pltpu.SMEM),
                  pl.BlockSpec((1, gw, tb), lambda b, g, i: (b * nq + i, g, 0)),
                  pl.BlockSpec((seq, gw), lambda b, g, i: (b, g)),
                  pl.BlockSpec((nq, gw, tb), lambda b, g, i: (b, g, 0)),
                  pl.BlockSpec((DIFF_GROUP, 2, tb, tb), lambda b, g, i: (A_HEADS // DIFF_GROUP + g, 0, 0, 0)),
                  pl.BlockSpec((4, HEAD_DIM), lambda b, g, i: (0, 0)),
                  pl.BlockSpec((1, dv), lambda b, g, i: (0, 0))],
        out_specs=pl.BlockSpec((tb, gw), lambda b, g, i: (b * nq + i, g)),
        out_shape=jax.ShapeDtypeStruct((batch * seq, B_HEADS * dv), BF16),
        scratch_shapes=[pltpu.VMEM((2 * DIFF_GROUP, seq, HEAD_DIM), BF16),
                        pltpu.VMEM((DIFF_GROUP, nq, dv + BF16_ROWS, tb), BF16),
                        pltpu.VMEM((2 * DIFF_GROUP, dv + BF16_ROWS, tb), F32)],
        compiler_params=_params(("arbitrary", "arbitrary", "arbitrary")), name="diff_attention",
    )(rel_bias, qt, k, vt, tiles, lam_vec, subln.reshape(1, -1))


def _dsa_kernel(rb_ref, qat_ref, qit_ref, wit_ref, kw_ref, ckv_ref, ckvt_ref, wuk_ref, wuvt_ref, tiles_ref,
                o_ref, keys_ref, selb_ref, qlat_ref, vaug_ref, acc_ref, *, tb, topk, nq):
    qb = pl.program_id(1)
    nblk = qb + 1
    hd = HEAD_DIM
    keep = _key_le_query(tb)

    @pl.when(qb == 0)
    def _():
        ones_rows = _ones_row(BF16_ROWS, tb)

        def fill(j, c):
            vaug_ref[j, 0:A_LATENT, :] = ckvt_ref[j]
            vaug_ref[j, A_LATENT:A_LATENT + BF16_ROWS, :] = ones_rows
            return c

        lax.fori_loop(0, nq, fill, 0)

    for h in range(A_HEADS):
        ql = jnp.dot(wuk_ref[h], qat_ref[0, h * hd:(h + 1) * hd, :], preferred_element_type=F32)
        qlat_ref[h] = ql.astype(BF16)

    wrows = [wit_ref[0, IDX_DIM + h:IDX_DIM + h + 1, :] * (IDX_HEADS ** -0.5) for h in range(IDX_HEADS)]
    zpad = jnp.zeros((LANES - IDX_DIM, tb), BF16)
    qi_pad = [jnp.concatenate([qit_ref[0, h * IDX_DIM:(h + 1) * IDX_DIM, :], zpad], axis=0)
              for h in range(IDX_HEADS)]

    def index_keys(j, masked):
        kb = kw_ref[pl.ds(j * tb, tb), :]
        isc = jnp.zeros((tb, tb), F32)
        for h in range(IDX_HEADS):
            li = jnp.dot(kb, qi_pad[h], preferred_element_type=F32)
            isc = isc + jnp.maximum(li, 0.0) * wrows[h]
        isc = jnp.where(isc == 0.0, 0.0, isc)
        bits = pltpu.bitcast(isc, I32)
        key = bits ^ ((bits >> 31) & 0x7FFFFFFF)
        if masked:
            key = jnp.where(keep, key, INT_MIN)
        keys_ref[j] = key

    def p1(j, c):
        index_keys(j, False)
        return c

    lax.fori_loop(0, qb, p1, 0)
    index_keys(qb, True)

    def count(pred):
        def body(j, acc):
            ind = jnp.where(pred(keys_ref[j]), 1.0, 0.0)
            return acc + jnp.sum(ind.reshape(tb // 8, 8, tb), axis=0)
        acc = lax.fori_loop(0, nblk, body, jnp.zeros((8, tb), F32))
        return jnp.sum(acc, axis=0, keepdims=True)

    kth = jnp.where(count(lambda k: k >= 0) >= topk, 0, INT_MIN).astype(I32)

    def bs(i, kth):
        cand = kth | lax.shift_left(jnp.int32(1), 30 - i)
        return jnp.where(count(lambda k: k >= cand) >= topk, cand, kth)

    kth = lax.fori_loop(0, 31, bs, kth)
    need = topk - count(lambda k: k > kth)

    lower = jnp.where(lax.broadcasted_iota(I32, (tb, tb), 1) <= lax.broadcasted_iota(I32, (tb, tb), 0),
                      1.0, 0.0).astype(BF16)

    def mask_block(j, seen, masked):
        key = keys_ref[j]
        eq = key == kth
        pre = jnp.dot(lower, jnp.where(eq, 1.0, 0.0).astype(BF16), preferred_element_type=F32)
        sel = (key > kth) | (eq & (pre + seen <= need))
        if masked:
            sel = sel & keep
        selb_ref[j] = jnp.where(sel, 0.0, NEG_INF)
        return seen + pre[tb - 1:tb, :]

    seen = lax.fori_loop(0, qb, lambda j, s: mask_block(j, s, False), jnp.zeros((1, tb), F32))
    mask_block(qb, seen, True)

    cfar = [rb_ref[REL_BUCKETS - 1, h] for h in range(A_HEADS)]
    acc_ref[...] = jnp.zeros_like(acc_ref)

    def step(j, ms, kind):
        kvb = ckv_ref[pl.ds(j * tb, tb), :]
        scores = [jnp.dot(kvb, qlat_ref[h], preferred_element_type=F32) for h in range(A_HEADS)]
        sb = selb_ref[j]
        out = []
        for h in range(A_HEADS):
            if kind == "far":
                s, shift = scores[h] + sb, cfar[h]
            else:
                s, shift = scores[h] + (tiles_ref[h, 0 if kind == "diag" else 1] + sb), 0.0
            m_new, acc = _flash_update(s, ms[h], shift, acc_ref[h], vaug_ref[j])
            acc_ref[h] = acc
            out.append(m_new)
        return tuple(out)

    ms = tuple(jnp.full((1, tb), NEG_INF, F32) for _ in range(A_HEADS))
    ms = lax.fori_loop(0, jnp.maximum(qb - 1, 0), lambda j, c: step(j, c, "far"), ms)
    ms = lax.cond(qb >= 1, lambda c: step(qb - 1, c, "near"), lambda c: c, ms)
    step(qb, ms, "diag")

    outs = []
    for h in range(A_HEADS):
        a = acc_ref[h]
        o_lat = (a[0:A_LATENT] / a[A_LATENT:A_LATENT + 1]).astype(BF16)
        outs.append(jnp.dot(wuvt_ref[h], o_lat, preferred_element_type=F32))
    o_ref[...] = jnp.concatenate(outs, axis=0).T.astype(o_ref.dtype)


def _dsa_attention(rel_bias, qat, qit, wit, kw, ckv, ckvt, wuk, wuvt, tiles, batch, seq):
    tb = ATT_BLOCK
    nq = seq // tb
    topk = min(TOPK_MAX, seq // 4)
    nqa = A_HEADS * HEAD_DIM
    nqi = IDX_HEADS * IDX_DIM
    aug = A_LATENT + BF16_ROWS
    return pl.pallas_call(
        functools.partial(_dsa_kernel, tb=tb, topk=topk, nq=nq),
        grid=(batch, nq),
        in_specs=[pl.BlockSpec(memory_space=pltpu.SMEM),
                  pl.BlockSpec((1, nqa, tb), lambda b, i: (b * nq + i, 0, 0)),
                  pl.BlockSpec((1, nqi, tb), lambda b, i: (b * nq + i, 0, 0)),
                  pl.BlockSpec((1, LANES, tb), lambda b, i: (b * nq + i, 0, 0)),
                  pl.BlockSpec((seq, LANES), lambda b, i: (b, 0)),
                  pl.BlockSpec((seq, A_LATENT), lambda b, i: (b, 0)),
                  pl.BlockSpec((nq, A_LATENT, tb), lambda b, i: (b, 0, 0)),
                  pl.BlockSpec(wuk.shape, lambda b, i: (0, 0, 0)),
                  pl.BlockSpec(wuvt.shape, lambda b, i: (0, 0, 0)),
                  pl.BlockSpec((A_HEADS, 2, tb, tb), lambda b, i: (0, 0, 0, 0))],
        out_specs=pl.BlockSpec((tb, nqa), lambda b, i: (b * nq + i, 0)),
        out_shape=jax.ShapeDtypeStruct((batch * seq, nqa), BF16),
        scratch_shapes=[pltpu.VMEM((nq, tb, tb), I32), pltpu.VMEM((nq, tb, tb), F32),
                        pltpu.VMEM((A_HEADS, A_LATENT, tb), BF16),
                        pltpu.VMEM((nq, aug, tb), BF16),
                        pltpu.VMEM((A_HEADS, aug, tb), F32)],
        compiler_params=_params(("arbitrary", "arbitrary")), name="dsa_attention",
    )(rel_bias, qat, qit, wit, kw, ckv, ckvt, wuk, wuvt, tiles)


def _top_sorted(x, k):
    rows = lax.broadcasted_iota(I32, (k, x.shape[1]), 0)
    out = jnp.zeros((k, x.shape[1]), F32)
    rank = jnp.full(x.shape, 127.0, F32)
    for r in range(k):
        m = jnp.max(x, axis=0, keepdims=True)
        hit = x == m
        out = jnp.where(rows == r, m, out)
        rank = jnp.where(hit, float(r), rank)
        x = jnp.where(hit, -jnp.inf, x)
    return out, rank


def _router_kernel(q_ref, keys_ref, c1_ref, a_ref, r2_ref, w_ref, *, tr):
    k = PEER_TOPK
    nk = keys_ref.shape[1]
    row8 = lax.broadcasted_iota(I32, (8, LANES), 0)
    for h in range(PEER_HEADS):
        for tc in range(tr // LANES):
            tok = slice(tc * LANES, (tc + 1) * LANES)
            qh = q_ref[tok, :]
            s1 = lax.dot_general(keys_ref[0], qh[:, (2 * h) * nk:(2 * h + 1) * nk], _NT,
                                 preferred_element_type=F32)
            s2 = lax.dot_general(keys_ref[1], qh[:, (2 * h + 1) * nk:(2 * h + 2) * nk], _NT,
                                 preferred_element_type=F32)
            a, rank1 = _top_sorted(s1, k)
            b, rank2 = _top_sorted(s2, k)
            b8 = b[0:8]
            parts = [a[0:1] + b, a[1:2] + b8]
            for i, lim in ((2, 5), (3, 4), (4, 3), (5, 2), (6, 2), (7, 2)):
                parts.append(jnp.where(row8 < lim, a[i:i + 1] + b8, -jnp.inf))
            parts.append(a[8:16] + b[0:1])
            cand = jnp.concatenate(parts, axis=0)
            x = cand
            thr = None
            for _ in range(k):
                thr = jnp.max(x, axis=0, keepdims=True)
                x = jnp.where(x == thr, -jnp.inf, x)
            mx = a[0:1] + b[0:1]
            z = jnp.sum(jnp.where(cand >= thr, jnp.exp(cand - mx), 0.0), axis=0, keepdims=True)
            c1 = jnp.zeros((nk, LANES), F32)
            for r in range(k):
                cnt = jnp.sum(jnp.where(a[r:r + 1] + b >= thr, 1.0, 0.0), axis=0, keepdims=True)
                c1 = jnp.where(rank1 == float(r), cnt, c1)
            c1_ref[h, :, tok] = c1
            a_ref[h, :, tok] = jnp.exp(s1 - a[0:1]) / z
            r2_ref[h, :, tok] = rank2.astype(BF16)
            w_ref[h, :, tok] = jnp.exp(s2 - b[0:1]).astype(BF16)


def _router(q, sub_keys):
    t = q.shape[0]
    tr = ROUTER_TILE
    nk = sub_keys.shape[1]
    shp32 = jax.ShapeDtypeStruct((PEER_HEADS, nk, t), F32)
    shp16 = jax.ShapeDtypeStruct((PEER_HEADS, nk, t), BF16)
    spec = pl.BlockSpec((PEER_HEADS, nk, tr), lambda i: (0, 0, i))
    return pl.pallas_call(
        functools.partial(_router_kernel, tr=tr),
        grid=(t // tr,),
        in_specs=[pl.BlockSpec((tr, q.shape[1]), lambda i: (i, 0)),
                  pl.BlockSpec(sub_keys.shape, lambda i: (0, 0, 0))],
        out_specs=[spec] * 4, out_shape=[shp32, shp32, shp16, shp16],
        compiler_params=_params(("arbitrary",)), name="peer_router",
    )(q, sub_keys)


def _peer_kernel(h_ref, u_ref, vt_ref, c1_ref, a_ref, r2_ref, w_ref, x_ref, g_ref, *rest,
                 tm, te, nk, ne, final):
    if final:
        gf_ref, o_ref, act0_ref, act1_ref, p_ref, acc_ref, r2s_ref, ws_ref = rest
    else:
        o_ref, act0_ref, act1_ref, p_ref, acc_ref, r2s_ref, ws_ref = rest
    s = pl.program_id(0)
    e0 = jnp.maximum(s - 1, 0) % ne
    nsub = nk // BF16_ROWS

    @pl.when(s == 0)
    def _():
        act1_ref[...] = jnp.zeros_like(act1_ref)

    @pl.when(e0 == 0)
    def _():
        acc_ref[...] = jnp.zeros_like(acc_ref)
        r2s_ref[...] = r2_ref[...]
        ws_ref[...] = w_ref[...]

    def main(cur_ref, prev_ref):
        tile = 2 * LANES

        def gate_piece(ii_list, tc_list):
            for tc in tc_list:
                tok = slice(tc * LANES, (tc + 1) * LANES)
                gates = [[None] * nsub for _ in ii_list]
                for h in range(PEER_HEADS):
                    r2 = [r2s_ref[h, k * BF16_ROWS:(k + 1) * BF16_ROWS, tok] for k in range(nsub)]
                    w2 = [ws_ref[h, k * BF16_ROWS:(k + 1) * BF16_ROWS, tok] for k in range(nsub)]
                    for n, ii in enumerate(ii_list):
                        c = jnp.broadcast_to(c1_ref[h, ii:ii + 1, tok], (BF16_ROWS, LANES)).astype(BF16)
                        a = jnp.broadcast_to(a_ref[h, ii:ii + 1, tok], (BF16_ROWS, LANES)).astype(BF16)
                        for k in range(nsub):
                            t = jnp.where(r2[k] < c, w2[k] * a, 0)
                            gates[n][k] = t if gates[n][k] is None else gates[n][k] + t
                for n, ii in enumerate(ii_list):
                    for k in range(nsub):
                        rows = slice(ii * nk + k * BF16_ROWS, ii * nk + (k + 1) * BF16_ROWS)
                        p_ref[rows, tok] = gates[n][k] * jax.nn.gelu(prev_ref[rows, tok])

        ii_per = tile // nk
        tc_per = tile // LANES
        for nt in range(tm // tile):
            cols = slice(nt * tile, (nt + 1) * tile)
            for kt in range(te // tile):
                rows = slice(kt * tile, (kt + 1) * tile)
                gate_piece(list(range(kt * ii_per, (kt + 1) * ii_per)),
                           list(range(nt * tc_per, (nt + 1) * tc_per)))
                acc_ref[:, cols] += jnp.dot(vt_ref[:, rows], p_ref[rows, cols], preferred_element_type=F32)
                if kt % 2 == 1:
                    urows = slice((kt // 2) * (te // 2), (kt // 2 + 1) * (te // 2))
                    cur_ref[urows, cols] = lax.dot_general(u_ref[urows, :], h_ref[cols, :], _NT,
                                                           preferred_element_type=F32).astype(BF16)

    @pl.when(s % 2 == 0)
    def _():
        main(act0_ref, act1_ref)

    @pl.when(s % 2 == 1)
    def _():
        main(act1_ref, act0_ref)

    @pl.when((e0 == ne - 1) & (s > 0))
    def _():
        xn = x_ref[...] + g_ref[0] * acc_ref[...].T
        if final:
            xn = _rms(xn, gf_ref[...])
        o_ref[...] = xn


def _peer(h, u, vt, c1, a, r2, w, x, g2, seq, final_gain=None):
    t, d = x.shape
    tm, te = PEER_TOKEN_TILE, PEER_EXPERT_TILE
    ne = u.shape[0] // te
    nk = r2.shape[1]
    rows_per = te // nk
    per_b = seq // tm
    final = final_gain is not None
    steps = (t // tm) * ne + 1

    def cur(s):
        s1 = jnp.minimum(s, steps - 2)
        return s1 // ne, s1 % ne

    def prev(s):
        s0 = jnp.maximum(s - 1, 0)
        return s0 // ne, s0 % ne

    in_specs = [pl.BlockSpec((tm, d), lambda s: (cur(s)[0], 0)),
                pl.BlockSpec((te, d), lambda s: (cur(s)[1], 0)),
                pl.BlockSpec((d, te), lambda s: (0, prev(s)[1])),
                pl.BlockSpec((PEER_HEADS, rows_per, tm), lambda s: (0, prev(s)[1], prev(s)[0])),
                pl.BlockSpec((PEER_HEADS, rows_per, tm), lambda s: (0, prev(s)[1], prev(s)[0])),
                pl.BlockSpec((PEER_HEADS, nk, tm), lambda s: (0, 0, prev(s)[0])),
                pl.BlockSpec((PEER_HEADS, nk, tm), lambda s: (0, 0, prev(s)[0])),
                pl.BlockSpec((tm, d), lambda s: (prev(s)[0], 0)),
                pl.BlockSpec((1, 1, d), lambda s: (prev(s)[0] // per_b, 0, 0))]
    args = [h, u, vt, c1, a, r2, w, x, g2]
    if final:
        in_specs.append(pl.BlockSpec((1, d), lambda s: (0, 0)))
        args.append(final_gain.reshape(1, d))
    return pl.pallas_call(
        functools.partial(_peer_kernel, tm=tm, te=te, nk=nk, ne=ne, final=final),
        grid=(steps,),
        in_specs=in_specs,
        out_specs=pl.BlockSpec((tm, d), lambda s: (prev(s)[0], 0)),
        out_shape=jax.ShapeDtypeStruct((t, d), F32),
        scratch_shapes=[pltpu.VMEM((te, tm), BF16), pltpu.VMEM((te, tm), BF16),
                        pltpu.VMEM((te, tm), BF16), pltpu.VMEM((d, tm), F32),
                        pltpu.VMEM((PEER_HEADS, nk, tm), BF16), pltpu.VMEM((PEER_HEADS, nk, tm), BF16)],
        compiler_params=_params(("arbitrary",)), name="peer_experts",
    )(*args)


def _pad_cols(w, n):
    return jnp.pad(w, ((0, 0), (0, n - w.shape[1])))


def _even_w_in(w):
    na, ni, nb = A_HEADS * HEAD_DIM, IDX_HEADS * IDX_DIM, B_HEADS * 2 * HEAD_DIM
    o = np.cumsum([0, na, A_LATENT, ni, IDX_DIM, IDX_HEADS, nb, nb, nb])
    kw = _pad_cols(w[:, o[3]:o[5]], LANES)
    return jnp.concatenate([w[:, o[0]:o[3]], kw, w[:, o[5]:o[8]]], axis=1).astype(BF16)


def kernel(x, c, rel_bias, ada_w, ada_b, norm_mix, norm_ffn, norm_final, even_w_in, even_w_out,
           a_kv_norm, a_w_uk, a_w_uv, b_lambda, b_subln, odd_w_in, odd_b_forget, odd_w_out,
           peer_w_q, peer_sub_keys, peer_u, peer_v):
    batch, seq, d = x.shape
    depth = ada_w.shape[0]
    t = batch * seq
    assert seq % ROW_TILE == 0 and seq % ATT_BLOCK == 0 and t % PEER_TOKEN_TILE == 0

    mod = _ada(c, ada_w, ada_b)
    tiles = _bias_tiles(rel_bias)

    na, ni, nb = A_HEADS * HEAD_DIM, IDX_HEADS * IDX_DIM, B_HEADS * 2 * HEAD_DIM
    nc = C_HEADS * HEAD_DIM
    o = np.cumsum([0, na, A_LATENT, ni, LANES, nb, nb, nb])
    even_segs = [(o[0], o[1], "Tq", [BF16]), (o[1], o[2], "kvnorm", [BF16]), (o[2], o[3], "Tq", [BF16]),
                 (o[3], o[4], None, [BF16]), (o[3], o[4], "T", [F32]), (o[4], o[5], "Tq", [BF16]),
                 (o[5], o[6], None, [BF16]), (o[6], o[7], "T", [BF16])]
    odd_segs = [(0, nc, "Tq", [BF16]), (nc, 2 * nc, None, [BF16]), (2 * nc, 3 * nc, "T", [BF16]),
                (3 * nc, 3 * nc + LANES, None, [F32])]
    nq_peer = peer_w_q.shape[2]
    peer_segs = [(0, nq_peer, None, [BF16])]

    xt = x.reshape(t, d)
    for layer in range(depth):
        m6 = mod[layer].reshape(batch, 6, 1, d)
        sh1, sc1, g1, sh2, sc2, g2 = [m6[:, i] for i in range(6)]
        if layer % 2 == 0:
            e = layer // 2
            lam_init = 0.8 - 0.6 * math.exp(-0.3 * layer)
            w_in = _even_w_in(even_w_in[e])
            qat, ckv, ckvt, qit, kw, wit, qbt, k_b, vbt = _normproj(
                xt, norm_mix[layer], sc1, sh1, w_in, even_segs, seq, kvn=a_kv_norm[e], wt=w_in.T,
                name="even_in_proj")
            wuk = jnp.transpose(a_w_uk[e], (1, 0, 2)).astype(BF16)
            wuvt = jnp.transpose(a_w_uv[e], (1, 2, 0)).astype(BF16)
            o_a = _dsa_attention(rel_bias, qat, qit, wit, kw, ckv, ckvt, wuk, wuvt, tiles, batch, seq)
            o_b = _diff_attention(rel_bias, qbt, k_b, vbt, tiles, b_lambda[e], b_subln[e], batch, seq, lam_init)
            mixes = [o_a, o_b]
            w_out = even_w_out[e].astype(BF16)
        else:
            od = layer // 2
            w_in = _pad_cols(odd_w_in[od], 3 * nc + LANES).astype(BF16)
            qt, k, vt, f = _normproj(xt, norm_mix[layer], sc1, sh1, w_in, odd_segs, seq, wt=w_in.T,
                                     name="odd_in_proj")
            kf, qft = _foxgate(f, odd_b_forget[od], batch, seq)
            mixes = [_fox_attention(qt, qft, k, kf, vt, batch, seq)]
            w_out = odd_w_out[od].astype(BF16)
        xt, h2, qp = _resproj(xt, mixes, w_out, g1, norm_ffn[layer], sc2, sh2,
                              peer_w_q[layer].astype(BF16), peer_segs, seq)
        c1, a, r2, w = _router(qp, peer_sub_keys[layer].astype(BF16))
        xt = _peer(h2, peer_u[layer].astype(BF16), peer_v[layer].T.astype(BF16), c1, a, r2, w,
                   xt, g2, seq, final_gain=norm_final if layer == depth - 1 else None)
    return xt.reshape(batch, seq, d)
```

```python
import functools
import math

import numpy as np
import jax
import jax.numpy as jnp
from jax import lax
from jax.experimental import pallas as pl
from jax.experimental.pallas import tpu as pltpu

F32 = jnp.float32
BF16 = jnp.bfloat16
I32 = jnp.int32

HEAD_DIM = 64
RMS_EPS = 1e-6
NEG_INF = -1e30
A_HEADS = 8
A_LATENT = 256
IDX_HEADS = 8
IDX_DIM = 64
TOPK_MAX = 256
B_HEADS = 4
C_HEADS = 16
REL_BUCKETS = 32
REL_MAX_DIST = 128
PEER_HEADS = 8
PEER_TOPK = 16

LANES = 128
BF16_ROWS = 16
ATT_BLOCK = 256
ROW_TILE = 512
PEER_TOKEN_TILE = 512
PEER_EXPERT_TILE = 1024
ROUTER_TILE = 512
FOX_GROUP = 8
DIFF_GROUP = 4
F_PIECES = 3
VMEM_LIMIT = 56 * 1024 * 1024
INT_MIN = -2 ** 31

_NT = (((1,), (1,)), ((), ()))


def _params(sem):
    return pltpu.CompilerParams(dimension_semantics=sem, vmem_limit_bytes=VMEM_LIMIT)


def _rms(x, g):
    return x * lax.rsqrt(jnp.mean(x * x, axis=-1, keepdims=True) + RMS_EPS) * g


def _ones_row(rows, cols):
    return jnp.where(lax.broadcasted_iota(I32, (rows, cols), 0) == 0, 1.0, 0.0).astype(BF16)


def _key_le_query(tb):
    return lax.broadcasted_iota(I32, (tb, tb), 0) <= lax.broadcasted_iota(I32, (tb, tb), 1)


def _ada_kernel(c_ref, w_ref, b_ref, o_ref):
    c = c_ref[...]
    ca = c * jax.nn.sigmoid(c)
    o_ref[0] = jnp.dot(ca, w_ref[0], precision=lax.Precision.HIGHEST,
                       preferred_element_type=F32) + b_ref[0]


def _ada(c, ada_w, ada_b):
    depth, d, n = ada_w.shape
    b = c.shape[0]
    bp = 8
    cp = jnp.zeros((bp, d), F32).at[:b].set(c)
    tn = 1536
    out = pl.pallas_call(
        _ada_kernel,
        grid=(depth, n // tn),
        in_specs=[pl.BlockSpec((bp, d), lambda l, j: (0, 0)),
                  pl.BlockSpec((1, d, tn), lambda l, j: (l, 0, j)),
                  pl.BlockSpec((1, 1, tn), lambda l, j: (l, 0, j))],
        out_specs=pl.BlockSpec((1, bp, tn), lambda l, j: (l, 0, j)),
        out_shape=jax.ShapeDtypeStruct((depth, bp, n), F32),
        compiler_params=_params(("arbitrary", "arbitrary")),
        name="ada_mod",
    )(cp, ada_w, ada_b.reshape(depth, 1, n))
    return out[:, :b]


def _store_transposed(out_ref, rt, dt):
    for c in range(rt.shape[1] // ATT_BLOCK):
        out_ref[c] = rt[:, c * ATT_BLOCK:(c + 1) * ATT_BLOCK].astype(dt)


def _emit_segments(hb, w_ref, wt_ref, segs, outs, kvn_ref):
    k = 0
    for (a, b, kind, dtypes) in segs:
        if kind in ("T", "Tq"):
            r = lax.dot_general(wt_ref[a:b, :], hb, _NT, preferred_element_type=F32)
            if kind == "Tq":
                r = r * (HEAD_DIM ** -0.5)
            for dt in dtypes:
                _store_transposed(outs[k], r, dt)
                k += 1
            continue
        r = jnp.dot(hb, w_ref[:, a:b], preferred_element_type=F32)
        if kind == "kvnorm":
            r = _rms(r, kvn_ref[...])
            outs[k][...] = r.astype(dtypes[0])
            _store_transposed(outs[k + 1], r.T, dtypes[0])
            k += 2
            continue
        for dt in dtypes:
            outs[k][...] = r.astype(dt)
            k += 1


def _normproj_kernel(*refs, segs, has_kvn, has_wt):
    x_ref, gain_ref, sc_ref, sh_ref, w_ref = refs[:5]
    pos = 5
    wt_ref = kvn_ref = None
    if has_wt:
        wt_ref = refs[pos]
        pos += 1
    if has_kvn:
        kvn_ref = refs[pos]
        pos += 1
    outs = list(refs[pos:])
    h = _rms(x_ref[...], gain_ref[...]) * (1.0 + sc_ref[0]) + sh_ref[0]
    _emit_segments(h.astype(BF16), w_ref, wt_ref, segs, outs, kvn_ref)


def _resproj_kernel(*refs, n_mix, segs):
    x_ref = refs[0]
    mix_refs = refs[1:1 + n_mix]
    wo_ref, g_ref, gain_ref, sc_ref, sh_ref, w_ref = refs[1 + n_mix:7 + n_mix]
    outs = list(refs[7 + n_mix:])
    y = None
    off = 0
    for m in mix_refs:
        kdim = m.shape[1]
        t = jnp.dot(m[...], wo_ref[off:off + kdim, :], preferred_element_type=F32)
        y = t if y is None else y + t
        off += kdim
    xn = x_ref[...] + g_ref[0] * y
    outs[0][...] = xn
    h = _rms(xn, gain_ref[...]) * (1.0 + sc_ref[0]) + sh_ref[0]
    hb = h.astype(BF16)
    outs[1][...] = hb
    _emit_segments(hb, w_ref, None, segs, outs[2:], None)


def _seg_out_shapes(t, segs):
    shapes, specs = [], []
    per = ROW_TILE // ATT_BLOCK

    def plain(n, dt):
        shapes.append(jax.ShapeDtypeStruct((t, n), dt))
        specs.append(pl.BlockSpec((ROW_TILE, n), lambda i: (i, 0)))

    def transposed(n, dt):
        shapes.append(jax.ShapeDtypeStruct((t // ATT_BLOCK, n, ATT_BLOCK), dt))
        specs.append(pl.BlockSpec((per, n, ATT_BLOCK), lambda i: (i, 0, 0)))

    for (a, b, kind, dtypes) in segs:
        if kind == "kvnorm":
            plain(b - a, dtypes[0])
            transposed(b - a, dtypes[0])
            continue
        for dt in dtypes:
            (transposed if kind in ("T", "Tq") else plain)(b - a, dt)
    return shapes, specs


def _normproj(x, gain, sc, sh, w, segs, seq, kvn=None, wt=None, name="normproj"):
    t, d = x.shape
    tm = ROW_TILE
    per_b = seq // tm
    n = w.shape[1]
    mod_spec = pl.BlockSpec((1, 1, d), lambda i: (i // per_b, 0, 0))
    in_specs = [pl.BlockSpec((tm, d), lambda i: (i, 0)),
                pl.BlockSpec((1, d), lambda i: (0, 0)), mod_spec, mod_spec,
                pl.BlockSpec((d, n), lambda i: (0, 0))]
    args = [x, gain.reshape(1, d), sc, sh, w]
    if wt is not None:
        in_specs.append(pl.BlockSpec((n, d), lambda i: (0, 0)))
        args.append(wt)
    if kvn is not None:
        in_specs.append(pl.BlockSpec((1, kvn.shape[-1]), lambda i: (0, 0)))
        args.append(kvn.reshape(1, -1))
    shapes, specs = _seg_out_shapes(t, segs)
    return pl.pallas_call(
        functools.partial(_normproj_kernel, segs=segs, has_kvn=kvn is not None, has_wt=wt is not None),
        grid=(t // tm,), in_specs=in_specs, out_specs=specs, out_shape=shapes,
        compiler_params=_params(("arbitrary",)), name=name,
    )(*args)


def _resproj(x, mixes, w_out, g, gain, sc, sh, w, segs, seq, name="resproj"):
    t, d = x.shape
    tm = ROW_TILE
    per_b = seq // tm
    n = w.shape[1]
    mod_spec = pl.BlockSpec((1, 1, d), lambda i: (i // per_b, 0, 0))
    in_specs = [pl.BlockSpec((tm, d), lambda i: (i, 0))]
    in_specs += [pl.BlockSpec((tm, m.shape[1]), lambda i: (i, 0)) for m in mixes]
    in_specs += [pl.BlockSpec(w_out.shape, lambda i: (0, 0)), mod_spec,
                 pl.BlockSpec((1, d), lambda i: (0, 0)), mod_spec, mod_spec,
                 pl.BlockSpec((d, n), lambda i: (0, 0))]
    shapes, specs = _seg_out_shapes(t, segs)
    shapes = [jax.ShapeDtypeStruct((t, d), F32), jax.ShapeDtypeStruct((t, d), BF16)] + shapes
    specs = [pl.BlockSpec((tm, d), lambda i: (i, 0)), pl.BlockSpec((tm, d), lambda i: (i, 0))] + specs
    return pl.pallas_call(
        functools.partial(_resproj_kernel, n_mix=len(mixes), segs=segs),
        grid=(t // tm,), in_specs=in_specs, out_specs=specs, out_shape=shapes,
        compiler_params=_params(("arbitrary",)), name=name,
    )(x, *mixes, w_out, g, gain.reshape(1, d), sc, sh, w)


def _bucket_table(n):
    max_exact = REL_BUCKETS // 2
    d = np.arange(n)
    df = np.maximum(d, 1).astype(np.float32)
    large = max_exact + (np.log(df / max_exact) / math.log(REL_MAX_DIST / max_exact)
                         * (REL_BUCKETS - max_exact)).astype(np.int32)
    large = np.minimum(large, REL_BUCKETS - 1)
    return np.where(d < max_exact, d, large).astype(np.int32)


def _bias_tiles_kernel(rb_ref, bk_ref, o_ref):
    h = pl.program_id(0)
    for t in range(2):
        bt = bk_ref[t]
        acc = jnp.zeros(bt.shape, F32)
        for b in range(REL_BUCKETS):
            acc = jnp.where(bt == b, rb_ref[b, h], acc)
        o_ref[0, t] = acc


def _bias_tiles(rel_bias):
    tb = ATT_BLOCK
    nh = rel_bias.shape[1]
    table = _bucket_table(2 * tb)
    s = np.arange(tb)[:, None]
    t = np.arange(tb)[None, :]
    bk = np.stack([table[np.maximum(t - s, 0)], table[tb + t - s]]).astype(np.int32)
    return pl.pallas_call(
        _bias_tiles_kernel,
        grid=(nh,),
        in_specs=[pl.BlockSpec(memory_space=pltpu.SMEM),
                  pl.BlockSpec((2, tb, tb), lambda h: (0, 0, 0))],
        out_specs=pl.BlockSpec((1, 2, tb, tb), lambda h: (h, 0, 0, 0)),
        out_shape=jax.ShapeDtypeStruct((nh, 2, tb, tb), F32),
        compiler_params=_params(("arbitrary",)), name="bias_tiles",
    )(rel_bias, jnp.asarray(bk))


def _flash_update(s, m, shift_const, acc_old, vaug):
    m_new = jnp.maximum(m, jnp.max(s, axis=0, keepdims=True) + shift_const)
    p = jnp.exp(s - (m_new - shift_const)).astype(BF16)
    acc = jnp.exp(m - m_new) * acc_old + jnp.dot(vaug, p, preferred_element_type=F32)
    return m_new, acc


def _split_bf16(x):
    pieces = []
    for _ in range(F_PIECES):
        p = x.astype(BF16)
        pieces.append(p)
        x = x - p.astype(F32)
    return pieces


def _fox_sel():
    selk = np.zeros((F_PIECES, LANES, C_HEADS * HEAD_DIM), np.float32)
    selq = np.zeros((F_PIECES, C_HEADS * HEAD_DIM, LANES), np.float32)
    onesk = np.zeros((1, C_HEADS * HEAD_DIM), np.float32)
    onesq = np.zeros((C_HEADS * HEAD_DIM, 1), np.float32)
    for h in range(C_HEADS):
        for p in range(F_PIECES):
            selk[p, h, h * HEAD_DIM + p] = -1.0
            selq[p, h * HEAD_DIM + F_PIECES + p, h] = 1.0
            onesk[0, h * HEAD_DIM + F_PIECES + p] = 1.0
            onesq[h * HEAD_DIM + p, 0] = 1.0
    return selk, selq, onesk, onesq


def _foxgate_kernel(f_ref, b_ref, selk_ref, selq_ref, onesk_ref, onesq_ref, kf_ref, qft_ref, *, tb):
    nblk = f_ref.shape[0] // tb
    row = lax.broadcasted_iota(I32, (tb, tb), 0)
    col = lax.broadcasted_iota(I32, (tb, tb), 1)
    tri = jnp.where(col <= row, 1.0, 0.0).astype(F32)

    def body(j, carry):
        z = f_ref[pl.ds(j * tb, tb), :] + b_ref[...]
        ls = -(jnp.maximum(-z, 0.0) + jnp.log(1.0 + jnp.exp(-jnp.abs(z))))
        cs = jnp.dot(tri, ls, precision=lax.Precision.HIGHEST, preferred_element_type=F32) + carry
        kf = onesk_ref[...]
        for p, piece in enumerate(_split_bf16(cs)):
            kf = kf + jnp.dot(piece, selk_ref[p], preferred_element_type=F32)
        kf_ref[pl.ds(j * tb, tb), :] = kf.astype(BF16)
        qf = onesq_ref[...]
        for p, piece in enumerate(_split_bf16(cs.T)):
            qf = qf + jnp.dot(selq_ref[p], piece, preferred_element_type=F32)
        qft_ref[j] = qf.astype(BF16)
        return cs[tb - 1:tb, :]

    lax.fori_loop(0, nblk, body, jnp.zeros((1, LANES), F32))


def _foxgate(f, b_forget, batch, seq):
    tb = ATT_BLOCK
    nblk = seq // tb
    n = C_HEADS * HEAD_DIM
    bpad = jnp.zeros((1, LANES), F32).at[0, :C_HEADS].set(b_forget)
    selk, selq, onesk, onesq = _fox_sel()
    return pl.pallas_call(
        functools.partial(_foxgate_kernel, tb=tb),
        grid=(batch,),
        in_specs=[pl.BlockSpec((seq, LANES), lambda b: (b, 0)),
                  pl.BlockSpec((1, LANES), lambda b: (0, 0)),
                  pl.BlockSpec(selk.shape, lambda b: (0, 0, 0)),
                  pl.BlockSpec(selq.shape, lambda b: (0, 0, 0)),
                  pl.BlockSpec(onesk.shape, lambda b: (0, 0)),
                  pl.BlockSpec(onesq.shape, lambda b: (0, 0))],
        out_specs=[pl.BlockSpec((seq, n), lambda b: (b, 0)),
                   pl.BlockSpec((nblk, n, tb), lambda b: (b, 0, 0))],
        out_shape=[jax.ShapeDtypeStruct((batch * seq, n), BF16),
                   jax.ShapeDtypeStruct((batch * nblk, n, tb), BF16)],
        compiler_params=_params(("arbitrary",)), name="fox_gates",
    )(f, bpad, jnp.asarray(selk, BF16), jnp.asarray(selq, BF16), jnp.asarray(onesk), jnp.asarray(onesq))


def _fox_kernel(qt_ref, qft_ref, k_ref, kf_ref, vt_ref, o_ref, kaug_ref, vaug_ref, *, tb, nq):
    qi = pl.program_id(2)
    hd = HEAD_DIM
    heads = range(FOX_GROUP)

    @pl.when(qi == 0)
    def _():
        ones_rows = _ones_row(hd, tb)
        for h in heads:
            kaug_ref[h, :, 0:hd] = k_ref[:, h * hd:(h + 1) * hd]
            kaug_ref[h, :, hd:2 * hd] = kf_ref[:, h * hd:(h + 1) * hd]

            def fill(j, c, h=h):
                vaug_ref[h, j, 0:hd, :] = vt_ref[j, h * hd:(h + 1) * hd, :]
                vaug_ref[h, j, hd:2 * hd, :] = ones_rows
                return c

            lax.fori_loop(0, nq, fill, 0)

    qa = [jnp.concatenate([qt_ref[0, h * hd:(h + 1) * hd, :], qft_ref[0, h * hd:(h + 1) * hd, :]], axis=0)
          for h in heads]
    keep = _key_le_query(tb)

    def step(j, carry, masked):
        scores = [jnp.dot(kaug_ref[h, pl.ds(j * tb, tb), :], qa[h], preferred_element_type=F32)
                  for h in heads]
        out = []
        for h in heads:
            m, acc = carry[h]
            s = jnp.where(keep, scores[h], NEG_INF) if masked else scores[h]
            out.append(_flash_update(s, m, 0.0, acc, vaug_ref[h, j]))
        return tuple(out)

    init = tuple((jnp.full((1, tb), NEG_INF, F32), jnp.zeros((2 * hd, tb), F32)) for _ in heads)
    carry = lax.fori_loop(0, qi, lambda j, c: step(j, c, False), init)
    carry = step(qi, carry, True)
    for pair in range(FOX_GROUP // 2):
        ot = jnp.concatenate([carry[h][1][0:hd] / carry[h][1][hd:hd + 1] for h in (2 * pair, 2 * pair + 1)], axis=0)
        o_ref[:, pair * 2 * hd:(pair + 1) * 2 * hd] = ot.T.astype(o_ref.dtype)


def _fox_attention(qt, qft, k, kf, vt, batch, seq):
    tb = ATT_BLOCK
    nq = seq // tb
    groups = C_HEADS // FOX_GROUP
    gw = FOX_GROUP * HEAD_DIM
    return pl.pallas_call(
        functools.partial(_fox_kernel, tb=tb, nq=nq),
        grid=(batch, groups, nq),
        in_specs=[pl.BlockSpec((1, gw, tb), lambda b, g, i: (b * nq + i, g, 0)),
                  pl.BlockSpec((1, gw, tb), lambda b, g, i: (b * nq + i, g, 0)),
                  pl.BlockSpec((seq, gw), lambda b, g, i: (b, g)),
                  pl.BlockSpec((seq, gw), lambda b, g, i: (b, g)),
                  pl.BlockSpec((nq, gw, tb), lambda b, g, i: (b, g, 0))],
        out_specs=pl.BlockSpec((tb, gw), lambda b, g, i: (b * nq + i, g)),
        out_shape=jax.ShapeDtypeStruct((batch * seq, C_HEADS * HEAD_DIM), BF16),
        scratch_shapes=[pltpu.VMEM((FOX_GROUP, seq, 2 * HEAD_DIM), BF16),
                        pltpu.VMEM((FOX_GROUP, nq, 2 * HEAD_DIM, tb), BF16)],
        compiler_params=_params(("arbitrary", "arbitrary", "arbitrary")), name="fox_attention",
    )(qt, qft, k, kf, vt)


def _diff_kernel(rb_ref, qt_ref, k_ref, vt_ref, tiles_ref, lam_ref, g_ref, o_ref,
                 ks_ref, vaug_ref, acc_ref, *, tb, nq, lam_init):
    g = pl.program_id(1)
    qi = pl.program_id(2)
    hd = HEAD_DIM
    dv = 2 * hd
    chains = [(hh, s_) for hh in range(DIFF_GROUP) for s_ in range(2)]

    @pl.when(qi == 0)
    def _():
        ones_rows = _ones_row(BF16_ROWS, tb)
        for c, (hh, s_) in enumerate(chains):
            ks_ref[c] = k_ref[:, hh * dv + s_ * hd:hh * dv + (s_ + 1) * hd]
        for hh in range(DIFF_GROUP):
            def fill(j, c, hh=hh):
                vaug_ref[hh, j, 0:dv, :] = vt_ref[j, hh * dv:(hh + 1) * dv, :]
                vaug_ref[hh, j, dv:dv + BF16_ROWS, :] = ones_rows
                return c

            lax.fori_loop(0, nq, fill, 0)

    qs = [qt_ref[0, hh * dv + s_ * hd:hh * dv + (s_ + 1) * hd, :] for (hh, s_) in chains]
    cfar = [rb_ref[REL_BUCKETS - 1, A_HEADS + g * DIFF_GROUP + hh] for hh in range(DIFF_GROUP)]
    keep = _key_le_query(tb)
    acc_ref[...] = jnp.zeros_like(acc_ref)

    def step(j, ms, kind):
        scores = [jnp.dot(ks_ref[c, pl.ds(j * tb, tb), :], qs[c], preferred_element_type=F32)
                  for c in range(len(chains))]
        out = []
        for c, (hh, s_) in enumerate(chains):
            if kind == "far":
                s, shift = scores[c], cfar[hh]
            else:
                s, shift = scores[c] + tiles_ref[hh, 0 if kind == "diag" else 1], 0.0
                if kind == "diag":
                    s = jnp.where(keep, s, NEG_INF)
            m_new, acc = _flash_update(s, ms[c], shift, acc_ref[c], vaug_ref[hh, j])
            acc_ref[c] = acc
            out.append(m_new)
        return tuple(out)

    ms = tuple(jnp.full((1, tb), NEG_INF, F32) for _ in chains)
    ms = lax.fori_loop(0, jnp.maximum(qi - 1, 0), lambda j, c: step(j, c, "far"), ms)
    ms = lax.cond(qi >= 1, lambda c: step(qi - 1, c, "near"), lambda c: c, ms)
    step(qi, ms, "diag")

    lv = lam_ref[...]
    lam = (jnp.exp(jnp.sum(lv[0:1] * lv[1:2], axis=-1, keepdims=True))
           - jnp.exp(jnp.sum(lv[2:3] * lv[3:4], axis=-1, keepdims=True)) + lam_init)
    for hh in range(DIFF_GROUP):
        a1 = acc_ref[2 * hh]
        a2 = acc_ref[2 * hh + 1]
        ot = a1[0:dv] / a1[dv:dv + 1] - lam * (a2[0:dv] / a2[dv:dv + 1])
        o = _rms(ot.T, g_ref[...]) * (1.0 - lam_init)
        o_ref[:, hh * dv:(hh + 1) * dv] = o.astype(o_ref.dtype)


def _diff_attention(rel_bias, qt, k, vt, tiles, lam_vec, subln, batch, seq, lam_init):
    tb = ATT_BLOCK
    nq = seq // tb
    groups = B_HEADS // DIFF_GROUP
    gw = DIFF_GROUP * 2 * HEAD_DIM
    dv = 2 * HEAD_DIM
    return pl.pallas_call(
        functools.partial(_diff_kernel, tb=tb, nq=nq, lam_init=lam_init),
        grid=(batch, groups, nq),
        in_specs=[pl.BlockSpec(memory_space=pltpu.SMEM),
                  pl.BlockSpec((1, gw, tb), lambda b, g, i: (b * nq + i, g, 0)),
                  pl.BlockSpec((seq, gw), lambda b, g, i: (b, g)),
                  pl.BlockSpec((nq, gw, tb), lambda b, g, i: (b, g, 0)),
                  pl.BlockSpec((DIFF_GROUP, 2, tb, tb), lambda b, g, i: (A_HEADS // DIFF_GROUP + g, 0, 0, 0)),
                  pl.BlockSpec((4, HEAD_DIM), lambda b, g, i: (0, 0)),
                  pl.BlockSpec((1, dv), lambda b, g, i: (0, 0))],
        out_specs=pl.BlockSpec((tb, gw), lambda b, g, i: (b * nq + i, g)),
        out_shape=jax.ShapeDtypeStruct((batch * seq, B_HEADS * dv), BF16),
        scratch_shapes=[pltpu.VMEM((2 * DIFF_GROUP, seq, HEAD_DIM), BF16),
                        pltpu.VMEM((DIFF_GROUP, nq, dv + BF16_ROWS, tb), BF16),
                        pltpu.VMEM((2 * DIFF_GROUP, dv + BF16_ROWS, tb), F32)],
        compiler_params=_params(("arbitrary", "arbitrary", "arbitrary")), name="diff_attention",
    )(rel_bias, qt, k, vt, tiles, lam_vec, subln.reshape(1, -1))


def _dsa_kernel(rb_ref, qat_ref, qit_ref, wit_ref, kw_ref, ckv_ref, ckvt_ref, wuk_ref, wuvt_ref, tiles_ref,
                o_ref, keys_ref, selb_ref, qlat_ref, vaug_ref, acc_ref, *, tb, topk, nq):
    qb = pl.program_id(1)
    nblk = qb + 1
    hd = HEAD_DIM
    keep = _key_le_query(tb)

    @pl.when(qb == 0)
    def _():
        ones_rows = _ones_row(BF16_ROWS, tb)

        def fill(j, c):
            vaug_ref[j, 0:A_LATENT, :] = ckvt_ref[j]
            vaug_ref[j, A_LATENT:A_LATENT + BF16_ROWS, :] = ones_rows
            return c

        lax.fori_loop(0, nq, fill, 0)

    for h in range(A_HEADS):
        ql = jnp.dot(wuk_ref[h], qat_ref[0, h * hd:(h + 1) * hd, :], preferred_element_type=F32)
        qlat_ref[h] = ql.astype(BF16)

    wrows = [wit_ref[0, IDX_DIM + h:IDX_DIM + h + 1, :] * (IDX_HEADS ** -0.5) for h in range(IDX_HEADS)]
    zpad = jnp.zeros((LANES - IDX_DIM, tb), BF16)
    qi_pad = [jnp.concatenate([qit_ref[0, h * IDX_DIM:(h + 1) * IDX_DIM, :], zpad], axis=0)
              for h in range(IDX_HEADS)]

    def index_keys(j, masked):
        kb = kw_ref[pl.ds(j * tb, tb), :]
        isc = jnp.zeros((tb, tb), F32)
        for h in range(IDX_HEADS):
            li = jnp.dot(kb, qi_pad[h], preferred_element_type=F32)
            isc = isc + jnp.maximum(li, 0.0) * wrows[h]
        isc = jnp.where(isc == 0.0, 0.0, isc)
        bits = pltpu.bitcast(isc, I32)
        key = bits ^ ((bits >> 31) & 0x7FFFFFFF)
        if masked:
            key = jnp.where(keep, key, INT_MIN)
        keys_ref[j] = key

    def p1(j, c):
        index_keys(j, False)
        return c

    lax.fori_loop(0, qb, p1, 0)
    index_keys(qb, True)

    def count(pred):
        def body(j, acc):
            ind = jnp.where(pred(keys_ref[j]), 1.0, 0.0)
            return acc + jnp.sum(ind.reshape(tb // 8, 8, tb), axis=0)
        acc = lax.fori_loop(0, nblk, body, jnp.zeros((8, tb), F32))
        return jnp.sum(acc, axis=0, keepdims=True)

    kth = jnp.where(count(lambda k: k >= 0) >= topk, 0, INT_MIN).astype(I32)

    def bs(i, kth):
        cand = kth | lax.shift_left(jnp.int32(1), 30 - i)
        return jnp.where(count(lambda k: k >= cand) >= topk, cand, kth)

    kth = lax.fori_loop(0, 31, bs, kth)
    need = topk - count(lambda k: k > kth)

    lower = jnp.where(lax.broadcasted_iota(I32, (tb, tb), 1) <= lax.broadcasted_iota(I32, (tb, tb), 0),
                      1.0, 0.0).astype(BF16)

    def mask_block(j, seen, masked):
        key = keys_ref[j]
        eq = key == kth
        pre = jnp.dot(lower, jnp.where(eq, 1.0, 0.0).astype(BF16), preferred_element_type=F32)
        sel = (key > kth) | (eq & (pre + seen <= need))
        if masked:
            sel = sel & keep
        selb_ref[j] = jnp.where(sel, 0.0, NEG_INF)
        return seen + pre[tb - 1:tb, :]

    seen = lax.fori_loop(0, qb, lambda j, s: mask_block(j, s, False), jnp.zeros((1, tb), F32))
    mask_block(qb, seen, True)

    cfar = [rb_ref[REL_BUCKETS - 1, h] for h in range(A_HEADS)]
    acc_ref[...] = jnp.zeros_like(acc_ref)

    def step(j, ms, kind):
        kvb = ckv_ref[pl.ds(j * tb, tb), :]
        scores = [jnp.dot(kvb, qlat_ref[h], preferred_element_type=F32) for h in range(A_HEADS)]
        sb = selb_ref[j]
        out = []
        for h in range(A_HEADS):
            if kind == "far":
                s, shift = scores[h] + sb, cfar[h]
            else:
                s, shift = scores[h] + (tiles_ref[h, 0 if kind == "diag" else 1] + sb), 0.0
            m_new, acc = _flash_update(s, ms[h], shift, acc_ref[h], vaug_ref[j])
            acc_ref[h] = acc
            out.append(m_new)
        return tuple(out)

    ms = tuple(jnp.full((1, tb), NEG_INF, F32) for _ in range(A_HEADS))
    ms = lax.fori_loop(0, jnp.maximum(qb - 1, 0), lambda j, c: step(j, c, "far"), ms)
    ms = lax.cond(qb >= 1, lambda c: step(qb - 1, c, "near"), lambda c: c, ms)
    step(qb, ms, "diag")

    outs = []
    for h in range(A_HEADS):
        a = acc_ref[h]
        o_lat = (a[0:A_LATENT] / a[A_LATENT:A_LATENT + 1]).astype(BF16)
        outs.append(jnp.dot(wuvt_ref[h], o_lat, preferred_element_type=F32))
    o_ref[...] = jnp.concatenate(outs, axis=0).T.astype(o_ref.dtype)


def _dsa_attention(rel_bias, qat, qit, wit, kw, ckv, ckvt, wuk, wuvt, tiles, batch, seq):
    tb = ATT_BLOCK
    nq = seq // tb
    topk = min(TOPK_MAX, seq // 4)
    nqa = A_HEADS * HEAD_DIM
    nqi = IDX_HEADS * IDX_DIM
    aug = A_LATENT + BF16_ROWS
    return pl.pallas_call(
        functools.partial(_dsa_kernel, tb=tb, topk=topk, nq=nq),
        grid=(batch, nq),
        in_specs=[pl.BlockSpec(memory_space=pltpu.SMEM),
                  pl.BlockSpec((1, nqa, tb), lambda b, i: (b * nq + i, 0, 0)),
                  pl.BlockSpec((1, nqi, tb), lambda b, i: (b * nq + i, 0, 0)),
                  pl.BlockSpec((1, LANES, tb), lambda b, i: (b * nq + i, 0, 0)),
                  pl.BlockSpec((seq, LANES), lambda b, i: (b, 0)),
                  pl.BlockSpec((seq, A_LATENT), lambda b, i: (b, 0)),
                  pl.BlockSpec((nq, A_LATENT, tb), lambda b, i: (b, 0, 0)),
                  pl.BlockSpec(wuk.shape, lambda b, i: (0, 0, 0)),
                  pl.BlockSpec(wuvt.shape, lambda b, i: (0, 0, 0)),
                  pl.BlockSpec((A_HEADS, 2, tb, tb), lambda b, i: (0, 0, 0, 0))],
        out_specs=pl.BlockSpec((tb, nqa), lambda b, i: (b * nq + i, 0)),
        out_shape=jax.ShapeDtypeStruct((batch * seq, nqa), BF16),
        scratch_shapes=[pltpu.VMEM((nq, tb, tb), I32), pltpu.VMEM((nq, tb, tb), F32),
                        pltpu.VMEM((A_HEADS, A_LATENT, tb), BF16),
                        pltpu.VMEM((nq, aug, tb), BF16),
                        pltpu.VMEM((A_HEADS, aug, tb), F32)],
        compiler_params=_params(("arbitrary", "arbitrary")), name="dsa_attention",
    )(rel_bias, qat, qit, wit, kw, ckv, ckvt, wuk, wuvt, tiles)


RANK_BASE = 1e30
RANK_STEP = 1e28


def _rank_mark(r):
    return -(RANK_BASE + r * RANK_STEP)


def _top_sorted(x, k):
    rows = lax.broadcasted_iota(I32, (k, x.shape[1]), 0)
    out = jnp.zeros((k, x.shape[1]), F32)
    for r in range(k):
        m = jnp.max(x, axis=0, keepdims=True)
        out = jnp.where(rows == r, m, out)
        x = jnp.where(x == m, _rank_mark(r), x)
    return out, x


def _router_kernel(q_ref, keys_ref, c1_ref, a_ref, r2_ref, w_ref, *, tr):
    k = PEER_TOPK
    nk = keys_ref.shape[1]
    row8 = lax.broadcasted_iota(I32, (8, LANES), 0)
    for h in range(PEER_HEADS):
        for tc in range(tr // LANES):
            tok = slice(tc * LANES, (tc + 1) * LANES)
            qh = q_ref[tok, :]
            s1 = lax.dot_general(keys_ref[0], qh[:, (2 * h) * nk:(2 * h + 1) * nk], _NT,
                                 preferred_element_type=F32)
            s2 = lax.dot_general(keys_ref[1], qh[:, (2 * h + 1) * nk:(2 * h + 2) * nk], _NT,
                                 preferred_element_type=F32)
            a, marked1 = _top_sorted(s1, k)
            b, marked2 = _top_sorted(s2, k)
            b8 = b[0:8]
            parts = [a[0:1] + b, a[1:2] + b8]
            for i, lim in ((2, 5), (3, 4), (4, 3), (5, 2), (6, 2), (7, 2)):
                parts.append(jnp.where(row8 < lim, a[i:i + 1] + b8, -jnp.inf))
            parts.append(a[8:16] + b[0:1])
            cand = jnp.concatenate(parts, axis=0)
            x = cand
            thr = None
            for _ in range(k):
                thr = jnp.max(x, axis=0, keepdims=True)
                x = jnp.where(x == thr, -jnp.inf, x)
            mx = a[0:1] + b[0:1]
            z = jnp.sum(jnp.where(cand >= thr, jnp.exp(cand - mx), 0.0), axis=0, keepdims=True)
            c1 = jnp.zeros((nk, LANES), F32)
            for r in range(k):
                cnt = jnp.sum(jnp.where(a[r:r + 1] + b >= thr, 1.0, 0.0), axis=0, keepdims=True)
                c1 = jnp.where(marked1 == _rank_mark(r), cnt, c1)
            rank2 = jnp.where(marked2 <= -RANK_BASE,
                              jnp.floor((-marked2 - RANK_BASE) * (1.0 / RANK_STEP) + 0.5), 127.0)
            c1_ref[h, :, tok] = c1
            a_ref[h, :, tok] = jnp.exp(s1 - a[0:1]) / z
            r2_ref[h, :, tok] = rank2.astype(BF16)
            w_ref[h, :, tok] = jnp.exp(s2 - b[0:1]).astype(BF16)


def _router(q, sub_keys):
    t = q.shape[0]
    tr = ROUTER_TILE
    nk = sub_keys.shape[1]
    shp32 = jax.ShapeDtypeStruct((PEER_HEADS, nk, t), F32)
    shp16 = jax.ShapeDtypeStruct((PEER_HEADS, nk, t), BF16)
    spec = pl.BlockSpec((PEER_HEADS, nk, tr), lambda i: (0, 0, i))
    return pl.pallas_call(
        functools.partial(_router_kernel, tr=tr),
        grid=(t // tr,),
        in_specs=[pl.BlockSpec((tr, q.shape[1]), lambda i: (i, 0)),
                  pl.BlockSpec(sub_keys.shape, lambda i: (0, 0, 0))],
        out_specs=[spec] * 4, out_shape=[shp32, shp32, shp16, shp16],
        compiler_params=_params(("arbitrary",)), name="peer_router",
    )(q, sub_keys)


def _peer_kernel(h_ref, u_ref, vt_ref, c1_ref, a_ref, r2_ref, w_ref, x_ref, g_ref, *rest,
                 tm, te, nk, ne, final):
    if final:
        gf_ref, o_ref, act0_ref, act1_ref, p_ref, acc_ref, r2s_ref, ws_ref = rest
    else:
        o_ref, act0_ref, act1_ref, p_ref, acc_ref, r2s_ref, ws_ref = rest
    s = pl.program_id(0)
    e0 = jnp.maximum(s - 1, 0) % ne
    nsub = nk // BF16_ROWS

    @pl.when(s == 0)
    def _():
        act1_ref[...] = jnp.zeros_like(act1_ref)

    @pl.when(e0 == 0)
    def _():
        acc_ref[...] = jnp.zeros_like(acc_ref)
        r2s_ref[...] = r2_ref[...]
        ws_ref[...] = w_ref[...]

    def main(cur_ref, prev_ref):
        tile = 2 * LANES

        def gate_piece(ii_list, tc_list):
            for tc in tc_list:
                tok = slice(tc * LANES, (tc + 1) * LANES)
                gates = [[None] * nsub for _ in ii_list]
                for h in range(PEER_HEADS):
                    r2 = [r2s_ref[h, k * BF16_ROWS:(k + 1) * BF16_ROWS, tok] for k in range(nsub)]
                    w2 = [ws_ref[h, k * BF16_ROWS:(k + 1) * BF16_ROWS, tok] for k in range(nsub)]
                    for n, ii in enumerate(ii_list):
                        c = jnp.broadcast_to(c1_ref[h, ii:ii + 1, tok], (BF16_ROWS, LANES)).astype(BF16)
                        a = jnp.broadcast_to(a_ref[h, ii:ii + 1, tok], (BF16_ROWS, LANES)).astype(BF16)
                        for k in range(nsub):
                            t = jnp.where(r2[k] < c, w2[k] * a, 0)
                            gates[n][k] = t if gates[n][k] is None else gates[n][k] + t
                for n, ii in enumerate(ii_list):
                    for k in range(nsub):
                        rows = slice(ii * nk + k * BF16_ROWS, ii * nk + (k + 1) * BF16_ROWS)
                        p_ref[rows, tok] = gates[n][k] * jax.nn.gelu(prev_ref[rows, tok])

        ii_per = tile // nk
        tc_per = tile // LANES
        for nt in range(tm // tile):
            cols = slice(nt * tile, (nt + 1) * tile)
            for kt in range(te // tile):
                rows = slice(kt * tile, (kt + 1) * tile)
                gate_piece(list(range(kt * ii_per, (kt + 1) * ii_per)),
                           list(range(nt * tc_per, (nt + 1) * tc_per)))
                acc_ref[:, cols] += jnp.dot(vt_ref[:, rows], p_ref[rows, cols], preferred_element_type=F32)
                if kt % 2 == 1:
                    urows = slice((kt // 2) * (te // 2), (kt // 2 + 1) * (te // 2))
                    cur_ref[urows, cols] = lax.dot_general(u_ref[urows, :], h_ref[cols, :], _NT,
                                                           preferred_element_type=F32).astype(BF16)

    @pl.when(s % 2 == 0)
    def _():
        main(act0_ref, act1_ref)

    @pl.when(s % 2 == 1)
    def _():
        main(act1_ref, act0_ref)

    @pl.when((e0 == ne - 1) & (s > 0))
    def _():
        xn = x_ref[...] + g_ref[0] * acc_ref[...].T
        if final:
            xn = _rms(xn, gf_ref[...])
        o_ref[...] = xn


def _peer(h, u, vt, c1, a, r2, w, x, g2, seq, final_gain=None):
    t, d = x.shape
    tm, te = PEER_TOKEN_TILE, PEER_EXPERT_TILE
    ne = u.shape[0] // te
    nk = r2.shape[1]
    rows_per = te // nk
    per_b = seq // tm
    final = final_gain is not None
    steps = (t // tm) * ne + 1

    def cur(s):
        s1 = jnp.minimum(s, steps - 2)
        return s1 // ne, s1 % ne

    def prev(s):
        s0 = jnp.maximum(s - 1, 0)
        return s0 // ne, s0 % ne

    in_specs = [pl.BlockSpec((tm, d), lambda s: (cur(s)[0], 0)),
                pl.BlockSpec((te, d), lambda s: (cur(s)[1], 0)),
                pl.BlockSpec((d, te), lambda s: (0, prev(s)[1])),
                pl.BlockSpec((PEER_HEADS, rows_per, tm), lambda s: (0, prev(s)[1], prev(s)[0])),
                pl.BlockSpec((PEER_HEADS, rows_per, tm), lambda s: (0, prev(s)[1], prev(s)[0])),
                pl.BlockSpec((PEER_HEADS, nk, tm), lambda s: (0, 0, prev(s)[0])),
                pl.BlockSpec((PEER_HEADS, nk, tm), lambda s: (0, 0, prev(s)[0])),
                pl.BlockSpec((tm, d), lambda s: (prev(s)[0], 0)),
                pl.BlockSpec((1, 1, d), lambda s: (prev(s)[0] // per_b, 0, 0))]
    args = [h, u, vt, c1, a, r2, w, x, g2]
    if final:
        in_specs.append(pl.BlockSpec((1, d), lambda s: (0, 0)))
        args.append(final_gain.reshape(1, d))
    return pl.pallas_call(
        functools.partial(_peer_kernel, tm=tm, te=te, nk=nk, ne=ne, final=final),
        grid=(steps,),
        in_specs=in_specs,
        out_specs=pl.BlockSpec((tm, d), lambda s: (prev(s)[0], 0)),
        out_shape=jax.ShapeDtypeStruct((t, d), F32),
        scratch_shapes=[pltpu.VMEM((te, tm), BF16), pltpu.VMEM((te, tm), BF16),
                        pltpu.VMEM((te, tm), BF16), pltpu.VMEM((d, tm), F32),
                        pltpu.VMEM((PEER_HEADS, nk, tm), BF16), pltpu.VMEM((PEER_HEADS, nk, tm), BF16)],
        compiler_params=_params(("arbitrary",)), name="peer_experts",
    )(*args)


def _pad_cols(w, n):
    return jnp.pad(w, ((0, 0), (0, n - w.shape[1])))


def _even_w_in(w):
    na, ni, nb = A_HEADS * HEAD_DIM, IDX_HEADS * IDX_DIM, B_HEADS * 2 * HEAD_DIM
    o = np.cumsum([0, na, A_LATENT, ni, IDX_DIM, IDX_HEADS, nb, nb, nb])
    kw = _pad_cols(w[:, o[3]:o[5]], LANES)
    return jnp.concatenate([w[:, o[0]:o[3]], kw, w[:, o[5]:o[8]]], axis=1).astype(BF16)


def kernel(x, c, rel_bias, ada_w, ada_b, norm_mix, norm_ffn, norm_final, even_w_in, even_w_out,
           a_kv_norm, a_w_uk, a_w_uv, b_lambda, b_subln, odd_w_in, odd_b_forget, odd_w_out,
           peer_w_q, peer_sub_keys, peer_u, peer_v):
    batch, seq, d = x.shape
    depth = ada_w.shape[0]
    t = batch * seq
    assert seq % ROW_TILE == 0 and seq % ATT_BLOCK == 0 and t % PEER_TOKEN_TILE == 0

    mod = _ada(c, ada_w, ada_b)
    tiles = _bias_tiles(rel_bias)

    na, ni, nb = A_HEADS * HEAD_DIM, IDX_HEADS * IDX_DIM, B_HEADS * 2 * HEAD_DIM
    nc = C_HEADS * HEAD_DIM
    o = np.cumsum([0, na, A_LATENT, ni, LANES, nb, nb, nb])
    even_segs = [(o[0], o[1], "Tq", [BF16]), (o[1], o[2], "kvnorm", [BF16]), (o[2], o[3], "Tq", [BF16]),
                 (o[3], o[4], None, [BF16]), (o[3], o[4], "T", [F32]), (o[4], o[5], "Tq", [BF16]),
                 (o[5], o[6], None, [BF16]), (o[6], o[7], "T", [BF16])]
    odd_segs = [(0, nc, "Tq", [BF16]), (nc, 2 * nc, None, [BF16]), (2 * nc, 3 * nc, "T", [BF16]),
                (3 * nc, 3 * nc + LANES, None, [F32])]
    nq_peer = peer_w_q.shape[2]
    peer_segs = [(0, nq_peer, None, [BF16])]

    xt = x.reshape(t, d)
    for layer in range(depth):
        m6 = mod[layer].reshape(batch, 6, 1, d)
        sh1, sc1, g1, sh2, sc2, g2 = [m6[:, i] for i in range(6)]
        if layer % 2 == 0:
            e = layer // 2
            lam_init = 0.8 - 0.6 * math.exp(-0.3 * layer)
            w_in = _even_w_in(even_w_in[e])
            qat, ckv, ckvt, qit, kw, wit, qbt, k_b, vbt = _normproj(
                xt, norm_mix[layer], sc1, sh1, w_in, even_segs, seq, kvn=a_kv_norm[e], wt=w_in.T,
                name="even_in_proj")
            wuk = jnp.transpose(a_w_uk[e], (1, 0, 2)).astype(BF16)
            wuvt = jnp.transpose(a_w_uv[e], (1, 2, 0)).astype(BF16)
            o_a = _dsa_attention(rel_bias, qat, qit, wit, kw, ckv, ckvt, wuk, wuvt, tiles, batch, seq)
            o_b = _diff_attention(rel_bias, qbt, k_b, vbt, tiles, b_lambda[e], b_subln[e], batch, seq, lam_init)
            mixes = [o_a, o_b]
            w_out = even_w_out[e].astype(BF16)
        else:
            od = layer // 2
            w_in = _pad_cols(odd_w_in[od], 3 * nc + LANES).astype(BF16)
            qt, k, vt, f = _normproj(xt, norm_mix[layer], sc1, sh1, w_in, odd_segs, seq, wt=w_in.T,
                                     name="odd_in_proj")
            kf, qft = _foxgate(f, odd_b_forget[od], batch, seq)
            mixes = [_fox_attention(qt, qft, k, kf, vt, batch, seq)]
            w_out = odd_w_out[od].astype(BF16)
        xt, h2, qp = _resproj(xt, mixes, w_out, g1, norm_ffn[layer], sc2, sh2,
                              peer_w_q[layer].astype(BF16), peer_segs, seq)
        c1, a, r2, w = _router(qp, peer_sub_keys[layer].astype(BF16))
        xt = _peer(h2, peer_u[layer].astype(BF16), peer_v[layer].T.astype(BF16), c1, a, r2, w,
                   xt, g2, seq, final_gain=norm_final if layer == depth - 1 else None)
    return xt.reshape(batch, seq, d)
```

```python
import functools
import math

import numpy as np
import jax
import jax.numpy as jnp
from jax import lax
from jax.experimental import pallas as pl
from jax.experimental.pallas import tpu as pltpu

F32 = jnp.float32
BF16 = jnp.bfloat16
I32 = jnp.int32

HEAD_DIM = 64
RMS_EPS = 1e-6
NEG_INF = -1e30
A_HEADS = 8
A_LATENT = 256
IDX_HEADS = 8
IDX_DIM = 64
TOPK_MAX = 256
B_HEADS = 4
C_HEADS = 16
REL_BUCKETS = 32
REL_MAX_DIST = 128
PEER_HEADS = 8
PEER_TOPK = 16

LANES = 128
BF16_ROWS = 16
ATT_BLOCK = 256
ROW_TILE = 512
PEER_TOKEN_TILE = 512
PEER_EXPERT_TILE = 2048
ROUTER_TILE = 512
FOX_GROUP = 8
DIFF_GROUP = 4
F_PIECES = 3
VMEM_LIMIT = 56 * 1024 * 1024
INT_MIN = -2 ** 31

_NT = (((1,), (1,)), ((), ()))


def _params(sem):
    return pltpu.CompilerParams(dimension_semantics=sem, vmem_limit_bytes=VMEM_LIMIT)


def _rms(x, g):
    return x * lax.rsqrt(jnp.mean(x * x, axis=-1, keepdims=True) + RMS_EPS) * g


def _ones_row(rows, cols):
    return jnp.where(lax.broadcasted_iota(I32, (rows, cols), 0) == 0, 1.0, 0.0).astype(BF16)


def _key_le_query(tb):
    return lax.broadcasted_iota(I32, (tb, tb), 0) <= lax.broadcasted_iota(I32, (tb, tb), 1)


def _ada_kernel(c_ref, w_ref, b_ref, o_ref):
    c = c_ref[...]
    ca = c * jax.nn.sigmoid(c)
    o_ref[0] = jnp.dot(ca, w_ref[0], precision=lax.Precision.HIGHEST,
                       preferred_element_type=F32) + b_ref[0]


def _ada(c, ada_w, ada_b):
    depth, d, n = ada_w.shape
    b = c.shape[0]
    bp = 8
    cp = jnp.zeros((bp, d), F32).at[:b].set(c)
    tn = 1536
    out = pl.pallas_call(
        _ada_kernel,
        grid=(depth, n // tn),
        in_specs=[pl.BlockSpec((bp, d), lambda l, j: (0, 0)),
                  pl.BlockSpec((1, d, tn), lambda l, j: (l, 0, j)),
                  pl.BlockSpec((1, 1, tn), lambda l, j: (l, 0, j))],
        out_specs=pl.BlockSpec((1, bp, tn), lambda l, j: (l, 0, j)),
        out_shape=jax.ShapeDtypeStruct((depth, bp, n), F32),
        compiler_params=_params(("arbitrary", "arbitrary")),
        name="ada_mod",
    )(cp, ada_w, ada_b.reshape(depth, 1, n))
    return out[:, :b]


def _store_transposed(out_ref, rt, dt):
    for c in range(rt.shape[1] // ATT_BLOCK):
        out_ref[c] = rt[:, c * ATT_BLOCK:(c + 1) * ATT_BLOCK].astype(dt)


def _emit_segments(hb, w_ref, wt_ref, segs, outs, kvn_ref):
    k = 0
    for (a, b, kind, dtypes) in segs:
        if kind in ("T", "Tq"):
            r = lax.dot_general(wt_ref[a:b, :], hb, _NT, preferred_element_type=F32)
            if kind == "Tq":
                r = r * (HEAD_DIM ** -0.5)
            for dt in dtypes:
                _store_transposed(outs[k], r, dt)
                k += 1
            continue
        r = jnp.dot(hb, w_ref[:, a:b], preferred_element_type=F32)
        if kind == "kvnorm":
            r = _rms(r, kvn_ref[...])
            outs[k][...] = r.astype(dtypes[0])
            _store_transposed(outs[k + 1], r.T, dtypes[0])
            k += 2
            continue
        for dt in dtypes:
            outs[k][...] = r.astype(dt)
            k += 1


def _normproj_kernel(*refs, segs, has_kvn, has_wt):
    x_ref, gain_ref, sc_ref, sh_ref, w_ref = refs[:5]
    pos = 5
    wt_ref = kvn_ref = None
    if has_wt:
        wt_ref = refs[pos]
        pos += 1
    if has_kvn:
        kvn_ref = refs[pos]
        pos += 1
    outs = list(refs[pos:])
    h = _rms(x_ref[...], gain_ref[...]) * (1.0 + sc_ref[0]) + sh_ref[0]
    _emit_segments(h.astype(BF16), w_ref, wt_ref, segs, outs, kvn_ref)


def _resproj_kernel(*refs, n_mix, segs):
    x_ref = refs[0]
    mix_refs = refs[1:1 + n_mix]
    wo_ref, g_ref, gain_ref, sc_ref, sh_ref, w_ref = refs[1 + n_mix:7 + n_mix]
    outs = list(refs[7 + n_mix:])
    y = None
    off = 0
    for m in mix_refs:
        kdim = m.shape[1]
        t = jnp.dot(m[...], wo_ref[off:off + kdim, :], preferred_element_type=F32)
        y = t if y is None else y + t
        off += kdim
    xn = x_ref[...] + g_ref[0] * y
    outs[0][...] = xn
    h = _rms(xn, gain_ref[...]) * (1.0 + sc_ref[0]) + sh_ref[0]
    hb = h.astype(BF16)
    outs[1][...] = hb
    _emit_segments(hb, w_ref, None, segs, outs[2:], None)


def _seg_out_shapes(t, segs):
    shapes, specs = [], []
    per = ROW_TILE // ATT_BLOCK

    def plain(n, dt):
        shapes.append(jax.ShapeDtypeStruct((t, n), dt))
        specs.append(pl.BlockSpec((ROW_TILE, n), lambda i: (i, 0)))

    def transposed(n, dt):
        shapes.append(jax.ShapeDtypeStruct((t // ATT_BLOCK, n, ATT_BLOCK), dt))
        specs.append(pl.BlockSpec((per, n, ATT_BLOCK), lambda i: (i, 0, 0)))

    for (a, b, kind, dtypes) in segs:
        if kind == "kvnorm":
            plain(b - a, dtypes[0])
            transposed(b - a, dtypes[0])
            continue
        for dt in dtypes:
            (transposed if kind in ("T", "Tq") else plain)(b - a, dt)
    return shapes, specs


def _normproj(x, gain, sc, sh, w, segs, seq, kvn=None, wt=None, name="normproj"):
    t, d = x.shape
    tm = ROW_TILE
    per_b = seq // tm
    n = w.shape[1]
    mod_spec = pl.BlockSpec((1, 1, d), lambda i: (i // per_b, 0, 0))
    in_specs = [pl.BlockSpec((tm, d), lambda i: (i, 0)),
                pl.BlockSpec((1, d), lambda i: (0, 0)), mod_spec, mod_spec,
                pl.BlockSpec((d, n), lambda i: (0, 0))]
    args = [x, gain.reshape(1, d), sc, sh, w]
    if wt is not None:
        in_specs.append(pl.BlockSpec((n, d), lambda i: (0, 0)))
        args.append(wt)
    if kvn is not None:
        in_specs.append(pl.BlockSpec((1, kvn.shape[-1]), lambda i: (0, 0)))
        args.append(kvn.reshape(1, -1))
    shapes, specs = _seg_out_shapes(t, segs)
    return pl.pallas_call(
        functools.partial(_normproj_kernel, segs=segs, has_kvn=kvn is not None, has_wt=wt is not None),
        grid=(t // tm,), in_specs=in_specs, out_specs=specs, out_shape=shapes,
        compiler_params=_params(("arbitrary",)), name=name,
    )(*args)


def _resproj(x, mixes, w_out, g, gain, sc, sh, w, segs, seq, name="resproj"):
    t, d = x.shape
    tm = ROW_TILE
    per_b = seq // tm
    n = w.shape[1]
    mod_spec = pl.BlockSpec((1, 1, d), lambda i: (i // per_b, 0, 0))
    in_specs = [pl.BlockSpec((tm, d), lambda i: (i, 0))]
    in_specs += [pl.BlockSpec((tm, m.shape[1]), lambda i: (i, 0)) for m in mixes]
    in_specs += [pl.BlockSpec(w_out.shape, lambda i: (0, 0)), mod_spec,
                 pl.BlockSpec((1, d), lambda i: (0, 0)), mod_spec, mod_spec,
                 pl.BlockSpec((d, n), lambda i: (0, 0))]
    shapes, specs = _seg_out_shapes(t, segs)
    shapes = [jax.ShapeDtypeStruct((t, d), F32), jax.ShapeDtypeStruct((t, d), BF16)] + shapes
    specs = [pl.BlockSpec((tm, d), lambda i: (i, 0)), pl.BlockSpec((tm, d), lambda i: (i, 0))] + specs
    return pl.pallas_call(
        functools.partial(_resproj_kernel, n_mix=len(mixes), segs=segs),
        grid=(t // tm,), in_specs=in_specs, out_specs=specs, out_shape=shapes,
        compiler_params=_params(("arbitrary",)), name=name,
    )(x, *mixes, w_out, g, gain.reshape(1, d), sc, sh, w)


def _bucket_table(n):
    max_exact = REL_BUCKETS // 2
    d = np.arange(n)
    df = np.maximum(d, 1).astype(np.float32)
    large = max_exact + (np.log(df / max_exact) / math.log(REL_MAX_DIST / max_exact)
                         * (REL_BUCKETS - max_exact)).astype(np.int32)
    large = np.minimum(large, REL_BUCKETS - 1)
    return np.where(d < max_exact, d, large).astype(np.int32)


def _bias_tiles_kernel(rb_ref, bk_ref, o_ref):
    h = pl.program_id(0)
    for t in range(2):
        bt = bk_ref[t]
        acc = jnp.zeros(bt.shape, F32)
        for b in range(REL_BUCKETS):
            acc = jnp.where(bt == b, rb_ref[b, h], acc)
        o_ref[0, t] = acc


def _bias_tiles(rel_bias):
    tb = ATT_BLOCK
    nh = rel_bias.shape[1]
    table = _bucket_table(2 * tb)
    s = np.arange(tb)[:, None]
    t = np.arange(tb)[None, :]
    bk = np.stack([table[np.maximum(t - s, 0)], table[tb + t - s]]).astype(np.int32)
    return pl.pallas_call(
        _bias_tiles_kernel,
        grid=(nh,),
        in_specs=[pl.BlockSpec(memory_space=pltpu.SMEM),
                  pl.BlockSpec((2, tb, tb), lambda h: (0, 0, 0))],
        out_specs=pl.BlockSpec((1, 2, tb, tb), lambda h: (h, 0, 0, 0)),
        out_shape=jax.ShapeDtypeStruct((nh, 2, tb, tb), F32),
        compiler_params=_params(("arbitrary",)), name="bias_tiles",
    )(rel_bias, jnp.asarray(bk))


def _flash_update(s, m, shift_const, acc_old, vaug):
    m_new = jnp.maximum(m, jnp.max(s, axis=0, keepdims=True) + shift_const)
    p = jnp.exp(s - (m_new - shift_const)).astype(BF16)
    acc = jnp.exp(m - m_new) * acc_old + jnp.dot(vaug, p, preferred_element_type=F32)
    return m_new, acc


def _split_bf16(x):
    pieces = []
    for _ in range(F_PIECES):
        p = x.astype(BF16)
        pieces.append(p)
        x = x - p.astype(F32)
    return pieces


def _fox_sel():
    selk = np.zeros((F_PIECES, LANES, C_HEADS * HEAD_DIM), np.float32)
    selq = np.zeros((F_PIECES, C_HEADS * HEAD_DIM, LANES), np.float32)
    onesk = np.zeros((1, C_HEADS * HEAD_DIM), np.float32)
    onesq = np.zeros((C_HEADS * HEAD_DIM, 1), np.float32)
    for h in range(C_HEADS):
        for p in range(F_PIECES):
            selk[p, h, h * HEAD_DIM + p] = -1.0
            selq[p, h * HEAD_DIM + F_PIECES + p, h] = 1.0
            onesk[0, h * HEAD_DIM + F_PIECES + p] = 1.0
            onesq[h * HEAD_DIM + p, 0] = 1.0
    return selk, selq, onesk, onesq


def _foxgate_kernel(f_ref, b_ref, selk_ref, selq_ref, onesk_ref, onesq_ref, kf_ref, qft_ref, *, tb):
    nblk = f_ref.shape[0] // tb
    row = lax.broadcasted_iota(I32, (tb, tb), 0)
    col = lax.broadcasted_iota(I32, (tb, tb), 1)
    tri = jnp.where(col <= row, 1.0, 0.0).astype(F32)

    def body(j, carry):
        z = f_ref[pl.ds(j * tb, tb), :] + b_ref[...]
        ls = -(jnp.maximum(-z, 0.0) + jnp.log(1.0 + jnp.exp(-jnp.abs(z))))
        cs = jnp.dot(tri, ls, precision=lax.Precision.HIGHEST, preferred_element_type=F32) + carry
        kf = onesk_ref[...]
        for p, piece in enumerate(_split_bf16(cs)):
            kf = kf + jnp.dot(piece, selk_ref[p], preferred_element_type=F32)
        kf_ref[pl.ds(j * tb, tb), :] = kf.astype(BF16)
        qf = onesq_ref[...]
        for p, piece in enumerate(_split_bf16(cs.T)):
            qf = qf + jnp.dot(selq_ref[p], piece, preferred_element_type=F32)
        qft_ref[j] = qf.astype(BF16)
        return cs[tb - 1:tb, :]

    lax.fori_loop(0, nblk, body, jnp.zeros((1, LANES), F32))


def _foxgate(f, b_forget, batch, seq):
    tb = ATT_BLOCK
    nblk = seq // tb
    n = C_HEADS * HEAD_DIM
    bpad = jnp.zeros((1, LANES), F32).at[0, :C_HEADS].set(b_forget)
    selk, selq, onesk, onesq = _fox_sel()
    return pl.pallas_call(
        functools.partial(_foxgate_kernel, tb=tb),
        grid=(batch,),
        in_specs=[pl.BlockSpec((seq, LANES), lambda b: (b, 0)),
                  pl.BlockSpec((1, LANES), lambda b: (0, 0)),
                  pl.BlockSpec(selk.shape, lambda b: (0, 0, 0)),
                  pl.BlockSpec(selq.shape, lambda b: (0, 0, 0)),
                  pl.BlockSpec(onesk.shape, lambda b: (0, 0)),
                  pl.BlockSpec(onesq.shape, lambda b: (0, 0))],
        out_specs=[pl.BlockSpec((seq, n), lambda b: (b, 0)),
                   pl.BlockSpec((nblk, n, tb), lambda b: (b, 0, 0))],
        out_shape=[jax.ShapeDtypeStruct((batch * seq, n), BF16),
                   jax.ShapeDtypeStruct((batch * nblk, n, tb), BF16)],
        compiler_params=_params(("arbitrary",)), name="fox_gates",
    )(f, bpad, jnp.asarray(selk, BF16), jnp.asarray(selq, BF16), jnp.asarray(onesk), jnp.asarray(onesq))


def _fox_kernel(qt_ref, qft_ref, k_ref, kf_ref, vt_ref, o_ref, kaug_ref, vaug_ref, *, tb, nq):
    qi = pl.program_id(2)
    hd = HEAD_DIM
    heads = range(FOX_GROUP)

    @pl.when(qi == 0)
    def _():
        ones_rows = _ones_row(hd, tb)
        for h in heads:
            kaug_ref[h, :, 0:hd] = k_ref[:, h * hd:(h + 1) * hd]
            kaug_ref[h, :, hd:2 * hd] = kf_ref[:, h * hd:(h + 1) * hd]

            def fill(j, c, h=h):
                vaug_ref[h, j, 0:hd, :] = vt_ref[j, h * hd:(h + 1) * hd, :]
                vaug_ref[h, j, hd:2 * hd, :] = ones_rows
                return c

            lax.fori_loop(0, nq, fill, 0)

    qa = [jnp.concatenate([qt_ref[0, h * hd:(h + 1) * hd, :], qft_ref[0, h * hd:(h + 1) * hd, :]], axis=0)
          for h in heads]
    keep = _key_le_query(tb)

    def step(j, carry, masked):
        scores = [jnp.dot(kaug_ref[h, pl.ds(j * tb, tb), :], qa[h], preferred_element_type=F32)
                  for h in heads]
        out = []
        for h in heads:
            m, acc = carry[h]
            s = jnp.where(keep, scores[h], NEG_INF) if masked else scores[h]
            out.append(_flash_update(s, m, 0.0, acc, vaug_ref[h, j]))
        return tuple(out)

    init = tuple((jnp.full((1, tb), NEG_INF, F32), jnp.zeros((2 * hd, tb), F32)) for _ in heads)
    carry = lax.fori_loop(0, qi, lambda j, c: step(j, c, False), init)
    carry = step(qi, carry, True)
    for pair in range(FOX_GROUP // 2):
        ot = jnp.concatenate([carry[h][1][0:hd] / carry[h][1][hd:hd + 1] for h in (2 * pair, 2 * pair + 1)], axis=0)
        o_ref[:, pair * 2 * hd:(pair + 1) * 2 * hd] = ot.T.astype(o_ref.dtype)


def _fox_attention(qt, qft, k, kf, vt, batch, seq):
    tb = ATT_BLOCK
    nq = seq // tb
    groups = C_HEADS // FOX_GROUP
    gw = FOX_GROUP * HEAD_DIM
    return pl.pallas_call(
        functools.partial(_fox_kernel, tb=tb, nq=nq),
        grid=(batch, groups, nq),
        in_specs=[pl.BlockSpec((1, gw, tb), lambda b, g, i: (b * nq + i, g, 0)),
                  pl.BlockSpec((1, gw, tb), lambda b, g, i: (b * nq + i, g, 0)),
                  pl.BlockSpec((seq, gw), lambda b, g, i: (b, g)),
                  pl.BlockSpec((seq, gw), lambda b, g, i: (b, g)),
                  pl.BlockSpec((nq, gw, tb), lambda b, g, i: (b, g, 0))],
        out_specs=pl.BlockSpec((tb, gw), lambda b, g, i: (b * nq + i, g)),
        out_shape=jax.ShapeDtypeStruct((batch * seq, C_HEADS * HEAD_DIM), BF16),
        scratch_shapes=[pltpu.VMEM((FOX_GROUP, seq, 2 * HEAD_DIM), BF16),
                        pltpu.VMEM((FOX_GROUP, nq, 2 * HEAD_DIM, tb), BF16)],
        compiler_params=_params(("arbitrary", "arbitrary", "arbitrary")), name="fox_attention",
    )(qt, qft, k, kf, vt)


def _diff_kernel(rb_ref, qt_ref, k_ref, vt_ref, tiles_ref, lam_ref, g_ref, o_ref,
                 ks_ref, vaug_ref, acc_ref, *, tb, nq, lam_init):
    g = pl.program_id(1)
    qi = pl.program_id(2)
    hd = HEAD_DIM
    dv = 2 * hd
    chains = [(hh, s_) for hh in range(DIFF_GROUP) for s_ in range(2)]

    @pl.when(qi == 0)
    def _():
        ones_rows = _ones_row(BF16_ROWS, tb)
        for c, (hh, s_) in enumerate(chains):
            ks_ref[c] = k_ref[:, hh * dv + s_ * hd:hh * dv + (s_ + 1) * hd]
        for hh in range(DIFF_GROUP):
            def fill(j, c, hh=hh):
                vaug_ref[hh, j, 0:dv, :] = vt_ref[j, hh * dv:(hh + 1) * dv, :]
                vaug_ref[hh, j, dv:dv + BF16_ROWS, :] = ones_rows
                return c

            lax.fori_loop(0, nq, fill, 0)

    qs = [qt_ref[0, hh * dv + s_ * hd:hh * dv + (s_ + 1) * hd, :] for (hh, s_) in chains]
    cfar = [rb_ref[REL_BUCKETS - 1, A_HEADS + g * DIFF_GROUP + hh] for hh in range(DIFF_GROUP)]
    keep = _key_le_query(tb)
    acc_ref[...] = jnp.zeros_like(acc_ref)

    def step(j, ms, kind):
        scores = [jnp.dot(ks_ref[c, pl.ds(j * tb, tb), :], qs[c], preferred_element_type=F32)
                  for c in range(len(chains))]
        out = []
        for c, (hh, s_) in enumerate(chains):
            if kind == "far":
                s, shift = scores[c], cfar[hh]
            else:
                s, shift = scores[c] + tiles_ref[hh, 0 if kind == "diag" else 1], 0.0
                if kind == "diag":
                    s = jnp.where(keep, s, NEG_INF)
            m_new, acc = _flash_update(s, ms[c], shift, acc_ref[c], vaug_ref[hh, j])
            acc_ref[c] = acc
            out.append(m_new)
        return tuple(out)

    ms = tuple(jnp.full((1, tb), NEG_INF, F32) for _ in chains)
    ms = lax.fori_loop(0, jnp.maximum(qi - 1, 0), lambda j, c: step(j, c, "far"), ms)
    ms = lax.cond(qi >= 1, lambda c: step(qi - 1, c, "near"), lambda c: c, ms)
    step(qi, ms, "diag")

    lv = lam_ref[...]
    lam = (jnp.exp(jnp.sum(lv[0:1] * lv[1:2], axis=-1, keepdims=True))
           - jnp.exp(jnp.sum(lv[2:3] * lv[3:4], axis=-1, keepdims=True)) + lam_init)
    for hh in range(DIFF_GROUP):
        a1 = acc_ref[2 * hh]
        a2 = acc_ref[2 * hh + 1]
        ot = a1[0:dv] / a1[dv:dv + 1] - lam * (a2[0:dv] / a2[dv:dv + 1])
        o = _rms(ot.T, g_ref[...]) * (1.0 - lam_init)
        o_ref[:, hh * dv:(hh + 1) * dv] = o.astype(o_ref.dtype)


def _diff_attention(rel_bias, qt, k, vt, tiles, lam_vec, subln, batch, seq, lam_init):
    tb = ATT_BLOCK
    nq = seq // tb
    groups = B_HEADS // DIFF_GROUP
    gw = DIFF_GROUP * 2 * HEAD_DIM
    dv = 2 * HEAD_DIM
    return pl.pallas_call(
        functools.partial(_diff_kernel, tb=tb, nq=nq, lam_init=lam_init),
        grid=(batch, groups, nq),
        in_specs=[pl.BlockSpec(memory_space=pltpu.SMEM),
                  pl.BlockSpec((1, gw, tb), lambda b, g, i: (b * nq + i, g, 0)),
                  pl.BlockSpec((seq, gw), lambda b, g, i: (b, g)),
                  pl.BlockSpec((nq, gw, tb), lambda b, g, i: (b, g, 0)),
                  pl.BlockSpec((DIFF_GROUP, 2, tb, tb), lambda b, g, i: (A_HEADS // DIFF_GROUP + g, 0, 0, 0)),
                  pl.BlockSpec((4, HEAD_DIM), lambda b, g, i: (0, 0)),
                  pl.BlockSpec((1, dv), lambda b, g, i: (0, 0))],
        out_specs=pl.BlockSpec((tb, gw), lambda b, g, i: (b * nq + i, g)),
        out_shape=jax.ShapeDtypeStruct((batch * seq, B_HEADS * dv), BF16),
        scratch_shapes=[pltpu.VMEM((2 * DIFF_GROUP, seq, HEAD_DIM), BF16),
                        pltpu.VMEM((DIFF_GROUP, nq, dv + BF16_ROWS, tb), BF16),
                        pltpu.VMEM((2 * DIFF_GROUP, dv + BF16_ROWS, tb), F32)],
        compiler_params=_params(("arbitrary", "arbitrary", "arbitrary")), name="diff_attention",
    )(rel_bias, qt, k, vt, tiles, lam_vec, subln.reshape(1, -1))


def _dsa_kernel(rb_ref, qat_ref, qit_ref, wit_ref, kw_ref, ckv_ref, ckvt_ref, wuk_ref, wuvt_ref, tiles_ref,
                o_ref, keys_ref, selb_ref, qlat_ref, vaug_ref, acc_ref, *, tb, topk, nq):
    qb = pl.program_id(1)
    nblk = qb + 1
    hd = HEAD_DIM
    keep = _key_le_query(tb)

    @pl.when(qb == 0)
    def _():
        ones_rows = _ones_row(BF16_ROWS, tb)

        def fill(j, c):
            vaug_ref[j, 0:A_LATENT, :] = ckvt_ref[j]
            vaug_ref[j, A_LATENT:A_LATENT + BF16_ROWS, :] = ones_rows
            return c

        lax.fori_loop(0, nq, fill, 0)

    for h in range(A_HEADS):
        ql = jnp.dot(wuk_ref[h], qat_ref[0, h * hd:(h + 1) * hd, :], preferred_element_type=F32)
        qlat_ref[h] = ql.astype(BF16)

    wrows = [wit_ref[0, IDX_DIM + h:IDX_DIM + h + 1, :] * (IDX_HEADS ** -0.5) for h in range(IDX_HEADS)]
    zpad = jnp.zeros((LANES - IDX_DIM, tb), BF16)
    qi_pad = [jnp.concatenate([qit_ref[0, h * IDX_DIM:(h + 1) * IDX_DIM, :], zpad], axis=0)
              for h in range(IDX_HEADS)]

    def index_keys(j, masked):
        kb = kw_ref[pl.ds(j * tb, tb), :]
        isc = jnp.zeros((tb, tb), F32)
        for h in range(IDX_HEADS):
            li = jnp.dot(kb, qi_pad[h], preferred_element_type=F32)
            isc = isc + jnp.maximum(li, 0.0) * wrows[h]
        isc = jnp.where(isc == 0.0, 0.0, isc)
        bits = pltpu.bitcast(isc, I32)
        key = bits ^ ((bits >> 31) & 0x7FFFFFFF)
        if masked:
            key = jnp.where(keep, key, INT_MIN)
        keys_ref[j] = key

    def p1(j, c):
        index_keys(j, False)
        return c

    lax.fori_loop(0, qb, p1, 0)
    index_keys(qb, True)

    def count(pred):
        def body(j, acc):
            ind = jnp.where(pred(keys_ref[j]), 1.0, 0.0)
            return acc + jnp.sum(ind.reshape(tb // 8, 8, tb), axis=0)
        acc = lax.fori_loop(0, nblk, body, jnp.zeros((8, tb), F32))
        return jnp.sum(acc, axis=0, keepdims=True)

    kth = jnp.where(count(lambda k: k >= 0) >= topk, 0, INT_MIN).astype(I32)

    def bs(i, kth):
        cand = kth | lax.shift_left(jnp.int32(1), 30 - i)
        return jnp.where(count(lambda k: k >= cand) >= topk, cand, kth)

    kth = lax.fori_loop(0, 31, bs, kth)
    need = topk - count(lambda k: k > kth)

    lower = jnp.where(lax.broadcasted_iota(I32, (tb, tb), 1) <= lax.broadcasted_iota(I32, (tb, tb), 0),
                      1.0, 0.0).astype(BF16)

    def mask_block(j, seen, masked):
        key = keys_ref[j]
        eq = key == kth
        pre = jnp.dot(lower, jnp.where(eq, 1.0, 0.0).astype(BF16), preferred_element_type=F32)
        sel = (key > kth) | (eq & (pre + seen <= need))
        if masked:
            sel = sel & keep
        selb_ref[j] = jnp.where(sel, 0.0, NEG_INF)
        return seen + pre[tb - 1:tb, :]

    seen = lax.fori_loop(0, qb, lambda j, s: mask_block(j, s, False), jnp.zeros((1, tb), F32))
    mask_block(qb, seen, True)

    cfar = [rb_ref[REL_BUCKETS - 1, h] for h in range(A_HEADS)]
    acc_ref[...] = jnp.zeros_like(acc_ref)

    def step(j, ms, kind):
        kvb = ckv_ref[pl.ds(j * tb, tb), :]
        scores = [jnp.dot(kvb, qlat_ref[h], preferred_element_type=F32) for h in range(A_HEADS)]
        sb = selb_ref[j]
        out = []
        for h in range(A_HEADS):
            if kind == "far":
                s, shift = scores[h] + sb, cfar[h]
            else:
                s, shift = scores[h] + (tiles_ref[h, 0 if kind == "diag" else 1] + sb), 0.0
            m_new, acc = _flash_update(s, ms[h], shift, acc_ref[h], vaug_ref[j])
            acc_ref[h] = acc
            out.append(m_new)
        return tuple(out)

    ms = tuple(jnp.full((1, tb), NEG_INF, F32) for _ in range(A_HEADS))
    ms = lax.fori_loop(0, jnp.maximum(qb - 1, 0), lambda j, c: step(j, c, "far"), ms)
    ms = lax.cond(qb >= 1, lambda c: step(qb - 1, c, "near"), lambda c: c, ms)
    step(qb, ms, "diag")

    outs = []
    for h in range(A_HEADS):
        a = acc_ref[h]
        o_lat = (a[0:A_LATENT] / a[A_LATENT:A_LATENT + 1]).astype(BF16)
        outs.append(jnp.dot(wuvt_ref[h], o_lat, preferred_element_type=F32))
    o_ref[...] = jnp.concatenate(outs, axis=0).T.astype(o_ref.dtype)


def _dsa_attention(rel_bias, qat, qit, wit, kw, ckv, ckvt, wuk, wuvt, tiles, batch, seq):
    tb = ATT_BLOCK
    nq = seq // tb
    topk = min(TOPK_MAX, seq // 4)
    nqa = A_HEADS * HEAD_DIM
    nqi = IDX_HEADS * IDX_DIM
    aug = A_LATENT + BF16_ROWS
    return pl.pallas_call(
        functools.partial(_dsa_kernel, tb=tb, topk=topk, nq=nq),
        grid=(batch, nq),
        in_specs=[pl.BlockSpec(memory_space=pltpu.SMEM),
                  pl.BlockSpec((1, nqa, tb), lambda b, i: (b * nq + i, 0, 0)),
                  pl.BlockSpec((1, nqi, tb), lambda b, i: (b * nq + i, 0, 0)),
                  pl.BlockSpec((1, LANES, tb), lambda b, i: (b * nq + i, 0, 0)),
                  pl.BlockSpec((seq, LANES), lambda b, i: (b, 0)),
                  pl.BlockSpec((seq, A_LATENT), lambda b, i: (b, 0)),
                  pl.BlockSpec((nq, A_LATENT, tb), lambda b, i: (b, 0, 0)),
                  pl.BlockSpec(wuk.shape, lambda b, i: (0, 0, 0)),
                  pl.BlockSpec(wuvt.shape, lambda b, i: (0, 0, 0)),
                  pl.BlockSpec((A_HEADS, 2, tb, tb), lambda b, i: (0, 0, 0, 0))],
        out_specs=pl.BlockSpec((tb, nqa), lambda b, i: (b * nq + i, 0)),
        out_shape=jax.ShapeDtypeStruct((batch * seq, nqa), BF16),
        scratch_shapes=[pltpu.VMEM((nq, tb, tb), I32), pltpu.VMEM((nq, tb, tb), F32),
                        pltpu.VMEM((A_HEADS, A_LATENT, tb), BF16),
                        pltpu.VMEM((nq, aug, tb), BF16),
                        pltpu.VMEM((A_HEADS, aug, tb), F32)],
        compiler_params=_params(("arbitrary", "arbitrary")), name="dsa_attention",
    )(rel_bias, qat, qit, wit, kw, ckv, ckvt, wuk, wuvt, tiles)


RANK_BASE = 1e30
RANK_STEP = 1e28


def _rank_mark(r):
    return -(RANK_BASE + r * RANK_STEP)


def _top_sorted(x, k):
    rows = lax.broadcasted_iota(I32, (k, x.shape[1]), 0)
    out = jnp.zeros((k, x.shape[1]), F32)
    for r in range(k):
        m = jnp.max(x, axis=0, keepdims=True)
        out = jnp.where(rows == r, m, out)
        x = jnp.where(x == m, _rank_mark(r), x)
    return out, x


def _router_kernel(q_ref, keys_ref, c1_ref, a_ref, r2_ref, w_ref, *, tr):
    k = PEER_TOPK
    nk = keys_ref.shape[1]
    row8 = lax.broadcasted_iota(I32, (8, LANES), 0)
    for h in range(PEER_HEADS):
        for tc in range(tr // LANES):
            tok = slice(tc * LANES, (tc + 1) * LANES)
            qh = q_ref[tok, :]
            s1 = lax.dot_general(keys_ref[0], qh[:, (2 * h) * nk:(2 * h + 1) * nk], _NT,
                                 preferred_element_type=F32)
            s2 = lax.dot_general(keys_ref[1], qh[:, (2 * h + 1) * nk:(2 * h + 2) * nk], _NT,
                                 preferred_element_type=F32)
            a, marked1 = _top_sorted(s1, k)
            b, marked2 = _top_sorted(s2, k)
            b8 = b[0:8]
            parts = [a[0:1] + b, a[1:2] + b8]
            for i, lim in ((2, 5), (3, 4), (4, 3), (5, 2), (6, 2), (7, 2)):
                parts.append(jnp.where(row8 < lim, a[i:i + 1] + b8, -jnp.inf))
            parts.append(a[8:16] + b[0:1])
            cand = jnp.concatenate(parts, axis=0)
            x = cand
            thr = None
            for _ in range(k):
                thr = jnp.max(x, axis=0, keepdims=True)
                x = jnp.where(x == thr, -jnp.inf, x)
            mx = a[0:1] + b[0:1]
            z = jnp.sum(jnp.where(cand >= thr, jnp.exp(cand - mx), 0.0), axis=0, keepdims=True)
            c1 = jnp.zeros((nk, LANES), F32)
            for r in range(k):
                cnt = jnp.sum(jnp.where(a[r:r + 1] + b >= thr, 1.0, 0.0), axis=0, keepdims=True)
                c1 = jnp.where(marked1 == _rank_mark(r), cnt, c1)
            rank2 = jnp.where(marked2 <= -RANK_BASE,
                              jnp.floor((-marked2 - RANK_BASE) * (1.0 / RANK_STEP) + 0.5), 127.0)
            c1_ref[h, :, tok] = c1
            a_ref[h, :, tok] = jnp.exp(s1 - a[0:1]) / z
            r2_ref[h, :, tok] = rank2.astype(BF16)
            w_ref[h, :, tok] = jnp.exp(s2 - b[0:1]).astype(BF16)


def _router(q, sub_keys):
    t = q.shape[0]
    tr = ROUTER_TILE
    nk = sub_keys.shape[1]
    shp32 = jax.ShapeDtypeStruct((PEER_HEADS, nk, t), F32)
    shp16 = jax.ShapeDtypeStruct((PEER_HEADS, nk, t), BF16)
    spec = pl.BlockSpec((PEER_HEADS, nk, tr), lambda i: (0, 0, i))
    return pl.pallas_call(
        functools.partial(_router_kernel, tr=tr),
        grid=(t // tr,),
        in_specs=[pl.BlockSpec((tr, q.shape[1]), lambda i: (i, 0)),
                  pl.BlockSpec(sub_keys.shape, lambda i: (0, 0, 0))],
        out_specs=[spec] * 4, out_shape=[shp32, shp32, shp16, shp16],
        compiler_params=_params(("arbitrary",)), name="peer_router",
    )(q, sub_keys)


def _peer_kernel(h_ref, u_ref, vt_ref, c1_ref, a_ref, r2_ref, w_ref, x_ref, g_ref, *rest,
                 tm, te, nk, ne, final):
    if final:
        gf_ref, o_ref, act0_ref, act1_ref, p_ref, acc_ref, r2s_ref, ws_ref = rest
    else:
        o_ref, act0_ref, act1_ref, p_ref, acc_ref, r2s_ref, ws_ref = rest
    s = pl.program_id(0)
    e0 = jnp.maximum(s - 1, 0) % ne
    nsub = nk // BF16_ROWS

    @pl.when(s == 0)
    def _():
        act1_ref[...] = jnp.zeros_like(act1_ref)

    @pl.when(e0 == 0)
    def _():
        acc_ref[...] = jnp.zeros_like(acc_ref)
        r2s_ref[...] = r2_ref[...]
        ws_ref[...] = w_ref[...]

    def main(cur_ref, prev_ref):
        tile = 2 * LANES

        def gate_piece(ii_list, tc_list):
            for tc in tc_list:
                tok = slice(tc * LANES, (tc + 1) * LANES)
                gates = [[None] * nsub for _ in ii_list]
                for h in range(PEER_HEADS):
                    r2 = [r2s_ref[h, k * BF16_ROWS:(k + 1) * BF16_ROWS, tok] for k in range(nsub)]
                    w2 = [ws_ref[h, k * BF16_ROWS:(k + 1) * BF16_ROWS, tok] for k in range(nsub)]
                    for n, ii in enumerate(ii_list):
                        c = jnp.broadcast_to(c1_ref[h, ii:ii + 1, tok], (BF16_ROWS, LANES)).astype(BF16)
                        a = jnp.broadcast_to(a_ref[h, ii:ii + 1, tok], (BF16_ROWS, LANES)).astype(BF16)
                        for k in range(nsub):
                            t = jnp.where(r2[k] < c, w2[k] * a, 0)
                            gates[n][k] = t if gates[n][k] is None else gates[n][k] + t
                for n, ii in enumerate(ii_list):
                    for k in range(nsub):
                        rows = slice(ii * nk + k * BF16_ROWS, ii * nk + (k + 1) * BF16_ROWS)
                        p_ref[rows, tok] = gates[n][k] * jax.nn.gelu(prev_ref[rows, tok])

        ii_per = tile // nk
        tc_per = tile // LANES
        for nt in range(tm // tile):
            cols = slice(nt * tile, (nt + 1) * tile)
            for kt in range(te // tile):
                rows = slice(kt * tile, (kt + 1) * tile)
                gate_piece(list(range(kt * ii_per, (kt + 1) * ii_per)),
                           list(range(nt * tc_per, (nt + 1) * tc_per)))
                acc_ref[:, cols] += jnp.dot(vt_ref[:, rows], p_ref[rows, cols], preferred_element_type=F32)
                if kt % 2 == 1:
                    urows = slice((kt // 2) * 2 * tile, (kt // 2 + 1) * 2 * tile)
                    cur_ref[urows, cols] = lax.dot_general(u_ref[urows, :], h_ref[cols, :], _NT,
                                                           preferred_element_type=F32).astype(BF16)

    @pl.when(s % 2 == 0)
    def _():
        main(act0_ref, act1_ref)

    @pl.when(s % 2 == 1)
    def _():
        main(act1_ref, act0_ref)

    @pl.when((e0 == ne - 1) & (s > 0))
    def _():
        xn = x_ref[...] + g_ref[0] * acc_ref[...].T
        if final:
            xn = _rms(xn, gf_ref[...])
        o_ref[...] = xn


def _peer(h, u, vt, c1, a, r2, w, x, g2, seq, final_gain=None):
    t, d = x.shape
    tm, te = PEER_TOKEN_TILE, PEER_EXPERT_TILE
    ne = u.shape[0] // te
    nk = r2.shape[1]
    rows_per = te // nk
    per_b = seq // tm
    final = final_gain is not None
    steps = (t // tm) * ne + 1

    def cur(s):
        s1 = jnp.minimum(s, steps - 2)
        return s1 // ne, s1 % ne

    def prev(s):
        s0 = jnp.maximum(s - 1, 0)
        return s0 // ne, s0 % ne

    in_specs = [pl.BlockSpec((tm, d), lambda s: (cur(s)[0], 0)),
                pl.BlockSpec((te, d), lambda s: (cur(s)[1], 0)),
                pl.BlockSpec((d, te), lambda s: (0, prev(s)[1])),
                pl.BlockSpec((PEER_HEADS, rows_per, tm), lambda s: (0, prev(s)[1], prev(s)[0])),
                pl.BlockSpec((PEER_HEADS, rows_per, tm), lambda s: (0, prev(s)[1], prev(s)[0])),
                pl.BlockSpec((PEER_HEADS, nk, tm), lambda s: (0, 0, prev(s)[0])),
                pl.BlockSpec((PEER_HEADS, nk, tm), lambda s: (0, 0, prev(s)[0])),
                pl.BlockSpec((tm, d), lambda s: (prev(s)[0], 0)),
                pl.BlockSpec((1, 1, d), lambda s: (prev(s)[0] // per_b, 0, 0))]
    args = [h, u, vt, c1, a, r2, w, x, g2]
    if final:
        in_specs.append(pl.BlockSpec((1, d), lambda s: (0, 0)))
        args.append(final_gain.reshape(1, d))
    return pl.pallas_call(
        functools.partial(_peer_kernel, tm=tm, te=te, nk=nk, ne=ne, final=final),
        grid=(steps,),
        in_specs=in_specs,
        out_specs=pl.BlockSpec((tm, d), lambda s: (prev(s)[0], 0)),
        out_shape=jax.ShapeDtypeStruct((t, d), F32),
        scratch_shapes=[pltpu.VMEM((te, tm), BF16), pltpu.VMEM((te, tm), BF16),
                        pltpu.VMEM((te, tm), BF16), pltpu.VMEM((d, tm), F32),
                        pltpu.VMEM((PEER_HEADS, nk, tm), BF16), pltpu.VMEM((PEER_HEADS, nk, tm), BF16)],
        compiler_params=_params(("arbitrary",)), name="peer_experts",
    )(*args)


def _pad_cols(w, n):
    return jnp.pad(w, ((0, 0), (0, n - w.shape[1])))


def _even_w_in(w):
    na, ni, nb = A_HEADS * HEAD_DIM, IDX_HEADS * IDX_DIM, B_HEADS * 2 * HEAD_DIM
    o = np.cumsum([0, na, A_LATENT, ni, IDX_DIM, IDX_HEADS, nb, nb, nb])
    kw = _pad_cols(w[:, o[3]:o[5]], LANES)
    return jnp.concatenate([w[:, o[0]:o[3]], kw, w[:, o[5]:o[8]]], axis=1).astype(BF16)


def kernel(x, c, rel_bias, ada_w, ada_b, norm_mix, norm_ffn, norm_final, even_w_in, even_w_out,
           a_kv_norm, a_w_uk, a_w_uv, b_lambda, b_subln, odd_w_in, odd_b_forget, odd_w_out,
           peer_w_q, peer_sub_keys, peer_u, peer_v):
    batch, seq, d = x.shape
    depth = ada_w.shape[0]
    t = batch * seq
    assert seq % ROW_TILE == 0 and seq % ATT_BLOCK == 0 and t % PEER_TOKEN_TILE == 0

    mod = _ada(c, ada_w, ada_b)
    tiles = _bias_tiles(rel_bias)

    na, ni, nb = A_HEADS * HEAD_DIM, IDX_HEADS * IDX_DIM, B_HEADS * 2 * HEAD_DIM
    nc = C_HEADS * HEAD_DIM
    o = np.cumsum([0, na, A_LATENT, ni, LANES, nb, nb, nb])
    even_segs = [(o[0], o[1], "Tq", [BF16]), (o[1], o[2], "kvnorm", [BF16]), (o[2], o[3], "Tq", [BF16]),
                 (o[3], o[4], None, [BF16]), (o[3], o[4], "T", [F32]), (o[4], o[5], "Tq", [BF16]),
                 (o[5], o[6], None, [BF16]), (o[6], o[7], "T", [BF16])]
    odd_segs = [(0, nc, "Tq", [BF16]), (nc, 2 * nc, None, [BF16]), (2 * nc, 3 * nc, "T", [BF16]),
                (3 * nc, 3 * nc + LANES, None, [F32])]
    nq_peer = peer_w_q.shape[2]
    peer_segs = [(0, nq_peer, None, [BF16])]

    xt = x.reshape(t, d)
    for layer in range(depth):
        m6 = mod[layer].reshape(batch, 6, 1, d)
        sh1, sc1, g1, sh2, sc2, g2 = [m6[:, i] for i in range(6)]
        if layer % 2 == 0:
            e = layer // 2
            lam_init = 0.8 - 0.6 * math.exp(-0.3 * layer)
            w_in = _even_w_in(even_w_in[e])
            qat, ckv, ckvt, qit, kw, wit, qbt, k_b, vbt = _normproj(
                xt, norm_mix[layer], sc1, sh1, w_in, even_segs, seq, kvn=a_kv_norm[e], wt=w_in.T,
                name="even_in_proj")
            wuk = jnp.transpose(a_w_uk[e], (1, 0, 2)).astype(BF16)
            wuvt = jnp.transpose(a_w_uv[e], (1, 2, 0)).astype(BF16)
            o_a = _dsa_attention(rel_bias, qat, qit, wit, kw, ckv, ckvt, wuk, wuvt, tiles, batch, seq)
            o_b = _diff_attention(rel_bias, qbt, k_b, vbt, tiles, b_lambda[e], b_subln[e], batch, seq, lam_init)
            mixes = [o_a, o_b]
            w_out = even_w_out[e].astype(BF16)
        else:
            od = layer // 2
            w_in = _pad_cols(odd_w_in[od], 3 * nc + LANES).astype(BF16)
            qt, k, vt, f = _normproj(xt, norm_mix[layer], sc1, sh1, w_in, odd_segs, seq, wt=w_in.T,
                                     name="odd_in_proj")
            kf, qft = _foxgate(f, odd_b_forget[od], batch, seq)
            mixes = [_fox_attention(qt, qft, k, kf, vt, batch, seq)]
            w_out = odd_w_out[od].astype(BF16)
        xt, h2, qp = _resproj(xt, mixes, w_out, g1, norm_ffn[layer], sc2, sh2,
                              peer_w_q[layer].astype(BF16), peer_segs, seq)
        c1, a, r2, w = _router(qp, peer_sub_keys[layer].astype(BF16))
        xt = _peer(h2, peer_u[layer].astype(BF16), peer_v[layer].T.astype(BF16), c1, a, r2, w,
                   xt, g2, seq, final_gain=norm_final if layer == depth - 1 else None)
    return xt.reshape(batch, seq, d)
```

```python
import functools
import math

import numpy as np
import jax
import jax.numpy as jnp
from jax import lax
from jax.experimental import pallas as pl
from jax.experimental.pallas import tpu as pltpu

F32 = jnp.float32
BF16 = jnp.bfloat16
I32 = jnp.int32

HEAD_DIM = 64
RMS_EPS = 1e-6
NEG_INF = -1e30
A_HEADS = 8
A_LATENT = 256
IDX_HEADS = 8
IDX_DIM = 64
TOPK_MAX = 256
B_HEADS = 4
C_HEADS = 16
REL_BUCKETS = 32
REL_MAX_DIST = 128
PEER_HEADS = 8
PEER_TOPK = 16

LANES = 128
BF16_ROWS = 16
ATT_BLOCK = 256
ROW_TILE = 512
PEER_TOKEN_TILE = 512
PEER_EXPERT_TILE = 2048
ROUTER_TILE = 512
FOX_GROUP = 8
DIFF_GROUP = 4
F_PIECES = 3
VMEM_LIMIT = 56 * 1024 * 1024
INT_MIN = -2 ** 31

_NT = (((1,), (1,)), ((), ()))


def _params(sem):
    return pltpu.CompilerParams(dimension_semantics=sem, vmem_limit_bytes=VMEM_LIMIT)


def _rms(x, g):
    return x * lax.rsqrt(jnp.mean(x * x, axis=-1, keepdims=True) + RMS_EPS) * g


def _ones_row(rows, cols):
    return jnp.where(lax.broadcasted_iota(I32, (rows, cols), 0) == 0, 1.0, 0.0).astype(BF16)


def _key_le_query(tb):
    return lax.broadcasted_iota(I32, (tb, tb), 0) <= lax.broadcasted_iota(I32, (tb, tb), 1)


def _ada_kernel(c_ref, w_ref, b_ref, o_ref):
    c = c_ref[...]
    ca = c * jax.nn.sigmoid(c)
    o_ref[0] = jnp.dot(ca, w_ref[0], precision=lax.Precision.HIGHEST,
                       preferred_element_type=F32) + b_ref[0]


def _ada(c, ada_w, ada_b):
    depth, d, n = ada_w.shape
    b = c.shape[0]
    bp = 8
    cp = jnp.zeros((bp, d), F32).at[:b].set(c)
    tn = 1536
    out = pl.pallas_call(
        _ada_kernel,
        grid=(depth, n // tn),
        in_specs=[pl.BlockSpec((bp, d), lambda l, j: (0, 0)),
                  pl.BlockSpec((1, d, tn), lambda l, j: (l, 0, j)),
                  pl.BlockSpec((1, 1, tn), lambda l, j: (l, 0, j))],
        out_specs=pl.BlockSpec((1, bp, tn), lambda l, j: (l, 0, j)),
        out_shape=jax.ShapeDtypeStruct((depth, bp, n), F32),
        compiler_params=_params(("arbitrary", "arbitrary")),
        name="ada_mod",
    )(cp, ada_w, ada_b.reshape(depth, 1, n))
    return out[:, :b]


def _store_transposed(out_ref, rt, dt):
    for c in range(rt.shape[1] // ATT_BLOCK):
        out_ref[c] = rt[:, c * ATT_BLOCK:(c + 1) * ATT_BLOCK].astype(dt)


def _emit_segments(hb, w_ref, wt_ref, segs, outs, kvn_ref):
    k = 0
    for (a, b, kind, dtypes) in segs:
        if kind in ("T", "Tq"):
            r = lax.dot_general(wt_ref[a:b, :], hb, _NT, preferred_element_type=F32)
            if kind == "Tq":
                r = r * (HEAD_DIM ** -0.5)
            for dt in dtypes:
                _store_transposed(outs[k], r, dt)
                k += 1
            continue
        r = jnp.dot(hb, w_ref[:, a:b], preferred_element_type=F32)
        if kind == "kvnorm":
            r = _rms(r, kvn_ref[...])
            outs[k][...] = r.astype(dtypes[0])
            _store_transposed(outs[k + 1], r.T, dtypes[0])
            k += 2
            continue
        for dt in dtypes:
            outs[k][...] = r.astype(dt)
            k += 1


def _normproj_kernel(*refs, segs, has_kvn, has_wt):
    x_ref, gain_ref, sc_ref, sh_ref, w_ref = refs[:5]
    pos = 5
    wt_ref = kvn_ref = None
    if has_wt:
        wt_ref = refs[pos]
        pos += 1
    if has_kvn:
        kvn_ref = refs[pos]
        pos += 1
    outs = list(refs[pos:])
    h = _rms(x_ref[...], gain_ref[...]) * (1.0 + sc_ref[0]) + sh_ref[0]
    _emit_segments(h.astype(BF16), w_ref, wt_ref, segs, outs, kvn_ref)


def _resproj_kernel(*refs, n_mix, segs):
    x_ref = refs[0]
    mix_refs = refs[1:1 + n_mix]
    wo_ref, g_ref, gain_ref, sc_ref, sh_ref, w_ref = refs[1 + n_mix:7 + n_mix]
    outs = list(refs[7 + n_mix:])
    y = None
    off = 0
    for m in mix_refs:
        kdim = m.shape[1]
        t = jnp.dot(m[...], wo_ref[off:off + kdim, :], preferred_element_type=F32)
        y = t if y is None else y + t
        off += kdim
    xn = x_ref[...] + g_ref[0] * y
    outs[0][...] = xn
    h = _rms(xn, gain_ref[...]) * (1.0 + sc_ref[0]) + sh_ref[0]
    hb = h.astype(BF16)
    outs[1][...] = hb
    _emit_segments(hb, w_ref, None, segs, outs[2:], None)


def _seg_out_shapes(t, segs):
    shapes, specs = [], []
    per = ROW_TILE // ATT_BLOCK

    def plain(n, dt):
        shapes.append(jax.ShapeDtypeStruct((t, n), dt))
        specs.append(pl.BlockSpec((ROW_TILE, n), lambda i: (i, 0)))

    def transposed(n, dt):
        shapes.append(jax.ShapeDtypeStruct((t // ATT_BLOCK, n, ATT_BLOCK), dt))
        specs.append(pl.BlockSpec((per, n, ATT_BLOCK), lambda i: (i, 0, 0)))

    for (a, b, kind, dtypes) in segs:
        if kind == "kvnorm":
            plain(b - a, dtypes[0])
            transposed(b - a, dtypes[0])
            continue
        for dt in dtypes:
            (transposed if kind in ("T", "Tq") else plain)(b - a, dt)
    return shapes, specs


def _normproj(x, gain, sc, sh, w, segs, seq, kvn=None, wt=None, name="normproj"):
    t, d = x.shape
    tm = ROW_TILE
    per_b = seq // tm
    n = w.shape[1]
    mod_spec = pl.BlockSpec((1, 1, d), lambda i: (i // per_b, 0, 0))
    in_specs = [pl.BlockSpec((tm, d), lambda i: (i, 0)),
                pl.BlockSpec((1, d), lambda i: (0, 0)), mod_spec, mod_spec,
                pl.BlockSpec((d, n), lambda i: (0, 0))]
    args = [x, gain.reshape(1, d), sc, sh, w]
    if wt is not None:
        in_specs.append(pl.BlockSpec((n, d), lambda i: (0, 0)))
        args.append(wt)
    if kvn is not None:
        in_specs.append(pl.BlockSpec((1, kvn.shape[-1]), lambda i: (0, 0)))
        args.append(kvn.reshape(1, -1))
    shapes, specs = _seg_out_shapes(t, segs)
    return pl.pallas_call(
        functools.partial(_normproj_kernel, segs=segs, has_kvn=kvn is not None, has_wt=wt is not None),
        grid=(t // tm,), in_specs=in_specs, out_specs=specs, out_shape=shapes,
        compiler_params=_params(("arbitrary",)), name=name,
    )(*args)


def _resproj(x, mixes, w_out, g, gain, sc, sh, w, segs, seq, name="resproj"):
    t, d = x.shape
    tm = ROW_TILE
    per_b = seq // tm
    n = w.shape[1]
    mod_spec = pl.BlockSpec((1, 1, d), lambda i: (i // per_b, 0, 0))
    in_specs = [pl.BlockSpec((tm, d), lambda i: (i, 0))]
    in_specs += [pl.BlockSpec((tm, m.shape[1]), lambda i: (i, 0)) for m in mixes]
    in_specs += [pl.BlockSpec(w_out.shape, lambda i: (0, 0)), mod_spec,
                 pl.BlockSpec((1, d), lambda i: (0, 0)), mod_spec, mod_spec,
                 pl.BlockSpec((d, n), lambda i: (0, 0))]
    shapes, specs = _seg_out_shapes(t, segs)
    shapes = [jax.ShapeDtypeStruct((t, d), F32), jax.ShapeDtypeStruct((t, d), BF16)] + shapes
    specs = [pl.BlockSpec((tm, d), lambda i: (i, 0)), pl.BlockSpec((tm, d), lambda i: (i, 0))] + specs
    return pl.pallas_call(
        functools.partial(_resproj_kernel, n_mix=len(mixes), segs=segs),
        grid=(t // tm,), in_specs=in_specs, out_specs=specs, out_shape=shapes,
        compiler_params=_params(("arbitrary",)), name=name,
    )(x, *mixes, w_out, g, gain.reshape(1, d), sc, sh, w)


def _bucket_table(n):
    max_exact = REL_BUCKETS // 2
    d = np.arange(n)
    df = np.maximum(d, 1).astype(np.float32)
    large = max_exact + (np.log(df / max_exact) / math.log(REL_MAX_DIST / max_exact)
                         * (REL_BUCKETS - max_exact)).astype(np.int32)
    large = np.minimum(large, REL_BUCKETS - 1)
    return np.where(d < max_exact, d, large).astype(np.int32)


def _bias_tiles_kernel(rb_ref, bk_ref, o_ref):
    h = pl.program_id(0)
    for t in range(2):
        bt = bk_ref[t]
        acc = jnp.zeros(bt.shape, F32)
        for b in range(REL_BUCKETS):
            acc = jnp.where(bt == b, rb_ref[b, h], acc)
        o_ref[0, t] = acc


def _bias_tiles(rel_bias):
    tb = ATT_BLOCK
    nh = rel_bias.shape[1]
    table = _bucket_table(2 * tb)
    s = np.arange(tb)[:, None]
    t = np.arange(tb)[None, :]
    bk = np.stack([table[np.maximum(t - s, 0)], table[tb + t - s]]).astype(np.int32)
    return pl.pallas_call(
        _bias_tiles_kernel,
        grid=(nh,),
        in_specs=[pl.BlockSpec(memory_space=pltpu.SMEM),
                  pl.BlockSpec((2, tb, tb), lambda h: (0, 0, 0))],
        out_specs=pl.BlockSpec((1, 2, tb, tb), lambda h: (h, 0, 0, 0)),
        out_shape=jax.ShapeDtypeStruct((nh, 2, tb, tb), F32),
        compiler_params=_params(("arbitrary",)), name="bias_tiles",
    )(rel_bias, jnp.asarray(bk))


def _flash_update(s, m, shift_const, acc_old, vaug):
    m_new = jnp.maximum(m, jnp.max(s, axis=0, keepdims=True) + shift_const)
    p = jnp.exp(s - (m_new - shift_const)).astype(BF16)
    acc = jnp.exp(m - m_new) * acc_old + jnp.dot(vaug, p, preferred_element_type=F32)
    return m_new, acc


def _split_bf16(x):
    pieces = []
    for _ in range(F_PIECES):
        p = x.astype(BF16)
        pieces.append(p)
        x = x - p.astype(F32)
    return pieces


def _fox_sel():
    selk = np.zeros((F_PIECES, LANES, C_HEADS * HEAD_DIM), np.float32)
    selq = np.zeros((F_PIECES, C_HEADS * HEAD_DIM, LANES), np.float32)
    onesk = np.zeros((1, C_HEADS * HEAD_DIM), np.float32)
    onesq = np.zeros((C_HEADS * HEAD_DIM, 1), np.float32)
    for h in range(C_HEADS):
        for p in range(F_PIECES):
            selk[p, h, h * HEAD_DIM + p] = -1.0
            selq[p, h * HEAD_DIM + F_PIECES + p, h] = 1.0
            onesk[0, h * HEAD_DIM + F_PIECES + p] = 1.0
            onesq[h * HEAD_DIM + p, 0] = 1.0
    return selk, selq, onesk, onesq


def _foxgate_kernel(f_ref, b_ref, selk_ref, selq_ref, onesk_ref, onesq_ref, kf_ref, qft_ref, *, tb):
    nblk = f_ref.shape[0] // tb
    row = lax.broadcasted_iota(I32, (tb, tb), 0)
    col = lax.broadcasted_iota(I32, (tb, tb), 1)
    tri = jnp.where(col <= row, 1.0, 0.0).astype(F32)

    def body(j, carry):
        z = f_ref[pl.ds(j * tb, tb), :] + b_ref[...]
        ls = -(jnp.maximum(-z, 0.0) + jnp.log(1.0 + jnp.exp(-jnp.abs(z))))
        cs = jnp.dot(tri, ls, precision=lax.Precision.HIGHEST, preferred_element_type=F32) + carry
        kf = onesk_ref[...]
        for p, piece in enumerate(_split_bf16(cs)):
            kf = kf + jnp.dot(piece, selk_ref[p], preferred_element_type=F32)
        kf_ref[pl.ds(j * tb, tb), :] = kf.astype(BF16)
        qf = onesq_ref[...]
        for p, piece in enumerate(_split_bf16(cs.T)):
            qf = qf + jnp.dot(selq_ref[p], piece, preferred_element_type=F32)
        qft_ref[j] = qf.astype(BF16)
        return cs[tb - 1:tb, :]

    lax.fori_loop(0, nblk, body, jnp.zeros((1, LANES), F32))


def _foxgate(f, b_forget, batch, seq):
    tb = ATT_BLOCK
    nblk = seq // tb
    n = C_HEADS * HEAD_DIM
    bpad = jnp.zeros((1, LANES), F32).at[0, :C_HEADS].set(b_forget)
    selk, selq, onesk, onesq = _fox_sel()
    return pl.pallas_call(
        functools.partial(_foxgate_kernel, tb=tb),
        grid=(batch,),
        in_specs=[pl.BlockSpec((seq, LANES), lambda b: (b, 0)),
                  pl.BlockSpec((1, LANES), lambda b: (0, 0)),
                  pl.BlockSpec(selk.shape, lambda b: (0, 0, 0)),
                  pl.BlockSpec(selq.shape, lambda b: (0, 0, 0)),
                  pl.BlockSpec(onesk.shape, lambda b: (0, 0)),
                  pl.BlockSpec(onesq.shape, lambda b: (0, 0))],
        out_specs=[pl.BlockSpec((seq, n), lambda b: (b, 0)),
                   pl.BlockSpec((nblk, n, tb), lambda b: (b, 0, 0))],
        out_shape=[jax.ShapeDtypeStruct((batch * seq, n), BF16),
                   jax.ShapeDtypeStruct((batch * nblk, n, tb), BF16)],
        compiler_params=_params(("arbitrary",)), name="fox_gates",
    )(f, bpad, jnp.asarray(selk, BF16), jnp.asarray(selq, BF16), jnp.asarray(onesk), jnp.asarray(onesq))


def _fox_kernel(qt_ref, qft_ref, k_ref, kf_ref, vt_ref, o_ref, kaug_ref, vaug_ref, *, tb, nq):
    qi = pl.program_id(2)
    hd = HEAD_DIM
    heads = range(FOX_GROUP)

    @pl.when(qi == 0)
    def _():
        ones_rows = _ones_row(hd, tb)
        for h in heads:
            kaug_ref[h, :, 0:hd] = k_ref[:, h * hd:(h + 1) * hd]
            kaug_ref[h, :, hd:2 * hd] = kf_ref[:, h * hd:(h + 1) * hd]

            def fill(j, c, h=h):
                vaug_ref[h, j, 0:hd, :] = vt_ref[j, h * hd:(h + 1) * hd, :]
                vaug_ref[h, j, hd:2 * hd, :] = ones_rows
                return c

            lax.fori_loop(0, nq, fill, 0)

    qa = [jnp.concatenate([qt_ref[0, h * hd:(h + 1) * hd, :], qft_ref[0, h * hd:(h + 1) * hd, :]], axis=0)
          for h in heads]
    keep = _key_le_query(tb)

    def step(j, carry, masked):
        scores = [jnp.dot(kaug_ref[h, pl.ds(j * tb, tb), :], qa[h], preferred_element_type=F32)
                  for h in heads]
        out = []
        for h in heads:
            m, acc = carry[h]
            s = jnp.where(keep, scores[h], NEG_INF) if masked else scores[h]
            out.append(_flash_update(s, m, 0.0, acc, vaug_ref[h, j]))
        return tuple(out)

    init = tuple((jnp.full((1, tb), NEG_INF, F32), jnp.zeros((2 * hd, tb), F32)) for _ in heads)
    carry = lax.fori_loop(0, qi, lambda j, c: step(j, c, False), init)
    carry = step(qi, carry, True)
    for pair in range(FOX_GROUP // 2):
        ot = jnp.concatenate([carry[h][1][0:hd] / carry[h][1][hd:hd + 1] for h in (2 * pair, 2 * pair + 1)], axis=0)
        o_ref[:, pair * 2 * hd:(pair + 1) * 2 * hd] = ot.T.astype(o_ref.dtype)


def _fox_attention(qt, qft, k, kf, vt, batch, seq):
    tb = ATT_BLOCK
    nq = seq // tb
    groups = C_HEADS // FOX_GROUP
    gw = FOX_GROUP * HEAD_DIM
    return pl.pallas_call(
        functools.partial(_fox_kernel, tb=tb, nq=nq),
        grid=(batch, groups, nq),
        in_specs=[pl.BlockSpec((1, gw, tb), lambda b, g, i: (b * nq + i, g, 0)),
                  pl.BlockSpec((1, gw, tb), lambda b, g, i: (b * nq + i, g, 0)),
                  pl.BlockSpec((seq, gw), lambda b, g, i: (b, g)),
                  pl.BlockSpec((seq, gw), lambda b, g, i: (b, g)),
                  pl.BlockSpec((nq, gw, tb), lambda b, g, i: (b, g, 0))],
        out_specs=pl.BlockSpec((tb, gw), lambda b, g, i: (b * nq + i, g)),
        out_shape=jax.ShapeDtypeStruct((batch * seq, C_HEADS * HEAD_DIM), BF16),
        scratch_shapes=[pltpu.VMEM((FOX_GROUP, seq, 2 * HEAD_DIM), BF16),
                        pltpu.VMEM((FOX_GROUP, nq, 2 * HEAD_DIM, tb), BF16)],
        compiler_params=_params(("arbitrary", "arbitrary", "arbitrary")), name="fox_attention",
    )(qt, qft, k, kf, vt)


def _diff_kernel(rb_ref, qt_ref, k_ref, vt_ref, tiles_ref, lam_ref, g_ref, o_ref,
                 ks_ref, vaug_ref, acc_ref, *, tb, nq, lam_init):
    g = pl.program_id(1)
    qi = pl.program_id(2)
    hd = HEAD_DIM
    dv = 2 * hd
    chains = [(hh, s_) for hh in range(DIFF_GROUP) for s_ in range(2)]

    @pl.when(qi == 0)
    def _():
        ones_rows = _ones_row(BF16_ROWS, tb)
        for c, (hh, s_) in enumerate(chains):
            ks_ref[c] = k_ref[:, hh * dv + s_ * hd:hh * dv + (s_ + 1) * hd]
        for hh in range(DIFF_GROUP):
            def fill(j, c, hh=hh):
                vaug_ref[hh, j, 0:dv, :] = vt_ref[j, hh * dv:(hh + 1) * dv, :]
                vaug_ref[hh, j, dv:dv + BF16_ROWS, :] = ones_rows
                return c

            lax.fori_loop(0, nq, fill, 0)

    qs = [qt_ref[0, hh * dv + s_ * hd:hh * dv + (s_ + 1) * hd, :] for (hh, s_) in chains]
    cfar = [rb_ref[REL_BUCKETS - 1, A_HEADS + g * DIFF_GROUP + hh] for hh in range(DIFF_GROUP)]
    keep = _key_le_query(tb)
    acc_ref[...] = jnp.zeros_like(acc_ref)

    def step(j, ms, kind):
        scores = [jnp.dot(ks_ref[c, pl.ds(j * tb, tb), :], qs[c], preferred_element_type=F32)
                  for c in range(len(chains))]
        out = []
        for c, (hh, s_) in enumerate(chains):
            if kind == "far":
                s, shift = scores[c], cfar[hh]
            else:
                s, shift = scores[c] + tiles_ref[hh, 0 if kind == "diag" else 1], 0.0
                if kind == "diag":
                    s = jnp.where(keep, s, NEG_INF)
            m_new, acc = _flash_update(s, ms[c], shift, acc_ref[c], vaug_ref[hh, j])
            acc_ref[c] = acc
            out.append(m_new)
        return tuple(out)

    ms = tuple(jnp.full((1, tb), NEG_INF, F32) for _ in chains)
    ms = lax.fori_loop(0, jnp.maximum(qi - 1, 0), lambda j, c: step(j, c, "far"), ms)
    ms = lax.cond(qi >= 1, lambda c: step(qi - 1, c, "near"), lambda c: c, ms)
    step(qi, ms, "diag")

    lv = lam_ref[...]
    lam = (jnp.exp(jnp.sum(lv[0:1] * lv[1:2], axis=-1, keepdims=True))
           - jnp.exp(jnp.sum(lv[2:3] * lv[3:4], axis=-1, keepdims=True)) + lam_init)
    for hh in range(DIFF_GROUP):
        a1 = acc_ref[2 * hh]
        a2 = acc_ref[2 * hh + 1]
        ot = a1[0:dv] / a1[dv:dv + 1] - lam * (a2[0:dv] / a2[dv:dv + 1])
        o = _rms(ot.T, g_ref[...]) * (1.0 - lam_init)
        o_ref[:, hh * dv:(hh + 1) * dv] = o.astype(o_ref.dtype)


def _diff_attention(rel_bias, qt, k, vt, tiles, lam_vec, subln, batch, seq, lam_init):
    tb = ATT_BLOCK
    nq = seq // tb
    groups = B_HEADS // DIFF_GROUP
    gw = DIFF_GROUP * 2 * HEAD_DIM
    dv = 2 * HEAD_DIM
    return pl.pallas_call(
        functools.partial(_diff_kernel, tb=tb, nq=nq, lam_init=lam_init),
        grid=(batch, groups, nq),
        in_specs=[pl.BlockSpec(memory_space=pltpu.SMEM),
                  pl.BlockSpec((1, gw, tb), lambda b, g, i: (b * nq + i, g, 0)),
                  pl.BlockSpec((seq, gw), lambda b, g, i: (b, g)),
                  pl.BlockSpec((nq, gw, tb), lambda b, g, i: (b, g, 0)),
                  pl.BlockSpec((DIFF_GROUP, 2, tb, tb), lambda b, g, i: (A_HEADS // DIFF_GROUP + g, 0, 0, 0)),
                  pl.BlockSpec((4, HEAD_DIM), lambda b, g, i: (0, 0)),
                  pl.BlockSpec((1, dv), lambda b, g, i: (0, 0))],
        out_specs=pl.BlockSpec((tb, gw), lambda b, g, i: (b * nq + i, g)),
        out_shape=jax.ShapeDtypeStruct((batch * seq, B_HEADS * dv), BF16),
        scratch_shapes=[pltpu.VMEM((2 * DIFF_GROUP, seq, HEAD_DIM), BF16),
                        pltpu.VMEM((DIFF_GROUP, nq, dv + BF16_ROWS, tb), BF16),
                        pltpu.VMEM((2 * DIFF_GROUP, dv + BF16_ROWS, tb), F32)],
        compiler_params=_params(("arbitrary", "arbitrary", "arbitrary")), name="diff_attention",
    )(rel_bias, qt, k, vt, tiles, lam_vec, subln.reshape(1, -1))


def _dsa_kernel(rb_ref, qat_ref, qit_ref, wit_ref, kw_ref, ckv_ref, ckvt_ref, wuk_ref, wuvt_ref, tiles_ref,
                o_ref, keys_ref, selb_ref, qlat_ref, vaug_ref, acc_ref, *, tb, topk, nq):
    qb = pl.program_id(1)
    nblk = qb + 1
    hd = HEAD_DIM
    keep = _key_le_query(tb)

    @pl.when(qb == 0)
    def _():
        ones_rows = _ones_row(BF16_ROWS, tb)

        def fill(j, c):
            vaug_ref[j, 0:A_LATENT, :] = ckvt_ref[j]
            vaug_ref[j, A_LATENT:A_LATENT + BF16_ROWS, :] = ones_rows
            return c

        lax.fori_loop(0, nq, fill, 0)

    for h in range(A_HEADS):
        ql = jnp.dot(wuk_ref[h], qat_ref[0, h * hd:(h + 1) * hd, :], preferred_element_type=F32)
        qlat_ref[h] = ql.astype(BF16)

    wrows = [wit_ref[0, IDX_DIM + h:IDX_DIM + h + 1, :] * (IDX_HEADS ** -0.5) for h in range(IDX_HEADS)]
    zpad = jnp.zeros((LANES - IDX_DIM, tb), BF16)
    qi_pad = [jnp.concatenate([qit_ref[0, h * IDX_DIM:(h + 1) * IDX_DIM, :], zpad], axis=0)
              for h in range(IDX_HEADS)]

    def index_keys(j, masked):
        kb = kw_ref[pl.ds(j * tb, tb), :]
        isc = jnp.zeros((tb, tb), F32)
        for h in range(IDX_HEADS):
            li = jnp.dot(kb, qi_pad[h], preferred_element_type=F32)
            isc = isc + jnp.maximum(li, 0.0) * wrows[h]
        isc = jnp.where(isc == 0.0, 0.0, isc)
        bits = pltpu.bitcast(isc, I32)
        key = bits ^ ((bits >> 31) & 0x7FFFFFFF)
        if masked:
            key = jnp.where(keep, key, INT_MIN)
        keys_ref[j] = key

    def p1(j, c):
        index_keys(j, False)
        return c

    lax.fori_loop(0, qb, p1, 0)
    index_keys(qb, True)

    def count(pred):
        def body(j, acc):
            ind = jnp.where(pred(keys_ref[j]), 1.0, 0.0)
            return acc + jnp.sum(ind.reshape(tb // 8, 8, tb), axis=0)
        acc = lax.fori_loop(0, nblk, body, jnp.zeros((8, tb), F32))
        return jnp.sum(acc, axis=0, keepdims=True)

    kth = jnp.where(count(lambda k: k >= 0) >= topk, 0, INT_MIN).astype(I32)

    def bs(i, kth):
        cand = kth | lax.shift_left(jnp.int32(1), 30 - i)
        return jnp.where(count(lambda k: k >= cand) >= topk, cand, kth)

    kth = lax.fori_loop(0, 31, bs, kth)
    need = topk - count(lambda k: k > kth)

    lower = jnp.where(lax.broadcasted_iota(I32, (tb, tb), 1) <= lax.broadcasted_iota(I32, (tb, tb), 0),
                      1.0, 0.0).astype(BF16)

    def mask_block(j, seen, masked):
        key = keys_ref[j]
        eq = key == kth
        pre = jnp.dot(lower, jnp.where(eq, 1.0, 0.0).astype(BF16), preferred_element_type=F32)
        sel = (key > kth) | (eq & (pre + seen <= need))
        if masked:
            sel = sel & keep
        selb_ref[j] = jnp.where(sel, 0.0, NEG_INF)
        return seen + pre[tb - 1:tb, :]

    seen = lax.fori_loop(0, qb, lambda j, s: mask_block(j, s, False), jnp.zeros((1, tb), F32))
    mask_block(qb, seen, True)

    cfar = [rb_ref[REL_BUCKETS - 1, h] for h in range(A_HEADS)]
    acc_ref[...] = jnp.zeros_like(acc_ref)

    def step(j, ms, kind):
        kvb = ckv_ref[pl.ds(j * tb, tb), :]
        scores = [jnp.dot(kvb, qlat_ref[h], preferred_element_type=F32) for h in range(A_HEADS)]
        sb = selb_ref[j]
        out = []
        for h in range(A_HEADS):
            if kind == "far":
                s, shift = scores[h] + sb, cfar[h]
            else:
                s, shift = scores[h] + (tiles_ref[h, 0 if kind == "diag" else 1] + sb), 0.0
            m_new, acc = _flash_update(s, ms[h], shift, acc_ref[h], vaug_ref[j])
            acc_ref[h] = acc
            out.append(m_new)
        return tuple(out)

    ms = tuple(jnp.full((1, tb), NEG_INF, F32) for _ in range(A_HEADS))
    ms = lax.fori_loop(0, jnp.maximum(qb - 1, 0), lambda j, c: step(j, c, "far"), ms)
    ms = lax.cond(qb >= 1, lambda c: step(qb - 1, c, "near"), lambda c: c, ms)
    step(qb, ms, "diag")

    outs = []
    for h in range(A_HEADS):
        a = acc_ref[h]
        o_lat = (a[0:A_LATENT] / a[A_LATENT:A_LATENT + 1]).astype(BF16)
        outs.append(jnp.dot(wuvt_ref[h], o_lat, preferred_element_type=F32))
    o_ref[...] = jnp.concatenate(outs, axis=0).T.astype(o_ref.dtype)


def _dsa_attention(rel_bias, qat, qit, wit, kw, ckv, ckvt, wuk, wuvt, tiles, batch, seq):
    tb = ATT_BLOCK
    nq = seq // tb
    topk = min(TOPK_MAX, seq // 4)
    nqa = A_HEADS * HEAD_DIM
    nqi = IDX_HEADS * IDX_DIM
    aug = A_LATENT + BF16_ROWS
    return pl.pallas_call(
        functools.partial(_dsa_kernel, tb=tb, topk=topk, nq=nq),
        grid=(batch, nq),
        in_specs=[pl.BlockSpec(memory_space=pltpu.SMEM),
                  pl.BlockSpec((1, nqa, tb), lambda b, i: (b * nq + i, 0, 0)),
                  pl.BlockSpec((1, nqi, tb), lambda b, i: (b * nq + i, 0, 0)),
                  pl.BlockSpec((1, LANES, tb), lambda b, i: (b * nq + i, 0, 0)),
                  pl.BlockSpec((seq, LANES), lambda b, i: (b, 0)),
                  pl.BlockSpec((seq, A_LATENT), lambda b, i: (b, 0)),
                  pl.BlockSpec((nq, A_LATENT, tb), lambda b, i: (b, 0, 0)),
                  pl.BlockSpec(wuk.shape, lambda b, i: (0, 0, 0)),
                  pl.BlockSpec(wuvt.shape, lambda b, i: (0, 0, 0)),
                  pl.BlockSpec((A_HEADS, 2, tb, tb), lambda b, i: (0, 0, 0, 0))],
        out_specs=pl.BlockSpec((tb, nqa), lambda b, i: (b * nq + i, 0)),
        out_shape=jax.ShapeDtypeStruct((batch * seq, nqa), BF16),
        scratch_shapes=[pltpu.VMEM((nq, tb, tb), I32), pltpu.VMEM((nq, tb, tb), F32),
                        pltpu.VMEM((A_HEADS, A_LATENT, tb), BF16),
                        pltpu.VMEM((nq, aug, tb), BF16),
                        pltpu.VMEM((A_HEADS, aug, tb), F32)],
        compiler_params=_params(("arbitrary", "arbitrary")), name="dsa_attention",
    )(rel_bias, qat, qit, wit, kw, ckv, ckvt, wuk, wuvt, tiles)


RANK_BASE = 1e30
RANK_STEP = 1e28


def _rank_mark(r):
    return -(RANK_BASE + r * RANK_STEP)


def _top_sorted(x, k):
    rows = lax.broadcasted_iota(I32, (k, x.shape[1]), 0)
    out = jnp.zeros((k, x.shape[1]), F32)
    for r in range(k):
        m = jnp.max(x, axis=0, keepdims=True)
        out = jnp.where(rows == r, m, out)
        x = jnp.where(x == m, _rank_mark(r), x)
    return out, x


def _router_kernel(q_ref, keys_ref, c1_ref, a_ref, r2_ref, w_ref, *, tr):
    k = PEER_TOPK
    nk = keys_ref.shape[1]
    row8 = lax.broadcasted_iota(I32, (8, LANES), 0)
    for h in range(PEER_HEADS):
        for tc in range(tr // LANES):
            tok = slice(tc * LANES, (tc + 1) * LANES)
            qh = q_ref[tok, :]
            s1 = lax.dot_general(keys_ref[0], qh[:, (2 * h) * nk:(2 * h + 1) * nk], _NT,
                                 preferred_element_type=F32)
            s2 = lax.dot_general(keys_ref[1], qh[:, (2 * h + 1) * nk:(2 * h + 2) * nk], _NT,
                                 preferred_element_type=F32)
            a, marked1 = _top_sorted(s1, k)
            b, marked2 = _top_sorted(s2, k)
            b8 = b[0:8]
            parts = [a[0:1] + b, a[1:2] + b8]
            for i, lim in ((2, 5), (3, 4), (4, 3), (5, 2), (6, 2), (7, 2)):
                parts.append(jnp.where(row8 < lim, a[i:i + 1] + b8, -jnp.inf))
            parts.append(a[8:16] + b[0:1])
            cand = jnp.concatenate(parts, axis=0)
            x = cand
            thr = None
            for _ in range(k):
                thr = jnp.max(x, axis=0, keepdims=True)
                x = jnp.where(x == thr, -jnp.inf, x)
            mx = a[0:1] + b[0:1]
            z = jnp.sum(jnp.where(cand >= thr, jnp.exp(cand - mx), 0.0), axis=0, keepdims=True)
            c1 = jnp.zeros((nk, LANES), F32)
            for r in range(k):
                cnt = jnp.sum(jnp.where(a[r:r + 1] + b >= thr, 1.0, 0.0), axis=0, keepdims=True)
                c1 = jnp.where(marked1 == _rank_mark(r), cnt, c1)
            rank2 = jnp.where(marked2 <= -RANK_BASE,
                              jnp.floor((-marked2 - RANK_BASE) * (1.0 / RANK_STEP) + 0.5), 127.0)
            c1_ref[h, :, tok] = c1
            a_ref[h, :, tok] = jnp.exp(s1 - a[0:1]) / z
            r2_ref[h, :, tok] = rank2.astype(BF16)
            w_ref[h, :, tok] = jnp.exp(s2 - b[0:1]).astype(BF16)


def _router(q, sub_keys):
    t = q.shape[0]
    tr = ROUTER_TILE
    nk = sub_keys.shape[1]
    shp32 = jax.ShapeDtypeStruct((PEER_HEADS, nk, t), F32)
    shp16 = jax.ShapeDtypeStruct((PEER_HEADS, nk, t), BF16)
    spec = pl.BlockSpec((PEER_HEADS, nk, tr), lambda i: (0, 0, i))
    return pl.pallas_call(
        functools.partial(_router_kernel, tr=tr),
        grid=(t // tr,),
        in_specs=[pl.BlockSpec((tr, q.shape[1]), lambda i: (i, 0)),
                  pl.BlockSpec(sub_keys.shape, lambda i: (0, 0, 0))],
        out_specs=[spec] * 4, out_shape=[shp32, shp32, shp16, shp16],
        compiler_params=_params(("arbitrary",)), name="peer_router",
    )(q, sub_keys)


def _peer_kernel(h_ref, u_ref, vt_ref, c1_ref, a_ref, r2_ref, w_ref, x_ref, g_ref, *rest,
                 tm, te, nk, ne, final):
    if final:
        gf_ref, o_ref, act0_ref, act1_ref, p_ref, acc_ref, r2s_ref, ws_ref = rest
    else:
        o_ref, act0_ref, act1_ref, p_ref, acc_ref, r2s_ref, ws_ref = rest
    s = pl.program_id(0)
    e0 = jnp.maximum(s - 1, 0) % ne
    nsub = nk // BF16_ROWS

    @pl.when(s == 0)
    def _():
        act1_ref[...] = jnp.zeros_like(act1_ref)

    @pl.when(e0 == 0)
    def _():
        acc_ref[...] = jnp.zeros_like(acc_ref)
        r2s_ref[...] = r2_ref[...]
        ws_ref[...] = w_ref[...]

    def main(cur_ref, prev_ref):
        tile = 2 * LANES

        def gate_piece(ii_list, tc_list):
            for tc in tc_list:
                tok = slice(tc * LANES, (tc + 1) * LANES)
                gates = [[None] * nsub for _ in ii_list]
                for h in range(PEER_HEADS):
                    r2 = [r2s_ref[h, k * BF16_ROWS:(k + 1) * BF16_ROWS, tok] for k in range(nsub)]
                    w2 = [ws_ref[h, k * BF16_ROWS:(k + 1) * BF16_ROWS, tok] for k in range(nsub)]
                    for n, ii in enumerate(ii_list):
                        c = jnp.broadcast_to(c1_ref[h, ii:ii + 1, tok], (BF16_ROWS, LANES)).astype(BF16)
                        a = jnp.broadcast_to(a_ref[h, ii:ii + 1, tok], (BF16_ROWS, LANES)).astype(BF16)
                        for k in range(nsub):
                            t = jnp.where(r2[k] < c, w2[k] * a, 0)
                            gates[n][k] = t if gates[n][k] is None else gates[n][k] + t
                for n, ii in enumerate(ii_list):
                    for k in range(nsub):
                        rows = slice(ii * nk + k * BF16_ROWS, ii * nk + (k + 1) * BF16_ROWS)
                        p_ref[rows, tok] = gates[n][k] * jax.nn.gelu(prev_ref[rows, tok])

        ii_per = tile // nk
        tc_per = tile // LANES
        for nt in range(tm // tile):
            cols = slice(nt * tile, (nt + 1) * tile)
            for kt in range(te // tile):
                rows = slice(kt * tile, (kt + 1) * tile)
                gate_piece(list(range(kt * ii_per, (kt + 1) * ii_per)),
                           list(range(nt * tc_per, (nt + 1) * tc_per)))
                acc_ref[:, cols] += jnp.dot(vt_ref[:, rows], p_ref[rows, cols], preferred_element_type=F32)
                if kt % 2 == 1:
                    urows = slice((kt // 2) * 2 * tile, (kt // 2 + 1) * 2 * tile)
                    cur_ref[urows, cols] = lax.dot_general(u_ref[urows, :], h_ref[cols, :], _NT,
                                                           preferred_element_type=F32).astype(BF16)

    @pl.when(s % 2 == 0)
    def _():
        main(act0_ref, act1_ref)

    @pl.when(s % 2 == 1)
    def _():
        main(act1_ref, act0_ref)

    @pl.when((e0 == ne - 1) & (s > 0))
    def _():
        xn = x_ref[...] + g_ref[0] * acc_ref[...].T
        if final:
            xn = _rms(xn, gf_ref[...])
        o_ref[...] = xn


def _peer(h, u, vt, layer, c1, a, r2, w, x, g2, seq, final_gain=None):
    t, d = x.shape
    tm, te = PEER_TOKEN_TILE, PEER_EXPERT_TILE
    ne = u.shape[1] // te
    nk = r2.shape[1]
    rows_per = te // nk
    per_b = seq // tm
    final = final_gain is not None
    steps = (t // tm) * ne + 1

    def cur(s):
        s1 = jnp.minimum(s, steps - 2)
        return s1 // ne, s1 % ne

    def prev(s):
        s0 = jnp.maximum(s - 1, 0)
        return s0 // ne, s0 % ne

    in_specs = [pl.BlockSpec((tm, d), lambda s: (cur(s)[0], 0)),
                pl.BlockSpec((None, te, d), lambda s: (layer, cur(s)[1], 0)),
                pl.BlockSpec((None, d, te), lambda s: (layer, 0, prev(s)[1])),
                pl.BlockSpec((PEER_HEADS, rows_per, tm), lambda s: (0, prev(s)[1], prev(s)[0])),
                pl.BlockSpec((PEER_HEADS, rows_per, tm), lambda s: (0, prev(s)[1], prev(s)[0])),
                pl.BlockSpec((PEER_HEADS, nk, tm), lambda s: (0, 0, prev(s)[0])),
                pl.BlockSpec((PEER_HEADS, nk, tm), lambda s: (0, 0, prev(s)[0])),
                pl.BlockSpec((tm, d), lambda s: (prev(s)[0], 0)),
                pl.BlockSpec((1, 1, d), lambda s: (prev(s)[0] // per_b, 0, 0))]
    args = [h, u, vt, c1, a, r2, w, x, g2]
    if final:
        in_specs.append(pl.BlockSpec((1, d), lambda s: (0, 0)))
        args.append(final_gain.reshape(1, d))
    return pl.pallas_call(
        functools.partial(_peer_kernel, tm=tm, te=te, nk=nk, ne=ne, final=final),
        grid=(steps,),
        in_specs=in_specs,
        out_specs=pl.BlockSpec((tm, d), lambda s: (prev(s)[0], 0)),
        out_shape=jax.ShapeDtypeStruct((t, d), F32),
        scratch_shapes=[pltpu.VMEM((te, tm), BF16), pltpu.VMEM((te, tm), BF16),
                        pltpu.VMEM((te, tm), BF16), pltpu.VMEM((d, tm), F32),
                        pltpu.VMEM((PEER_HEADS, nk, tm), BF16), pltpu.VMEM((PEER_HEADS, nk, tm), BF16)],
        compiler_params=_params(("arbitrary",)), name="peer_experts",
    )(*args)


def _pad_cols(w, n):
    return jnp.pad(w, ((0, 0), (0, n - w.shape[1])))


def _even_w_in(w):
    na, ni, nb = A_HEADS * HEAD_DIM, IDX_HEADS * IDX_DIM, B_HEADS * 2 * HEAD_DIM
    o = np.cumsum([0, na, A_LATENT, ni, IDX_DIM, IDX_HEADS, nb, nb, nb])
    kw = _pad_cols(w[:, o[3]:o[5]], LANES)
    return jnp.concatenate([w[:, o[0]:o[3]], kw, w[:, o[5]:o[8]]], axis=1).astype(BF16)


def kernel(x, c, rel_bias, ada_w, ada_b, norm_mix, norm_ffn, norm_final, even_w_in, even_w_out,
           a_kv_norm, a_w_uk, a_w_uv, b_lambda, b_subln, odd_w_in, odd_b_forget, odd_w_out,
           peer_w_q, peer_sub_keys, peer_u, peer_v):
    batch, seq, d = x.shape
    depth = ada_w.shape[0]
    t = batch * seq
    assert seq % ROW_TILE == 0 and seq % ATT_BLOCK == 0 and t % PEER_TOKEN_TILE == 0

    mod = _ada(c, ada_w, ada_b)
    tiles = _bias_tiles(rel_bias)

    na, ni, nb = A_HEADS * HEAD_DIM, IDX_HEADS * IDX_DIM, B_HEADS * 2 * HEAD_DIM
    nc = C_HEADS * HEAD_DIM
    o = np.cumsum([0, na, A_LATENT, ni, LANES, nb, nb, nb])
    even_segs = [(o[0], o[1], "Tq", [BF16]), (o[1], o[2], "kvnorm", [BF16]), (o[2], o[3], "Tq", [BF16]),
                 (o[3], o[4], None, [BF16]), (o[3], o[4], "T", [F32]), (o[4], o[5], "Tq", [BF16]),
                 (o[5], o[6], None, [BF16]), (o[6], o[7], "T", [BF16])]
    odd_segs = [(0, nc, "Tq", [BF16]), (nc, 2 * nc, None, [BF16]), (2 * nc, 3 * nc, "T", [BF16]),
                (3 * nc, 3 * nc + LANES, None, [F32])]
    nq_peer = peer_w_q.shape[2]
    peer_segs = [(0, nq_peer, None, [BF16])]

    u_all = peer_u.astype(BF16)
    vt_all = jnp.transpose(peer_v, (0, 2, 1)).astype(BF16)
    xt = x.reshape(t, d)
    for layer in range(depth):
        m6 = mod[layer].reshape(batch, 6, 1, d)
        sh1, sc1, g1, sh2, sc2, g2 = [m6[:, i] for i in range(6)]
        if layer % 2 == 0:
            e = layer // 2
            lam_init = 0.8 - 0.6 * math.exp(-0.3 * layer)
            w_in = _even_w_in(even_w_in[e])
            qat, ckv, ckvt, qit, kw, wit, qbt, k_b, vbt = _normproj(
                xt, norm_mix[layer], sc1, sh1, w_in, even_segs, seq, kvn=a_kv_norm[e], wt=w_in.T,
                name="even_in_proj")
            wuk = jnp.transpose(a_w_uk[e], (1, 0, 2)).astype(BF16)
            wuvt = jnp.transpose(a_w_uv[e], (1, 2, 0)).astype(BF16)
            o_a = _dsa_attention(rel_bias, qat, qit, wit, kw, ckv, ckvt, wuk, wuvt, tiles, batch, seq)
            o_b = _diff_attention(rel_bias, qbt, k_b, vbt, tiles, b_lambda[e], b_subln[e], batch, seq, lam_init)
            mixes = [o_a, o_b]
            w_out = even_w_out[e].astype(BF16)
        else:
            od = layer // 2
            w_in = _pad_cols(odd_w_in[od], 3 * nc + LANES).astype(BF16)
            qt, k, vt, f = _normproj(xt, norm_mix[layer], sc1, sh1, w_in, odd_segs, seq, wt=w_in.T,
                                     name="odd_in_proj")
            kf, qft = _foxgate(f, odd_b_forget[od], batch, seq)
            mixes = [_fox_attention(qt, qft, k, kf, vt, batch, seq)]
            w_out = odd_w_out[od].astype(BF16)
        xt, h2, qp = _resproj(xt, mixes, w_out, g1, norm_ffn[layer], sc2, sh2,
                              peer_w_q[layer].astype(BF16), peer_segs, seq)
        c1, a, r2, w = _router(qp, peer_sub_keys[layer].astype(BF16))
        xt = _peer(h2, u_all, vt_all, layer, c1, a, r2, w,
                   xt, g2, seq, final_gain=norm_final if layer == depth - 1 else None)
    return xt.reshape(batch, seq, d)
```

```python
import functools
import math

import numpy as np
import jax
import jax.numpy as jnp
from jax import lax
from jax.experimental import pallas as pl
from jax.experimental.pallas import tpu as pltpu

F32 = jnp.float32
BF16 = jnp.bfloat16
I32 = jnp.int32

HEAD_DIM = 64
RMS_EPS = 1e-6
NEG_INF = -1e30
A_HEADS = 8
A_LATENT = 256
IDX_HEADS = 8
IDX_DIM = 64
TOPK_MAX = 256
B_HEADS = 4
C_HEADS = 16
REL_BUCKETS = 32
REL_MAX_DIST = 128
PEER_HEADS = 8
PEER_TOPK = 16

LANES = 128
BF16_ROWS = 16
ATT_BLOCK = 256
ROW_TILE = 512
PEER_TOKEN_TILE = 512
PEER_EXPERT_TILE = 2048
ROUTER_TILE = 512
FOX_GROUP = 8
DIFF_GROUP = 4
F_PIECES = 3
VMEM_LIMIT = 56 * 1024 * 1024
INT_MIN = -2 ** 31

_NT = (((1,), (1,)), ((), ()))


def _params(sem):
    return pltpu.CompilerParams(dimension_semantics=sem, vmem_limit_bytes=VMEM_LIMIT)


def _rms(x, g):
    return x * lax.rsqrt(jnp.mean(x * x, axis=-1, keepdims=True) + RMS_EPS) * g


def _ones_row(rows, cols):
    return jnp.where(lax.broadcasted_iota(I32, (rows, cols), 0) == 0, 1.0, 0.0).astype(BF16)


def _key_le_query(tb):
    return lax.broadcasted_iota(I32, (tb, tb), 0) <= lax.broadcasted_iota(I32, (tb, tb), 1)


def _ada_kernel(c_ref, w_ref, b_ref, o_ref):
    c = c_ref[...]
    ca = c * jax.nn.sigmoid(c)
    o_ref[0] = jnp.dot(ca, w_ref[0], precision=lax.Precision.HIGHEST,
                       preferred_element_type=F32) + b_ref[0]


def _ada(c, ada_w, ada_b):
    depth, d, n = ada_w.shape
    b = c.shape[0]
    bp = 8
    cp = jnp.zeros((bp, d), F32).at[:b].set(c)
    tn = 1536
    out = pl.pallas_call(
        _ada_kernel,
        grid=(depth, n // tn),
        in_specs=[pl.BlockSpec((bp, d), lambda l, j: (0, 0)),
                  pl.BlockSpec((1, d, tn), lambda l, j: (l, 0, j)),
                  pl.BlockSpec((1, 1, tn), lambda l, j: (l, 0, j))],
        out_specs=pl.BlockSpec((1, bp, tn), lambda l, j: (l, 0, j)),
        out_shape=jax.ShapeDtypeStruct((depth, bp, n), F32),
        compiler_params=_params(("arbitrary", "arbitrary")),
        name="ada_mod",
    )(cp, ada_w, ada_b.reshape(depth, 1, n))
    return out[:, :b]


def _store_transposed(out_ref, rt, dt):
    for c in range(rt.shape[1] // ATT_BLOCK):
        out_ref[c] = rt[:, c * ATT_BLOCK:(c + 1) * ATT_BLOCK].astype(dt)


def _emit_segments(hb, w_ref, wt_ref, segs, outs, kvn_ref):
    k = 0
    for (a, b, kind, dtypes) in segs:
        if kind in ("T", "Tq"):
            r = lax.dot_general(wt_ref[a:b, :], hb, _NT, preferred_element_type=F32)
            if kind == "Tq":
                r = r * (HEAD_DIM ** -0.5)
            for dt in dtypes:
                _store_transposed(outs[k], r, dt)
                k += 1
            continue
        r = jnp.dot(hb, w_ref[:, a:b], preferred_element_type=F32)
        if kind == "kvnorm":
            r = _rms(r, kvn_ref[...])
            outs[k][...] = r.astype(dtypes[0])
            _store_transposed(outs[k + 1], r.T, dtypes[0])
            k += 2
            continue
        for dt in dtypes:
            outs[k][...] = r.astype(dt)
            k += 1


def _normproj_kernel(*refs, segs, has_kvn, has_wt):
    x_ref, gain_ref, sc_ref, sh_ref, w_ref = refs[:5]
    pos = 5
    wt_ref = kvn_ref = None
    if has_wt:
        wt_ref = refs[pos]
        pos += 1
    if has_kvn:
        kvn_ref = refs[pos]
        pos += 1
    outs = list(refs[pos:])
    h = _rms(x_ref[...], gain_ref[...]) * (1.0 + sc_ref[0]) + sh_ref[0]
    _emit_segments(h.astype(BF16), w_ref, wt_ref, segs, outs, kvn_ref)


def _resproj_kernel(*refs, n_mix, segs):
    x_ref = refs[0]
    mix_refs = refs[1:1 + n_mix]
    wo_ref, g_ref, gain_ref, sc_ref, sh_ref, w_ref = refs[1 + n_mix:7 + n_mix]
    outs = list(refs[7 + n_mix:])
    y = None
    off = 0
    for m in mix_refs:
        kdim = m.shape[1]
        t = jnp.dot(m[...], wo_ref[off:off + kdim, :], preferred_element_type=F32)
        y = t if y is None else y + t
        off += kdim
    xn = x_ref[...] + g_ref[0] * y
    outs[0][...] = xn
    h = _rms(xn, gain_ref[...]) * (1.0 + sc_ref[0]) + sh_ref[0]
    hb = h.astype(BF16)
    outs[1][...] = hb
    _emit_segments(hb, w_ref, None, segs, outs[2:], None)


def _seg_out_shapes(t, segs):
    shapes, specs = [], []
    per = ROW_TILE // ATT_BLOCK

    def plain(n, dt):
        shapes.append(jax.ShapeDtypeStruct((t, n), dt))
        specs.append(pl.BlockSpec((ROW_TILE, n), lambda i: (i, 0)))

    def transposed(n, dt):
        shapes.append(jax.ShapeDtypeStruct((t // ATT_BLOCK, n, ATT_BLOCK), dt))
        specs.append(pl.BlockSpec((per, n, ATT_BLOCK), lambda i: (i, 0, 0)))

    for (a, b, kind, dtypes) in segs:
        if kind == "kvnorm":
            plain(b - a, dtypes[0])
            transposed(b - a, dtypes[0])
            continue
        for dt in dtypes:
            (transposed if kind in ("T", "Tq") else plain)(b - a, dt)
    return shapes, specs


def _normproj(x, gain, sc, sh, w, segs, seq, kvn=None, wt=None, name="normproj"):
    t, d = x.shape
    tm = ROW_TILE
    per_b = seq // tm
    n = w.shape[1]
    mod_spec = pl.BlockSpec((1, 1, d), lambda i: (i // per_b, 0, 0))
    in_specs = [pl.BlockSpec((tm, d), lambda i: (i, 0)),
                pl.BlockSpec((1, d), lambda i: (0, 0)), mod_spec, mod_spec,
                pl.BlockSpec((d, n), lambda i: (0, 0))]
    args = [x, gain.reshape(1, d), sc, sh, w]
    if wt is not None:
        in_specs.append(pl.BlockSpec((n, d), lambda i: (0, 0)))
        args.append(wt)
    if kvn is not None:
        in_specs.append(pl.BlockSpec((1, kvn.shape[-1]), lambda i: (0, 0)))
        args.append(kvn.reshape(1, -1))
    shapes, specs = _seg_out_shapes(t, segs)
    return pl.pallas_call(
        functools.partial(_normproj_kernel, segs=segs, has_kvn=kvn is not None, has_wt=wt is not None),
        grid=(t // tm,), in_specs=in_specs, out_specs=specs, out_shape=shapes,
        compiler_params=_params(("arbitrary",)), name=name,
    )(*args)


def _resproj(x, mixes, w_out, g, gain, sc, sh, w, segs, seq, name="resproj"):
    t, d = x.shape
    tm = ROW_TILE
    per_b = seq // tm
    n = w.shape[1]
    mod_spec = pl.BlockSpec((1, 1, d), lambda i: (i // per_b, 0, 0))
    in_specs = [pl.BlockSpec((tm, d), lambda i: (i, 0))]
    in_specs += [pl.BlockSpec((tm, m.shape[1]), lambda i: (i, 0)) for m in mixes]
    in_specs += [pl.BlockSpec(w_out.shape, lambda i: (0, 0)), mod_spec,
                 pl.BlockSpec((1, d), lambda i: (0, 0)), mod_spec, mod_spec,
                 pl.BlockSpec((d, n), lambda i: (0, 0))]
    shapes, specs = _seg_out_shapes(t, segs)
    shapes = [jax.ShapeDtypeStruct((t, d), F32), jax.ShapeDtypeStruct((t, d), BF16)] + shapes
    specs = [pl.BlockSpec((tm, d), lambda i: (i, 0)), pl.BlockSpec((tm, d), lambda i: (i, 0))] + specs
    return pl.pallas_call(
        functools.partial(_resproj_kernel, n_mix=len(mixes), segs=segs),
        grid=(t // tm,), in_specs=in_specs, out_specs=specs, out_shape=shapes,
        compiler_params=_params(("arbitrary",)), name=name,
    )(x, *mixes, w_out, g, gain.reshape(1, d), sc, sh, w)


def _bucket_table(n):
    max_exact = REL_BUCKETS // 2
    d = np.arange(n)
    df = np.maximum(d, 1).astype(np.float32)
    large = max_exact + (np.log(df / max_exact) / math.log(REL_MAX_DIST / max_exact)
                         * (REL_BUCKETS - max_exact)).astype(np.int32)
    large = np.minimum(large, REL_BUCKETS - 1)
    return np.where(d < max_exact, d, large).astype(np.int32)


def _bias_tiles_kernel(rb_ref, bk_ref, o_ref):
    h = pl.program_id(0)
    for t in range(2):
        bt = bk_ref[t]
        acc = jnp.zeros(bt.shape, F32)
        for b in range(REL_BUCKETS):
            acc = jnp.where(bt == b, rb_ref[b, h], acc)
        o_ref[0, t] = acc


def _bias_tiles(rel_bias):
    tb = ATT_BLOCK
    nh = rel_bias.shape[1]
    table = _bucket_table(2 * tb)
    s = np.arange(tb)[:, None]
    t = np.arange(tb)[None, :]
    bk = np.stack([table[np.maximum(t - s, 0)], table[tb + t - s]]).astype(np.int32)
    return pl.pallas_call(
        _bias_tiles_kernel,
        grid=(nh,),
        in_specs=[pl.BlockSpec(memory_space=pltpu.SMEM),
                  pl.BlockSpec((2, tb, tb), lambda h: (0, 0, 0))],
        out_specs=pl.BlockSpec((1, 2, tb, tb), lambda h: (h, 0, 0, 0)),
        out_shape=jax.ShapeDtypeStruct((nh, 2, tb, tb), F32),
        compiler_params=_params(("arbitrary",)), name="bias_tiles",
    )(rel_bias, jnp.asarray(bk))


def _flash_update(s, m, shift_const, acc_old, vaug):
    m_new = jnp.maximum(m, jnp.max(s, axis=0, keepdims=True) + shift_const)
    p = jnp.exp(s - (m_new - shift_const)).astype(BF16)
    acc = jnp.exp(m - m_new) * acc_old + jnp.dot(vaug, p, preferred_element_type=F32)
    return m_new, acc


def _split_bf16(x):
    pieces = []
    for _ in range(F_PIECES):
        p = x.astype(BF16)
        pieces.append(p)
        x = x - p.astype(F32)
    return pieces


def _fox_sel():
    selk = np.zeros((F_PIECES, LANES, C_HEADS * HEAD_DIM), np.float32)
    selq = np.zeros((F_PIECES, C_HEADS * HEAD_DIM, LANES), np.float32)
    onesk = np.zeros((1, C_HEADS * HEAD_DIM), np.float32)
    onesq = np.zeros((C_HEADS * HEAD_DIM, 1), np.float32)
    for h in range(C_HEADS):
        for p in range(F_PIECES):
            selk[p, h, h * HEAD_DIM + p] = -1.0
            selq[p, h * HEAD_DIM + F_PIECES + p, h] = 1.0
            onesk[0, h * HEAD_DIM + F_PIECES + p] = 1.0
            onesq[h * HEAD_DIM + p, 0] = 1.0
    return selk, selq, onesk, onesq


def _foxgate_kernel(f_ref, b_ref, selk_ref, selq_ref, onesk_ref, onesq_ref, kf_ref, qft_ref, *, tb):
    nblk = f_ref.shape[0] // tb
    row = lax.broadcasted_iota(I32, (tb, tb), 0)
    col = lax.broadcasted_iota(I32, (tb, tb), 1)
    tri = jnp.where(col <= row, 1.0, 0.0).astype(F32)

    def body(j, carry):
        z = f_ref[pl.ds(j * tb, tb), :] + b_ref[...]
        ls = -(jnp.maximum(-z, 0.0) + jnp.log(1.0 + jnp.exp(-jnp.abs(z))))
        cs = jnp.dot(tri, ls, precision=lax.Precision.HIGHEST, preferred_element_type=F32) + carry
        kf = onesk_ref[...]
        for p, piece in enumerate(_split_bf16(cs)):
            kf = kf + jnp.dot(piece, selk_ref[p], preferred_element_type=F32)
        kf_ref[pl.ds(j * tb, tb), :] = kf.astype(BF16)
        qf = onesq_ref[...]
        for p, piece in enumerate(_split_bf16(cs.T)):
            qf = qf + jnp.dot(selq_ref[p], piece, preferred_element_type=F32)
        qft_ref[j] = qf.astype(BF16)
        return cs[tb - 1:tb, :]

    lax.fori_loop(0, nblk, body, jnp.zeros((1, LANES), F32))


def _foxgate(f, b_forget, batch, seq):
    tb = ATT_BLOCK
    nblk = seq // tb
    n = C_HEADS * HEAD_DIM
    bpad = jnp.zeros((1, LANES), F32).at[0, :C_HEADS].set(b_forget)
    selk, selq, onesk, onesq = _fox_sel()
    return pl.pallas_call(
        functools.partial(_foxgate_kernel, tb=tb),
        grid=(batch,),
        in_specs=[pl.BlockSpec((seq, LANES), lambda b: (b, 0)),
                  pl.BlockSpec((1, LANES), lambda b: (0, 0)),
                  pl.BlockSpec(selk.shape, lambda b: (0, 0, 0)),
                  pl.BlockSpec(selq.shape, lambda b: (0, 0, 0)),
                  pl.BlockSpec(onesk.shape, lambda b: (0, 0)),
                  pl.BlockSpec(onesq.shape, lambda b: (0, 0))],
        out_specs=[pl.BlockSpec((seq, n), lambda b: (b, 0)),
                   pl.BlockSpec((nblk, n, tb), lambda b: (b, 0, 0))],
        out_shape=[jax.ShapeDtypeStruct((batch * seq, n), BF16),
                   jax.ShapeDtypeStruct((batch * nblk, n, tb), BF16)],
        compiler_params=_params(("arbitrary",)), name="fox_gates",
    )(f, bpad, jnp.asarray(selk, BF16), jnp.asarray(selq, BF16), jnp.asarray(onesk), jnp.asarray(onesq))


def _fox_kernel(qt_ref, qft_ref, k_ref, kf_ref, vt_ref, o_ref, kaug_ref, vaug_ref, *, tb, nq):
    qi = pl.program_id(2)
    hd = HEAD_DIM
    heads = range(FOX_GROUP)

    @pl.when(qi == 0)
    def _():
        ones_rows = _ones_row(hd, tb)
        for h in heads:
            kaug_ref[h, :, 0:hd] = k_ref[:, h * hd:(h + 1) * hd]
            kaug_ref[h, :, hd:2 * hd] = kf_ref[:, h * hd:(h + 1) * hd]

            def fill(j, c, h=h):
                vaug_ref[h, j, 0:hd, :] = vt_ref[j, h * hd:(h + 1) * hd, :]
                vaug_ref[h, j, hd:2 * hd, :] = ones_rows
                return c

            lax.fori_loop(0, nq, fill, 0)

    qa = [jnp.concatenate([qt_ref[0, h * hd:(h + 1) * hd, :], qft_ref[0, h * hd:(h + 1) * hd, :]], axis=0)
          for h in heads]
    keep = _key_le_query(tb)

    def step(j, carry, masked):
        scores = [jnp.dot(kaug_ref[h, pl.ds(j * tb, tb), :], qa[h], preferred_element_type=F32)
                  for h in heads]
        out = []
        for h in heads:
            m, acc = carry[h]
            s = jnp.where(keep, scores[h], NEG_INF) if masked else scores[h]
            out.append(_flash_update(s, m, 0.0, acc, vaug_ref[h, j]))
        return tuple(out)

    init = tuple((jnp.full((1, tb), NEG_INF, F32), jnp.zeros((2 * hd, tb), F32)) for _ in heads)

    def step2(i, carry):
        ja = 2 * i
        sa = [jnp.dot(kaug_ref[h, pl.ds(ja * tb, tb), :], qa[h], preferred_element_type=F32) for h in heads]
        sb = [jnp.dot(kaug_ref[h, pl.ds((ja + 1) * tb, tb), :], qa[h], preferred_element_type=F32) for h in heads]
        out = []
        for h in heads:
            m, acc = carry[h]
            m, acc = _flash_update(sa[h], m, 0.0, acc, vaug_ref[h, ja])
            out.append(_flash_update(sb[h], m, 0.0, acc, vaug_ref[h, ja + 1]))
        return tuple(out)

    carry = lax.fori_loop(0, qi // 2, step2, init)
    carry = lax.cond(qi % 2 == 1, lambda c: step(qi - 1, c, False), lambda c: c, carry)
    carry = step(qi, carry, True)
    for pair in range(FOX_GROUP // 2):
        ot = jnp.concatenate([carry[h][1][0:hd] / carry[h][1][hd:hd + 1] for h in (2 * pair, 2 * pair + 1)], axis=0)
        o_ref[:, pair * 2 * hd:(pair + 1) * 2 * hd] = ot.T.astype(o_ref.dtype)


def _fox_attention(qt, qft, k, kf, vt, batch, seq):
    tb = ATT_BLOCK
    nq = seq // tb
    groups = C_HEADS // FOX_GROUP
    gw = FOX_GROUP * HEAD_DIM
    return pl.pallas_call(
        functools.partial(_fox_kernel, tb=tb, nq=nq),
        grid=(batch, groups, nq),
        in_specs=[pl.BlockSpec((1, gw, tb), lambda b, g, i: (b * nq + i, g, 0)),
                  pl.BlockSpec((1, gw, tb), lambda b, g, i: (b * nq + i, g, 0)),
                  pl.BlockSpec((seq, gw), lambda b, g, i: (b, g)),
                  pl.BlockSpec((seq, gw), lambda b, g, i: (b, g)),
                  pl.BlockSpec((nq, gw, tb), lambda b, g, i: (b, g, 0))],
        out_specs=pl.BlockSpec((tb, gw), lambda b, g, i: (b * nq + i, g)),
        out_shape=jax.ShapeDtypeStruct((batch * seq, C_HEADS * HEAD_DIM), BF16),
        scratch_shapes=[pltpu.VMEM((FOX_GROUP, seq, 2 * HEAD_DIM), BF16),
                        pltpu.VMEM((FOX_GROUP, nq, 2 * HEAD_DIM, tb), BF16)],
        compiler_params=_params(("arbitrary", "arbitrary", "arbitrary")), name="fox_attention",
    )(qt, qft, k, kf, vt)


def _diff_kernel(rb_ref, qt_ref, k_ref, vt_ref, tiles_ref, lam_ref, g_ref, o_ref,
                 ks_ref, vaug_ref, acc_ref, *, tb, nq, lam_init):
    g = pl.program_id(1)
    qi = pl.program_id(2)
    hd = HEAD_DIM
    dv = 2 * hd
    chains = [(hh, s_) for hh in range(DIFF_GROUP) for s_ in range(2)]

    @pl.when(qi == 0)
    def _():
        ones_rows = _ones_row(BF16_ROWS, tb)
        for c, (hh, s_) in enumerate(chains):
            ks_ref[c] = k_ref[:, hh * dv + s_ * hd:hh * dv + (s_ + 1) * hd]
        for hh in range(DIFF_GROUP):
            def fill(j, c, hh=hh):
                vaug_ref[hh, j, 0:dv, :] = vt_ref[j, hh * dv:(hh + 1) * dv, :]
                vaug_ref[hh, j, dv:dv + BF16_ROWS, :] = ones_rows
                return c

            lax.fori_loop(0, nq, fill, 0)

    qs = [qt_ref[0, hh * dv + s_ * hd:hh * dv + (s_ + 1) * hd, :] for (hh, s_) in chains]
    cfar = [rb_ref[REL_BUCKETS - 1, A_HEADS + g * DIFF_GROUP + hh] for hh in range(DIFF_GROUP)]
    keep = _key_le_query(tb)
    acc_ref[...] = jnp.zeros_like(acc_ref)

    def step(j, ms, kind):
        scores = [jnp.dot(ks_ref[c, pl.ds(j * tb, tb), :], qs[c], preferred_element_type=F32)
                  for c in range(len(chains))]
        out = []
        for c, (hh, s_) in enumerate(chains):
            if kind == "far":
                s, shift = scores[c], cfar[hh]
            else:
                s, shift = scores[c] + tiles_ref[hh, 0 if kind == "diag" else 1], 0.0
                if kind == "diag":
                    s = jnp.where(keep, s, NEG_INF)
            m_new, acc = _flash_update(s, ms[c], shift, acc_ref[c], vaug_ref[hh, j])
            acc_ref[c] = acc
            out.append(m_new)
        return tuple(out)

    ms = tuple(jnp.full((1, tb), NEG_INF, F32) for _ in chains)
    ms = lax.fori_loop(0, jnp.maximum(qi - 1, 0), lambda j, c: step(j, c, "far"), ms)
    ms = lax.cond(qi >= 1, lambda c: step(qi - 1, c, "near"), lambda c: c, ms)
    step(qi, ms, "diag")

    lv = lam_ref[...]
    lam = (jnp.exp(jnp.sum(lv[0:1] * lv[1:2], axis=-1, keepdims=True))
           - jnp.exp(jnp.sum(lv[2:3] * lv[3:4], axis=-1, keepdims=True)) + lam_init)
    for hh in range(DIFF_GROUP):
        a1 = acc_ref[2 * hh]
        a2 = acc_ref[2 * hh + 1]
        ot = a1[0:dv] / a1[dv:dv + 1] - lam * (a2[0:dv] / a2[dv:dv + 1])
        o = _rms(ot.T, g_ref[...]) * (1.0 - lam_init)
        o_ref[:, hh * dv:(hh + 1) * dv] = o.astype(o_ref.dtype)


def _diff_attention(rel_bias, qt, k, vt, tiles, lam_vec, subln, batch, seq, lam_init):
    tb = ATT_BLOCK
    nq = seq // tb
    groups = B_HEADS // DIFF_GROUP
    gw = DIFF_GROUP * 2 * HEAD_DIM
    dv = 2 * HEAD_DIM
    return pl.pallas_call(
        functools.partial(_diff_kernel, tb=tb, nq=nq, lam_init=lam_init),
        grid=(batch, groups, nq),
        in_specs=[pl.BlockSpec(memory_space=pltpu.SMEM),
                  pl.BlockSpec((1, gw, tb), lambda b, g, i: (b * nq + i, g, 0)),
                  pl.BlockSpec((seq, gw), lambda b, g, i: (b, g)),
                  pl.BlockSpec((nq, gw, tb), lambda b, g, i: (b, g, 0)),
                  pl.BlockSpec((DIFF_GROUP, 2, tb, tb), lambda b, g, i: (A_HEADS // DIFF_GROUP + g, 0, 0, 0)),
                  pl.BlockSpec((4, HEAD_DIM), lambda b, g, i: (0, 0)),
                  pl.BlockSpec((1, dv), lambda b, g, i: (0, 0))],
        out_specs=pl.BlockSpec((tb, gw), lambda b, g, i: (b * nq + i, g)),
        out_shape=jax.ShapeDtypeStruct((batch * seq, B_HEADS * dv), BF16),
        scratch_shapes=[pltpu.VMEM((2 * DIFF_GROUP, seq, HEAD_DIM), BF16),
                        pltpu.VMEM((DIFF_GROUP, nq, dv + BF16_ROWS, tb), BF16),
                        pltpu.VMEM((2 * DIFF_GROUP, dv + BF16_ROWS, tb), F32)],
        compiler_params=_params(("arbitrary", "arbitrary", "arbitrary")), name="diff_attention",
    )(rel_bias, qt, k, vt, tiles, lam_vec, subln.reshape(1, -1))


def _dsa_kernel(rb_ref, qat_ref, qit_ref, wit_ref, kw_ref, ckv_ref, ckvt_ref, wuk_ref, wuvt_ref, tiles_ref,
                o_ref, keys_ref, selb_ref, qlat_ref, vaug_ref, acc_ref, *, tb, topk, nq):
    qb = pl.program_id(1)
    nblk = qb + 1
    hd = HEAD_DIM
    keep = _key_le_query(tb)

    @pl.when(qb == 0)
    def _():
        ones_rows = _ones_row(BF16_ROWS, tb)

        def fill(j, c):
            vaug_ref[j, 0:A_LATENT, :] = ckvt_ref[j]
            vaug_ref[j, A_LATENT:A_LATENT + BF16_ROWS, :] = ones_rows
            return c

        lax.fori_loop(0, nq, fill, 0)

    for h in range(A_HEADS):
        ql = jnp.dot(wuk_ref[h], qat_ref[0, h * hd:(h + 1) * hd, :], preferred_element_type=F32)
        qlat_ref[h] = ql.astype(BF16)

    wrows = [wit_ref[0, IDX_DIM + h:IDX_DIM + h + 1, :] * (IDX_HEADS ** -0.5) for h in range(IDX_HEADS)]
    zpad = jnp.zeros((LANES - IDX_DIM, tb), BF16)
    qi_pad = [jnp.concatenate([qit_ref[0, h * IDX_DIM:(h + 1) * IDX_DIM, :], zpad], axis=0)
              for h in range(IDX_HEADS)]

    def index_keys(j, masked):
        kb = kw_ref[pl.ds(j * tb, tb), :]
        isc = jnp.zeros((tb, tb), F32)
        for h in range(IDX_HEADS):
            li = jnp.dot(kb, qi_pad[h], preferred_element_type=F32)
            isc = isc + jnp.maximum(li, 0.0) * wrows[h]
        isc = jnp.where(isc == 0.0, 0.0, isc)
        bits = pltpu.bitcast(isc, I32)
        key = bits ^ ((bits >> 31) & 0x7FFFFFFF)
        if masked:
            key = jnp.where(keep, key, INT_MIN)
        keys_ref[j] = key

    def p1(j, c):
        index_keys(j, False)
        return c

    lax.fori_loop(0, qb, p1, 0)
    index_keys(qb, True)

    def count(pred):
        def body(j, acc):
            ind = jnp.where(pred(keys_ref[j]), 1.0, 0.0)
            return acc + jnp.sum(ind.reshape(tb // 8, 8, tb), axis=0)
        acc = lax.fori_loop(0, nblk, body, jnp.zeros((8, tb), F32))
        return jnp.sum(acc, axis=0, keepdims=True)

    kth = jnp.where(count(lambda k: k >= 0) >= topk, 0, INT_MIN).astype(I32)

    def bs(i, kth):
        cand = kth | lax.shift_left(jnp.int32(1), 30 - i)
        return jnp.where(count(lambda k: k >= cand) >= topk, cand, kth)

    kth = lax.fori_loop(0, 31, bs, kth)
    need = topk - count(lambda k: k > kth)

    lower = jnp.where(lax.broadcasted_iota(I32, (tb, tb), 1) <= lax.broadcasted_iota(I32, (tb, tb), 0),
                      1.0, 0.0).astype(BF16)

    def mask_block(j, seen, masked):
        key = keys_ref[j]
        eq = key == kth
        pre = jnp.dot(lower, jnp.where(eq, 1.0, 0.0).astype(BF16), preferred_element_type=F32)
        sel = (key > kth) | (eq & (pre + seen <= need))
        if masked:
            sel = sel & keep
        selb_ref[j] = jnp.where(sel, 0.0, NEG_INF)
        return seen + pre[tb - 1:tb, :]

    seen = lax.fori_loop(0, qb, lambda j, s: mask_block(j, s, False), jnp.zeros((1, tb), F32))
    mask_block(qb, seen, True)

    cfar = [rb_ref[REL_BUCKETS - 1, h] for h in range(A_HEADS)]
    acc_ref[...] = jnp.zeros_like(acc_ref)

    def step(j, ms, kind):
        kvb = ckv_ref[pl.ds(j * tb, tb), :]
        scores = [jnp.dot(kvb, qlat_ref[h], preferred_element_type=F32) for h in range(A_HEADS)]
        sb = selb_ref[j]
        out = []
        for h in range(A_HEADS):
            if kind == "far":
                s, shift = scores[h] + sb, cfar[h]
            else:
                s, shift = scores[h] + (tiles_ref[h, 0 if kind == "diag" else 1] + sb), 0.0
            m_new, acc = _flash_update(s, ms[h], shift, acc_ref[h], vaug_ref[j])
            acc_ref[h] = acc
            out.append(m_new)
        return tuple(out)

    ms = tuple(jnp.full((1, tb), NEG_INF, F32) for _ in range(A_HEADS))
    ms = lax.fori_loop(0, jnp.maximum(qb - 1, 0), lambda j, c: step(j, c, "far"), ms)
    ms = lax.cond(qb >= 1, lambda c: step(qb - 1, c, "near"), lambda c: c, ms)
    step(qb, ms, "diag")

    outs = []
    for h in range(A_HEADS):
        a = acc_ref[h]
        o_lat = (a[0:A_LATENT] / a[A_LATENT:A_LATENT + 1]).astype(BF16)
        outs.append(jnp.dot(wuvt_ref[h], o_lat, preferred_element_type=F32))
    o_ref[...] = jnp.concatenate(outs, axis=0).T.astype(o_ref.dtype)


def _dsa_attention(rel_bias, qat, qit, wit, kw, ckv, ckvt, wuk, wuvt, tiles, batch, seq):
    tb = ATT_BLOCK
    nq = seq // tb
    topk = min(TOPK_MAX, seq // 4)
    nqa = A_HEADS * HEAD_DIM
    nqi = IDX_HEADS * IDX_DIM
    aug = A_LATENT + BF16_ROWS
    return pl.pallas_call(
        functools.partial(_dsa_kernel, tb=tb, topk=topk, nq=nq),
        grid=(batch, nq),
        in_specs=[pl.BlockSpec(memory_space=pltpu.SMEM),
                  pl.BlockSpec((1, nqa, tb), lambda b, i: (b * nq + i, 0, 0)),
                  pl.BlockSpec((1, nqi, tb), lambda b, i: (b * nq + i, 0, 0)),
                  pl.BlockSpec((1, LANES, tb), lambda b, i: (b * nq + i, 0, 0)),
                  pl.BlockSpec((seq, LANES), lambda b, i: (b, 0)),
                  pl.BlockSpec((seq, A_LATENT), lambda b, i: (b, 0)),
                  pl.BlockSpec((nq, A_LATENT, tb), lambda b, i: (b, 0, 0)),
                  pl.BlockSpec(wuk.shape, lambda b, i: (0, 0, 0)),
                  pl.BlockSpec(wuvt.shape, lambda b, i: (0, 0, 0)),
                  pl.BlockSpec((A_HEADS, 2, tb, tb), lambda b, i: (0, 0, 0, 0))],
        out_specs=pl.BlockSpec((tb, nqa), lambda b, i: (b * nq + i, 0)),
        out_shape=jax.ShapeDtypeStruct((batch * seq, nqa), BF16),
        scratch_shapes=[pltpu.VMEM((nq, tb, tb), I32), pltpu.VMEM((nq, tb, tb), F32),
                        pltpu.VMEM((A_HEADS, A_LATENT, tb), BF16),
                        pltpu.VMEM((nq, aug, tb), BF16),
                        pltpu.VMEM((A_HEADS, aug, tb), F32)],
        compiler_params=_params(("arbitrary", "arbitrary")), name="dsa_attention",
    )(rel_bias, qat, qit, wit, kw, ckv, ckvt, wuk, wuvt, tiles)


RANK_BASE = 1e30
RANK_STEP = 1e28


def _rank_mark(r):
    return -(RANK_BASE + r * RANK_STEP)


def _top_sorted(x, k):
    rows = lax.broadcasted_iota(I32, (k, x.shape[1]), 0)
    out = jnp.zeros((k, x.shape[1]), F32)
    for r in range(k):
        m = jnp.max(x, axis=0, keepdims=True)
        out = jnp.where(rows == r, m, out)
        x = jnp.where(x == m, _rank_mark(r), x)
    return out, x


def _router_kernel(q_ref, keys_ref, c1_ref, a_ref, r2_ref, w_ref, *, tr):
    k = PEER_TOPK
    nk = keys_ref.shape[1]
    row8 = lax.broadcasted_iota(I32, (8, LANES), 0)
    for h in range(PEER_HEADS):
        for tc in range(tr // LANES):
            tok = slice(tc * LANES, (tc + 1) * LANES)
            qh = q_ref[tok, :]
            s1 = lax.dot_general(keys_ref[0], qh[:, (2 * h) * nk:(2 * h + 1) * nk], _NT,
                                 preferred_element_type=F32)
            s2 = lax.dot_general(keys_ref[1], qh[:, (2 * h + 1) * nk:(2 * h + 2) * nk], _NT,
                                 preferred_element_type=F32)
            a, marked1 = _top_sorted(s1, k)
            b, marked2 = _top_sorted(s2, k)
            b8 = b[0:8]
            parts = [a[0:1] + b, a[1:2] + b8]
            for i, lim in ((2, 5), (3, 4), (4, 3), (5, 2), (6, 2), (7, 2)):
                parts.append(jnp.where(row8 < lim, a[i:i + 1] + b8, -jnp.inf))
            parts.append(a[8:16] + b[0:1])
            cand = jnp.concatenate(parts, axis=0)
            x = cand
            thr = None
            for _ in range(k):
                thr = jnp.max(x, axis=0, keepdims=True)
                x = jnp.where(x == thr, -jnp.inf, x)
            mx = a[0:1] + b[0:1]
            z = jnp.sum(jnp.where(cand >= thr, jnp.exp(cand - mx), 0.0), axis=0, keepdims=True)
            c1 = jnp.zeros((nk, LANES), F32)
            for r in range(k):
                cnt = jnp.sum(jnp.where(a[r:r + 1] + b >= thr, 1.0, 0.0), axis=0, keepdims=True)
                c1 = jnp.where(marked1 == _rank_mark(r), cnt, c1)
            rank2 = jnp.where(marked2 <= -RANK_BASE,
                              jnp.floor((-marked2 - RANK_BASE) * (1.0 / RANK_STEP) + 0.5), 127.0)
            c1_ref[h, :, tok] = c1
            a_ref[h, :, tok] = jnp.exp(s1 - a[0:1]) / z
            r2_ref[h, :, tok] = rank2.astype(BF16)
            w_ref[h, :, tok] = jnp.exp(s2 - b[0:1]).astype(BF16)


def _router(q, sub_keys):
    t = q.shape[0]
    tr = ROUTER_TILE
    nk = sub_keys.shape[1]
    shp32 = jax.ShapeDtypeStruct((PEER_HEADS, nk, t), F32)
    shp16 = jax.ShapeDtypeStruct((PEER_HEADS, nk, t), BF16)
    spec = pl.BlockSpec((PEER_HEADS, nk, tr), lambda i: (0, 0, i))
    return pl.pallas_call(
        functools.partial(_router_kernel, tr=tr),
        grid=(t // tr,),
        in_specs=[pl.BlockSpec((tr, q.shape[1]), lambda i: (i, 0)),
                  pl.BlockSpec(sub_keys.shape, lambda i: (0, 0, 0))],
        out_specs=[spec] * 4, out_shape=[shp32, shp32, shp16, shp16],
        compiler_params=_params(("arbitrary",)), name="peer_router",
    )(q, sub_keys)


def _peer_kernel(h_ref, u_ref, vt_ref, c1_ref, a_ref, r2_ref, w_ref, x_ref, g_ref, *rest,
                 tm, te, nk, ne, final):
    if final:
        gf_ref, o_ref, act0_ref, act1_ref, p_ref, acc_ref, r2s_ref, ws_ref = rest
    else:
        o_ref, act0_ref, act1_ref, p_ref, acc_ref, r2s_ref, ws_ref = rest
    s = pl.program_id(0)
    e0 = jnp.maximum(s - 1, 0) % ne
    nsub = nk // BF16_ROWS

    @pl.when(s == 0)
    def _():
        act1_ref[...] = jnp.zeros_like(act1_ref)

    @pl.when(e0 == 0)
    def _():
        acc_ref[...] = jnp.zeros_like(acc_ref)
        r2s_ref[...] = r2_ref[...]
        ws_ref[...] = w_ref[...]

    def main(cur_ref, prev_ref):
        tile = 2 * LANES

        def gate_piece(ii_list, tc_list):
            for tc in tc_list:
                tok = slice(tc * LANES, (tc + 1) * LANES)
                gates = [[None] * nsub for _ in ii_list]
                for h in range(PEER_HEADS):
                    r2 = [r2s_ref[h, k * BF16_ROWS:(k + 1) * BF16_ROWS, tok] for k in range(nsub)]
                    w2 = [ws_ref[h, k * BF16_ROWS:(k + 1) * BF16_ROWS, tok] for k in range(nsub)]
                    for n, ii in enumerate(ii_list):
                        c = jnp.broadcast_to(c1_ref[h, ii:ii + 1, tok], (BF16_ROWS, LANES)).astype(BF16)
                        a = jnp.broadcast_to(a_ref[h, ii:ii + 1, tok], (BF16_ROWS, LANES)).astype(BF16)
                        for k in range(nsub):
                            t = jnp.where(r2[k] < c, w2[k] * a, 0)
                            gates[n][k] = t if gates[n][k] is None else gates[n][k] + t
                for n, ii in enumerate(ii_list):
                    for k in range(nsub):
                        rows = slice(ii * nk + k * BF16_ROWS, ii * nk + (k + 1) * BF16_ROWS)
                        p_ref[rows, tok] = gates[n][k] * jax.nn.gelu(prev_ref[rows, tok])

        ii_per = tile // nk
        tc_per = tile // LANES
        for nt in range(tm // tile):
            cols = slice(nt * tile, (nt + 1) * tile)
            for kt in range(te // tile):
                rows = slice(kt * tile, (kt + 1) * tile)
                gate_piece(list(range(kt * ii_per, (kt + 1) * ii_per)),
                           list(range(nt * tc_per, (nt + 1) * tc_per)))
                acc_ref[:, cols] += jnp.dot(vt_ref[:, rows], p_ref[rows, cols], preferred_element_type=F32)
                if kt % 2 == 1:
                    urows = slice((kt // 2) * 2 * tile, (kt // 2 + 1) * 2 * tile)
                    cur_ref[urows, cols] = lax.dot_general(u_ref[urows, :], h_ref[cols, :], _NT,
                                                           preferred_element_type=F32).astype(BF16)

    @pl.when(s % 2 == 0)
    def _():
        main(act0_ref, act1_ref)

    @pl.when(s % 2 == 1)
    def _():
        main(act1_ref, act0_ref)

    @pl.when((e0 == ne - 1) & (s > 0))
    def _():
        xn = x_ref[...] + g_ref[0] * acc_ref[...].T
        if final:
            xn = _rms(xn, gf_ref[...])
        o_ref[...] = xn


def _peer(h, u, vt, layer, c1, a, r2, w, x, g2, seq, final_gain=None):
    t, d = x.shape
    tm, te = PEER_TOKEN_TILE, PEER_EXPERT_TILE
    ne = u.shape[1] // te
    nk = r2.shape[1]
    rows_per = te // nk
    per_b = seq // tm
    final = final_gain is not None
    steps = (t // tm) * ne + 1

    def cur(s):
        s1 = jnp.minimum(s, steps - 2)
        return s1 // ne, s1 % ne

    def prev(s):
        s0 = jnp.maximum(s - 1, 0)
        return s0 // ne, s0 % ne

    in_specs = [pl.BlockSpec((tm, d), lambda s: (cur(s)[0], 0)),
                pl.BlockSpec((None, te, d), lambda s: (layer, cur(s)[1], 0)),
                pl.BlockSpec((None, d, te), lambda s: (layer, 0, prev(s)[1])),
                pl.BlockSpec((PEER_HEADS, rows_per, tm), lambda s: (0, prev(s)[1], prev(s)[0])),
                pl.BlockSpec((PEER_HEADS, rows_per, tm), lambda s: (0, prev(s)[1], prev(s)[0])),
                pl.BlockSpec((PEER_HEADS, nk, tm), lambda s: (0, 0, prev(s)[0])),
                pl.BlockSpec((PEER_HEADS, nk, tm), lambda s: (0, 0, prev(s)[0])),
                pl.BlockSpec((tm, d), lambda s: (prev(s)[0], 0)),
                pl.BlockSpec((1, 1, d), lambda s: (prev(s)[0] // per_b, 0, 0))]
    args = [h, u, vt, c1, a, r2, w, x, g2]
    if final:
        in_specs.append(pl.BlockSpec((1, d), lambda s: (0, 0)))
        args.append(final_gain.reshape(1, d))
    return pl.pallas_call(
        functools.partial(_peer_kernel, tm=tm, te=te, nk=nk, ne=ne, final=final),
        grid=(steps,),
        in_specs=in_specs,
        out_specs=pl.BlockSpec((tm, d), lambda s: (prev(s)[0], 0)),
        out_shape=jax.ShapeDtypeStruct((t, d), F32),
        scratch_shapes=[pltpu.VMEM((te, tm), BF16), pltpu.VMEM((te, tm), BF16),
                        pltpu.VMEM((te, tm), BF16), pltpu.VMEM((d, tm), F32),
                        pltpu.VMEM((PEER_HEADS, nk, tm), BF16), pltpu.VMEM((PEER_HEADS, nk, tm), BF16)],
        compiler_params=_params(("arbitrary",)), name="peer_experts",
    )(*args)


def _pad_cols(w, n):
    return jnp.pad(w, ((0, 0), (0, n - w.shape[1])))


def _even_w_in(w):
    na, ni, nb = A_HEADS * HEAD_DIM, IDX_HEADS * IDX_DIM, B_HEADS * 2 * HEAD_DIM
    o = np.cumsum([0, na, A_LATENT, ni, IDX_DIM, IDX_HEADS, nb, nb, nb])
    kw = _pad_cols(w[:, o[3]:o[5]], LANES)
    return jnp.concatenate([w[:, o[0]:o[3]], kw, w[:, o[5]:o[8]]], axis=1).astype(BF16)


def kernel(x, c, rel_bias, ada_w, ada_b, norm_mix, norm_ffn, norm_final, even_w_in, even_w_out,
           a_kv_norm, a_w_uk, a_w_uv, b_lambda, b_subln, odd_w_in, odd_b_forget, odd_w_out,
           peer_w_q, peer_sub_keys, peer_u, peer_v):
    batch, seq, d = x.shape
    depth = ada_w.shape[0]
    t = batch * seq
    assert seq % ROW_TILE == 0 and seq % ATT_BLOCK == 0 and t % PEER_TOKEN_TILE == 0

    mod = _ada(c, ada_w, ada_b)
    tiles = _bias_tiles(rel_bias)

    na, ni, nb = A_HEADS * HEAD_DIM, IDX_HEADS * IDX_DIM, B_HEADS * 2 * HEAD_DIM
    nc = C_HEADS * HEAD_DIM
    o = np.cumsum([0, na, A_LATENT, ni, LANES, nb, nb, nb])
    even_segs = [(o[0], o[1], "Tq", [BF16]), (o[1], o[2], "kvnorm", [BF16]), (o[2], o[3], "Tq", [BF16]),
                 (o[3], o[4], None, [BF16]), (o[3], o[4], "T", [F32]), (o[4], o[5], "Tq", [BF16]),
                 (o[5], o[6], None, [BF16]), (o[6], o[7], "T", [BF16])]
    odd_segs = [(0, nc, "Tq", [BF16]), (nc, 2 * nc, None, [BF16]), (2 * nc, 3 * nc, "T", [BF16]),
                (3 * nc, 3 * nc + LANES, None, [F32])]
    nq_peer = peer_w_q.shape[2]
    peer_segs = [(0, nq_peer, None, [BF16])]

    u_all = peer_u.astype(BF16)
    vt_all = jnp.transpose(peer_v, (0, 2, 1)).astype(BF16)
    xt = x.reshape(t, d)
    for layer in range(depth):
        m6 = mod[layer].reshape(batch, 6, 1, d)
        sh1, sc1, g1, sh2, sc2, g2 = [m6[:, i] for i in range(6)]
        if layer % 2 == 0:
            e = layer // 2
            lam_init = 0.8 - 0.6 * math.exp(-0.3 * layer)
            w_in = _even_w_in(even_w_in[e])
            qat, ckv, ckvt, qit, kw, wit, qbt, k_b, vbt = _normproj(
                xt, norm_mix[layer], sc1, sh1, w_in, even_segs, seq, kvn=a_kv_norm[e], wt=w_in.T,
                name="even_in_proj")
            wuk = jnp.transpose(a_w_uk[e], (1, 0, 2)).astype(BF16)
            wuvt = jnp.transpose(a_w_uv[e], (1, 2, 0)).astype(BF16)
            o_a = _dsa_attention(rel_bias, qat, qit, wit, kw, ckv, ckvt, wuk, wuvt, tiles, batch, seq)
            o_b = _diff_attention(rel_bias, qbt, k_b, vbt, tiles, b_lambda[e], b_subln[e], batch, seq, lam_init)
            mixes = [o_a, o_b]
            w_out = even_w_out[e].astype(BF16)
        else:
            od = layer // 2
            w_in = _pad_cols(odd_w_in[od], 3 * nc + LANES).astype(BF16)
            qt, k, vt, f = _normproj(xt, norm_mix[layer], sc1, sh1, w_in, odd_segs, seq, wt=w_in.T,
                                     name="odd_in_proj")
            kf, qft = _foxgate(f, odd_b_forget[od], batch, seq)
            mixes = [_fox_attention(qt, qft, k, kf, vt, batch, seq)]
            w_out = odd_w_out[od].astype(BF16)
        xt, h2, qp = _resproj(xt, mixes, w_out, g1, norm_ffn[layer], sc2, sh2,
                              peer_w_q[layer].astype(BF16), peer_segs, seq)
        c1, a, r2, w = _router(qp, peer_sub_keys[layer].astype(BF16))
        xt = _peer(h2, u_all, vt_all, layer, c1, a, r2, w,
                   xt, g2, seq, final_gain=norm_final if layer == depth - 1 else None)
    return xt.reshape(batch, seq, d)
```

```python
import functools
import math

import numpy as np
import jax
import jax.numpy as jnp
from jax import lax
from jax.experimental import pallas as pl
from jax.experimental.pallas import tpu as pltpu

F32 = jnp.float32
BF16 = jnp.bfloat16
I32 = jnp.int32

HEAD_DIM = 64
RMS_EPS = 1e-6
NEG_INF = -1e30
A_HEADS = 8
A_LATENT = 256
IDX_HEADS = 8
IDX_DIM = 64
TOPK_MAX = 256
B_HEADS = 4
C_HEADS = 16
REL_BUCKETS = 32
REL_MAX_DIST = 128
PEER_HEADS = 8
PEER_TOPK = 16

LANES = 128
BF16_ROWS = 16
ATT_BLOCK = 256
ROW_TILE = 512
PEER_TOKEN_TILE = 512
PEER_EXPERT_TILE = 2048
ROUTER_TILE = 512
FOX_GROUP = 8
DIFF_GROUP = 4
F_PIECES = 3
VMEM_LIMIT = 56 * 1024 * 1024
INT_MIN = -2 ** 31

_NT = (((1,), (1,)), ((), ()))


def _params(sem):
    return pltpu.CompilerParams(dimension_semantics=sem, vmem_limit_bytes=VMEM_LIMIT)


def _rms(x, g):
    return x * lax.rsqrt(jnp.mean(x * x, axis=-1, keepdims=True) + RMS_EPS) * g


def _ones_row(rows, cols):
    return jnp.where(lax.broadcasted_iota(I32, (rows, cols), 0) == 0, 1.0, 0.0).astype(BF16)


def _key_le_query(tb):
    return lax.broadcasted_iota(I32, (tb, tb), 0) <= lax.broadcasted_iota(I32, (tb, tb), 1)


def _ada_kernel(c_ref, w_ref, b_ref, o_ref):
    c = c_ref[...]
    ca = c * jax.nn.sigmoid(c)
    o_ref[0] = jnp.dot(ca, w_ref[0], precision=lax.Precision.HIGHEST,
                       preferred_element_type=F32) + b_ref[0]


def _ada(c, ada_w, ada_b):
    depth, d, n = ada_w.shape
    b = c.shape[0]
    bp = 8
    cp = jnp.zeros((bp, d), F32).at[:b].set(c)
    tn = 1536
    out = pl.pallas_call(
        _ada_kernel,
        grid=(depth, n // tn),
        in_specs=[pl.BlockSpec((bp, d), lambda l, j: (0, 0)),
                  pl.BlockSpec((1, d, tn), lambda l, j: (l, 0, j)),
                  pl.BlockSpec((1, 1, tn), lambda l, j: (l, 0, j))],
        out_specs=pl.BlockSpec((1, bp, tn), lambda l, j: (l, 0, j)),
        out_shape=jax.ShapeDtypeStruct((depth, bp, n), F32),
        compiler_params=_params(("arbitrary", "arbitrary")),
        name="ada_mod",
    )(cp, ada_w, ada_b.reshape(depth, 1, n))
    return out[:, :b]


def _store_transposed(out_ref, rt, dt):
    for c in range(rt.shape[1] // ATT_BLOCK):
        out_ref[c] = rt[:, c * ATT_BLOCK:(c + 1) * ATT_BLOCK].astype(dt)


def _emit_segments(hb, w_ref, wt_ref, segs, outs, kvn_ref):
    k = 0
    for (a, b, kind, dtypes) in segs:
        if kind in ("T", "Tq"):
            r = lax.dot_general(wt_ref[a:b, :], hb, _NT, preferred_element_type=F32)
            if kind == "Tq":
                r = r * (HEAD_DIM ** -0.5)
            for dt in dtypes:
                _store_transposed(outs[k], r, dt)
                k += 1
            continue
        r = jnp.dot(hb, w_ref[:, a:b], preferred_element_type=F32)
        if kind == "kvnorm":
            r = _rms(r, kvn_ref[...])
            outs[k][...] = r.astype(dtypes[0])
            _store_transposed(outs[k + 1], r.T, dtypes[0])
            k += 2
            continue
        for dt in dtypes:
            outs[k][...] = r.astype(dt)
            k += 1


def _normproj_kernel(*refs, segs, has_kvn, has_wt):
    x_ref, gain_ref, sc_ref, sh_ref, w_ref = refs[:5]
    pos = 5
    wt_ref = kvn_ref = None
    if has_wt:
        wt_ref = refs[pos]
        pos += 1
    if has_kvn:
        kvn_ref = refs[pos]
        pos += 1
    outs = list(refs[pos:])
    h = _rms(x_ref[...], gain_ref[...]) * (1.0 + sc_ref[0]) + sh_ref[0]
    _emit_segments(h.astype(BF16), w_ref, wt_ref, segs, outs, kvn_ref)


def _resproj_kernel(*refs, n_mix, segs):
    x_ref = refs[0]
    mix_refs = refs[1:1 + n_mix]
    wo_ref, g_ref, gain_ref, sc_ref, sh_ref, w_ref = refs[1 + n_mix:7 + n_mix]
    outs = list(refs[7 + n_mix:])
    y = None
    off = 0
    for m in mix_refs:
        kdim = m.shape[1]
        t = jnp.dot(m[...], wo_ref[off:off + kdim, :], preferred_element_type=F32)
        y = t if y is None else y + t
        off += kdim
    xn = x_ref[...] + g_ref[0] * y
    outs[0][...] = xn
    h = _rms(xn, gain_ref[...]) * (1.0 + sc_ref[0]) + sh_ref[0]
    hb = h.astype(BF16)
    outs[1][...] = hb
    _emit_segments(hb, w_ref, None, segs, outs[2:], None)


def _seg_out_shapes(t, segs):
    shapes, specs = [], []
    per = ROW_TILE // ATT_BLOCK

    def plain(n, dt):
        shapes.append(jax.ShapeDtypeStruct((t, n), dt))
        specs.append(pl.BlockSpec((ROW_TILE, n), lambda i: (i, 0)))

    def transposed(n, dt):
        shapes.append(jax.ShapeDtypeStruct((t // ATT_BLOCK, n, ATT_BLOCK), dt))
        specs.append(pl.BlockSpec((per, n, ATT_BLOCK), lambda i: (i, 0, 0)))

    for (a, b, kind, dtypes) in segs:
        if kind == "kvnorm":
            plain(b - a, dtypes[0])
            transposed(b - a, dtypes[0])
            continue
        for dt in dtypes:
            (transposed if kind in ("T", "Tq") else plain)(b - a, dt)
    return shapes, specs


def _normproj(x, gain, sc, sh, w, segs, seq, kvn=None, wt=None, name="normproj"):
    t, d = x.shape
    tm = ROW_TILE
    per_b = seq // tm
    n = w.shape[1]
    mod_spec = pl.BlockSpec((1, 1, d), lambda i: (i // per_b, 0, 0))
    in_specs = [pl.BlockSpec((tm, d), lambda i: (i, 0)),
                pl.BlockSpec((1, d), lambda i: (0, 0)), mod_spec, mod_spec,
                pl.BlockSpec((d, n), lambda i: (0, 0))]
    args = [x, gain.reshape(1, d), sc, sh, w]
    if wt is not None:
        in_specs.append(pl.BlockSpec((n, d), lambda i: (0, 0)))
        args.append(wt)
    if kvn is not None:
        in_specs.append(pl.BlockSpec((1, kvn.shape[-1]), lambda i: (0, 0)))
        args.append(kvn.reshape(1, -1))
    shapes, specs = _seg_out_shapes(t, segs)
    return pl.pallas_call(
        functools.partial(_normproj_kernel, segs=segs, has_kvn=kvn is not None, has_wt=wt is not None),
        grid=(t // tm,), in_specs=in_specs, out_specs=specs, out_shape=shapes,
        compiler_params=_params(("arbitrary",)), name=name,
    )(*args)


def _resproj(x, mixes, w_out, g, gain, sc, sh, w, segs, seq, name="resproj"):
    t, d = x.shape
    tm = ROW_TILE
    per_b = seq // tm
    n = w.shape[1]
    mod_spec = pl.BlockSpec((1, 1, d), lambda i: (i // per_b, 0, 0))
    in_specs = [pl.BlockSpec((tm, d), lambda i: (i, 0))]
    in_specs += [pl.BlockSpec((tm, m.shape[1]), lambda i: (i, 0)) for m in mixes]
    in_specs += [pl.BlockSpec(w_out.shape, lambda i: (0, 0)), mod_spec,
                 pl.BlockSpec((1, d), lambda i: (0, 0)), mod_spec, mod_spec,
                 pl.BlockSpec((d, n), lambda i: (0, 0))]
    shapes, specs = _seg_out_shapes(t, segs)
    shapes = [jax.ShapeDtypeStruct((t, d), F32), jax.ShapeDtypeStruct((t, d), BF16)] + shapes
    specs = [pl.BlockSpec((tm, d), lambda i: (i, 0)), pl.BlockSpec((tm, d), lambda i: (i, 0))] + specs
    return pl.pallas_call(
        functools.partial(_resproj_kernel, n_mix=len(mixes), segs=segs),
        grid=(t // tm,), in_specs=in_specs, out_specs=specs, out_shape=shapes,
        compiler_params=_params(("arbitrary",)), name=name,
    )(x, *mixes, w_out, g, gain.reshape(1, d), sc, sh, w)


def _bucket_table(n):
    max_exact = REL_BUCKETS // 2
    d = np.arange(n)
    df = np.maximum(d, 1).astype(np.float32)
    large = max_exact + (np.log(df / max_exact) / math.log(REL_MAX_DIST / max_exact)
                         * (REL_BUCKETS - max_exact)).astype(np.int32)
    large = np.minimum(large, REL_BUCKETS - 1)
    return np.where(d < max_exact, d, large).astype(np.int32)


def _bias_tiles_kernel(rb_ref, bk_ref, o_ref):
    h = pl.program_id(0)
    for t in range(2):
        bt = bk_ref[t]
        acc = jnp.zeros(bt.shape, F32)
        for b in range(REL_BUCKETS):
            acc = jnp.where(bt == b, rb_ref[b, h], acc)
        o_ref[0, t] = acc


def _bias_tiles(rel_bias):
    tb = ATT_BLOCK
    nh = rel_bias.shape[1]
    table = _bucket_table(2 * tb)
    s = np.arange(tb)[:, None]
    t = np.arange(tb)[None, :]
    bk = np.stack([table[np.maximum(t - s, 0)], table[tb + t - s]]).astype(np.int32)
    return pl.pallas_call(
        _bias_tiles_kernel,
        grid=(nh,),
        in_specs=[pl.BlockSpec(memory_space=pltpu.SMEM),
                  pl.BlockSpec((2, tb, tb), lambda h: (0, 0, 0))],
        out_specs=pl.BlockSpec((1, 2, tb, tb), lambda h: (h, 0, 0, 0)),
        out_shape=jax.ShapeDtypeStruct((nh, 2, tb, tb), F32),
        compiler_params=_params(("arbitrary",)), name="bias_tiles",
    )(rel_bias, jnp.asarray(bk))


def _flash_update(s, m, shift_const, acc_old, vaug):
    m_new = jnp.maximum(m, jnp.max(s, axis=0, keepdims=True) + shift_const)
    p = jnp.exp(s - (m_new - shift_const)).astype(BF16)
    acc = jnp.exp(m - m_new) * acc_old + jnp.dot(vaug, p, preferred_element_type=F32)
    return m_new, acc


def _split_bf16(x):
    pieces = []
    for _ in range(F_PIECES):
        p = x.astype(BF16)
        pieces.append(p)
        x = x - p.astype(F32)
    return pieces


def _fox_sel():
    selk = np.zeros((F_PIECES, LANES, C_HEADS * HEAD_DIM), np.float32)
    selq = np.zeros((F_PIECES, C_HEADS * HEAD_DIM, LANES), np.float32)
    onesk = np.zeros((1, C_HEADS * HEAD_DIM), np.float32)
    onesq = np.zeros((C_HEADS * HEAD_DIM, 1), np.float32)
    for h in range(C_HEADS):
        for p in range(F_PIECES):
            selk[p, h, h * HEAD_DIM + p] = -1.0
            selq[p, h * HEAD_DIM + F_PIECES + p, h] = 1.0
            onesk[0, h * HEAD_DIM + F_PIECES + p] = 1.0
            onesq[h * HEAD_DIM + p, 0] = 1.0
    return selk, selq, onesk, onesq


def _foxgate_kernel(f_ref, b_ref, selk_ref, selq_ref, onesk_ref, onesq_ref, kf_ref, qft_ref, *, tb):
    nblk = f_ref.shape[0] // tb
    row = lax.broadcasted_iota(I32, (tb, tb), 0)
    col = lax.broadcasted_iota(I32, (tb, tb), 1)
    tri = jnp.where(col <= row, 1.0, 0.0).astype(F32)

    def body(j, carry):
        z = f_ref[pl.ds(j * tb, tb), :] + b_ref[...]
        ls = -(jnp.maximum(-z, 0.0) + jnp.log(1.0 + jnp.exp(-jnp.abs(z))))
        cs = jnp.dot(tri, ls, precision=lax.Precision.HIGHEST, preferred_element_type=F32) + carry
        kf = onesk_ref[...]
        for p, piece in enumerate(_split_bf16(cs)):
            kf = kf + jnp.dot(piece, selk_ref[p], preferred_element_type=F32)
        kf_ref[pl.ds(j * tb, tb), :] = kf.astype(BF16)
        qf = onesq_ref[...]
        for p, piece in enumerate(_split_bf16(cs.T)):
            qf = qf + jnp.dot(selq_ref[p], piece, preferred_element_type=F32)
        qft_ref[j] = qf.astype(BF16)
        return cs[tb - 1:tb, :]

    lax.fori_loop(0, nblk, body, jnp.zeros((1, LANES), F32))


def _foxgate(f, b_forget, batch, seq):
    tb = ATT_BLOCK
    nblk = seq // tb
    n = C_HEADS * HEAD_DIM
    bpad = jnp.zeros((1, LANES), F32).at[0, :C_HEADS].set(b_forget)
    selk, selq, onesk, onesq = _fox_sel()
    return pl.pallas_call(
        functools.partial(_foxgate_kernel, tb=tb),
        grid=(batch,),
        in_specs=[pl.BlockSpec((seq, LANES), lambda b: (b, 0)),
                  pl.BlockSpec((1, LANES), lambda b: (0, 0)),
                  pl.BlockSpec(selk.shape, lambda b: (0, 0, 0)),
                  pl.BlockSpec(selq.shape, lambda b: (0, 0, 0)),
                  pl.BlockSpec(onesk.shape, lambda b: (0, 0)),
                  pl.BlockSpec(onesq.shape, lambda b: (0, 0))],
        out_specs=[pl.BlockSpec((seq, n), lambda b: (b, 0)),
                   pl.BlockSpec((nblk, n, tb), lambda b: (b, 0, 0))],
        out_shape=[jax.ShapeDtypeStruct((batch * seq, n), BF16),
                   jax.ShapeDtypeStruct((batch * nblk, n, tb), BF16)],
        compiler_params=_params(("arbitrary",)), name="fox_gates",
    )(f, bpad, jnp.asarray(selk, BF16), jnp.asarray(selq, BF16), jnp.asarray(onesk), jnp.asarray(onesq))


def _fox_kernel(qt_ref, qft_ref, k_ref, kf_ref, vt_ref, o_ref, kaug_ref, vaug_ref, *, tb, nq):
    qi = pl.program_id(2)
    hd = HEAD_DIM
    heads = range(FOX_GROUP)

    @pl.when(qi == 0)
    def _():
        ones_rows = _ones_row(hd, tb)
        for h in heads:
            kaug_ref[h, :, 0:hd] = k_ref[:, h * hd:(h + 1) * hd]
            kaug_ref[h, :, hd:2 * hd] = kf_ref[:, h * hd:(h + 1) * hd]

            def fill(j, c, h=h):
                vaug_ref[h, j, 0:hd, :] = vt_ref[j, h * hd:(h + 1) * hd, :]
                vaug_ref[h, j, hd:2 * hd, :] = ones_rows
                return c

            lax.fori_loop(0, nq, fill, 0)

    qa = [jnp.concatenate([qt_ref[0, h * hd:(h + 1) * hd, :], qft_ref[0, h * hd:(h + 1) * hd, :]], axis=0)
          for h in heads]
    keep = _key_le_query(tb)

    def step(j, carry, masked):
        scores = [jnp.dot(kaug_ref[h, pl.ds(j * tb, tb), :], qa[h], preferred_element_type=F32)
                  for h in heads]
        out = []
        for h in heads:
            m, acc = carry[h]
            s = jnp.where(keep, scores[h], NEG_INF) if masked else scores[h]
            out.append(_flash_update(s, m, 0.0, acc, vaug_ref[h, j]))
        return tuple(out)

    init = tuple((jnp.full((1, tb), NEG_INF, F32), jnp.zeros((2 * hd, tb), F32)) for _ in heads)

    def step2(i, carry):
        ja = 2 * i
        sa = [jnp.dot(kaug_ref[h, pl.ds(ja * tb, tb), :], qa[h], preferred_element_type=F32) for h in heads]
        sb = [jnp.dot(kaug_ref[h, pl.ds((ja + 1) * tb, tb), :], qa[h], preferred_element_type=F32) for h in heads]
        out = []
        for h in heads:
            m, acc = carry[h]
            m, acc = _flash_update(sa[h], m, 0.0, acc, vaug_ref[h, ja])
            out.append(_flash_update(sb[h], m, 0.0, acc, vaug_ref[h, ja + 1]))
        return tuple(out)

    carry = lax.fori_loop(0, qi // 2, step2, init)
    carry = lax.cond(qi % 2 == 1, lambda c: step(qi - 1, c, False), lambda c: c, carry)
    carry = step(qi, carry, True)
    for pair in range(FOX_GROUP // 2):
        ot = jnp.concatenate([carry[h][1][0:hd] / carry[h][1][hd:hd + 1] for h in (2 * pair, 2 * pair + 1)], axis=0)
        o_ref[:, pair * 2 * hd:(pair + 1) * 2 * hd] = ot.T.astype(o_ref.dtype)


def _fox_attention(qt, qft, k, kf, vt, batch, seq):
    tb = ATT_BLOCK
    nq = seq // tb
    groups = C_HEADS // FOX_GROUP
    gw = FOX_GROUP * HEAD_DIM
    return pl.pallas_call(
        functools.partial(_fox_kernel, tb=tb, nq=nq),
        grid=(batch, groups, nq),
        in_specs=[pl.BlockSpec((1, gw, tb), lambda b, g, i: (b * nq + i, g, 0)),
                  pl.BlockSpec((1, gw, tb), lambda b, g, i: (b * nq + i, g, 0)),
                  pl.BlockSpec((seq, gw), lambda b, g, i: (b, g)),
                  pl.BlockSpec((seq, gw), lambda b, g, i: (b, g)),
                  pl.BlockSpec((nq, gw, tb), lambda b, g, i: (b, g, 0))],
        out_specs=pl.BlockSpec((tb, gw), lambda b, g, i: (b * nq + i, g)),
        out_shape=jax.ShapeDtypeStruct((batch * seq, C_HEADS * HEAD_DIM), BF16),
        scratch_shapes=[pltpu.VMEM((FOX_GROUP, seq, 2 * HEAD_DIM), BF16),
                        pltpu.VMEM((FOX_GROUP, nq, 2 * HEAD_DIM, tb), BF16)],
        compiler_params=_params(("arbitrary", "arbitrary", "arbitrary")), name="fox_attention",
    )(qt, qft, k, kf, vt)


def _diff_kernel(rb_ref, qt_ref, k_ref, vt_ref, tiles_ref, lam_ref, g_ref, o_ref,
                 ks_ref, vaug_ref, acc_ref, *, tb, nq, lam_init):
    g = pl.program_id(1)
    qi = pl.program_id(2)
    hd = HEAD_DIM
    dv = 2 * hd
    chains = [(hh, s_) for hh in range(DIFF_GROUP) for s_ in range(2)]

    @pl.when(qi == 0)
    def _():
        ones_rows = _ones_row(BF16_ROWS, tb)
        for c, (hh, s_) in enumerate(chains):
            ks_ref[c] = k_ref[:, hh * dv + s_ * hd:hh * dv + (s_ + 1) * hd]
        for hh in range(DIFF_GROUP):
            def fill(j, c, hh=hh):
                vaug_ref[hh, j, 0:dv, :] = vt_ref[j, hh * dv:(hh + 1) * dv, :]
                vaug_ref[hh, j, dv:dv + BF16_ROWS, :] = ones_rows
                return c

            lax.fori_loop(0, nq, fill, 0)

    qs = [qt_ref[0, hh * dv + s_ * hd:hh * dv + (s_ + 1) * hd, :] for (hh, s_) in chains]
    cfar = [rb_ref[REL_BUCKETS - 1, A_HEADS + g * DIFF_GROUP + hh] for hh in range(DIFF_GROUP)]
    keep = _key_le_query(tb)
    acc_ref[...] = jnp.zeros_like(acc_ref)

    def step(j, ms, kind):
        scores = [jnp.dot(ks_ref[c, pl.ds(j * tb, tb), :], qs[c], preferred_element_type=F32)
                  for c in range(len(chains))]
        out = []
        for c, (hh, s_) in enumerate(chains):
            if kind == "far":
                s, shift = scores[c], cfar[hh]
            else:
                s, shift = scores[c] + tiles_ref[hh, 0 if kind == "diag" else 1], 0.0
                if kind == "diag":
                    s = jnp.where(keep, s, NEG_INF)
            m_new, acc = _flash_update(s, ms[c], shift, acc_ref[c], vaug_ref[hh, j])
            acc_ref[c] = acc
            out.append(m_new)
        return tuple(out)

    def far2(i, ms):
        ja = 2 * i
        sa = [jnp.dot(ks_ref[c, pl.ds(ja * tb, tb), :], qs[c], preferred_element_type=F32)
              for c in range(len(chains))]
        sb = [jnp.dot(ks_ref[c, pl.ds((ja + 1) * tb, tb), :], qs[c], preferred_element_type=F32)
              for c in range(len(chains))]
        out = []
        for c, (hh, s_) in enumerate(chains):
            m, acc = _flash_update(sa[c], ms[c], cfar[hh], acc_ref[c], vaug_ref[hh, ja])
            m, acc = _flash_update(sb[c], m, cfar[hh], acc, vaug_ref[hh, ja + 1])
            acc_ref[c] = acc
            out.append(m)
        return tuple(out)

    nfar = jnp.maximum(qi - 1, 0)
    ms = tuple(jnp.full((1, tb), NEG_INF, F32) for _ in chains)
    ms = lax.fori_loop(0, nfar // 2, far2, ms)
    ms = lax.cond(nfar % 2 == 1, lambda c: step(nfar - 1, c, "far"), lambda c: c, ms)
    ms = lax.cond(qi >= 1, lambda c: step(qi - 1, c, "near"), lambda c: c, ms)
    step(qi, ms, "diag")

    lv = lam_ref[...]
    lam = (jnp.exp(jnp.sum(lv[0:1] * lv[1:2], axis=-1, keepdims=True))
           - jnp.exp(jnp.sum(lv[2:3] * lv[3:4], axis=-1, keepdims=True)) + lam_init)
    for hh in range(DIFF_GROUP):
        a1 = acc_ref[2 * hh]
        a2 = acc_ref[2 * hh + 1]
        ot = a1[0:dv] / a1[dv:dv + 1] - lam * (a2[0:dv] / a2[dv:dv + 1])
        o = _rms(ot.T, g_ref[...]) * (1.0 - lam_init)
        o_ref[:, hh * dv:(hh + 1) * dv] = o.astype(o_ref.dtype)


def _diff_attention(rel_bias, qt, k, vt, tiles, lam_vec, subln, batch, seq, lam_init):
    tb = ATT_BLOCK
    nq = seq // tb
    groups = B_HEADS // DIFF_GROUP
    gw = DIFF_GROUP * 2 * HEAD_DIM
    dv = 2 * HEAD_DIM
    return pl.pallas_call(
        functools.partial(_diff_kernel, tb=tb, nq=nq, lam_init=lam_init),
        grid=(batch, groups, nq),
        in_specs=[pl.BlockSpec(memory_space=pltpu.SMEM),
                  pl.BlockSpec((1, gw, tb), lambda b, g, i: (b * nq + i, g, 0)),
                  pl.BlockSpec((seq, gw), lambda b, g, i: (b, g)),
                  pl.BlockSpec((nq, gw, tb), lambda b, g, i: (b, g, 0)),
                  pl.BlockSpec((DIFF_GROUP, 2, tb, tb), lambda b, g, i: (A_HEADS // DIFF_GROUP + g, 0, 0, 0)),
                  pl.BlockSpec((4, HEAD_DIM), lambda b, g, i: (0, 0)),
                  pl.BlockSpec((1, dv), lambda b, g, i: (0, 0))],
        out_specs=pl.BlockSpec((tb, gw), lambda b, g, i: (b * nq + i, g)),
        out_shape=jax.ShapeDtypeStruct((batch * seq, B_HEADS * dv), BF16),
        scratch_shapes=[pltpu.VMEM((2 * DIFF_GROUP, seq, HEAD_DIM), BF16),
                        pltpu.VMEM((DIFF_GROUP, nq, dv + BF16_ROWS, tb), BF16),
                        pltpu.VMEM((2 * DIFF_GROUP, dv + BF16_ROWS, tb), F32)],
        compiler_params=_params(("arbitrary", "arbitrary", "arbitrary")), name="diff_attention",
    )(rel_bias, qt, k, vt, tiles, lam_vec, subln.reshape(1, -1))


def _dsa_kernel(rb_ref, qat_ref, qit_ref, wit_ref, kw_ref, ckv_ref, ckvt_ref, wuk_ref, wuvt_ref, tiles_ref,
                o_ref, keys_ref, selb_ref, qlat_ref, vaug_ref, acc_ref, *, tb, topk, nq):
    qb = pl.program_id(1)
    nblk = qb + 1
    hd = HEAD_DIM
    keep = _key_le_query(tb)

    @pl.when(qb == 0)
    def _():
        ones_rows = _ones_row(BF16_ROWS, tb)

        def fill(j, c):
            vaug_ref[j, 0:A_LATENT, :] = ckvt_ref[j]
            vaug_ref[j, A_LATENT:A_LATENT + BF16_ROWS, :] = ones_rows
            return c

        lax.fori_loop(0, nq, fill, 0)

    for h in range(A_HEADS):
        ql = jnp.dot(wuk_ref[h], qat_ref[0, h * hd:(h + 1) * hd, :], preferred_element_type=F32)
        qlat_ref[h] = ql.astype(BF16)

    wrows = [wit_ref[0, IDX_DIM + h:IDX_DIM + h + 1, :] * (IDX_HEADS ** -0.5) for h in range(IDX_HEADS)]
    zpad = jnp.zeros((LANES - IDX_DIM, tb), BF16)
    qi_pad = [jnp.concatenate([qit_ref[0, h * IDX_DIM:(h + 1) * IDX_DIM, :], zpad], axis=0)
              for h in range(IDX_HEADS)]

    def index_keys(j, masked):
        kb = kw_ref[pl.ds(j * tb, tb), :]
        isc = jnp.zeros((tb, tb), F32)
        for h in range(IDX_HEADS):
            li = jnp.dot(kb, qi_pad[h], preferred_element_type=F32)
            isc = isc + jnp.maximum(li, 0.0) * wrows[h]
        isc = jnp.where(isc == 0.0, 0.0, isc)
        bits = pltpu.bitcast(isc, I32)
        key = bits ^ ((bits >> 31) & 0x7FFFFFFF)
        if masked:
            key = jnp.where(keep, key, INT_MIN)
        keys_ref[j] = key

    def p1(j, c):
        index_keys(j, False)
        return c

    lax.fori_loop(0, qb, p1, 0)
    index_keys(qb, True)

    def count(pred):
        def body(j, acc):
            ind = jnp.where(pred(keys_ref[j]), 1.0, 0.0)
            return acc + jnp.sum(ind.reshape(tb // 8, 8, tb), axis=0)
        acc = lax.fori_loop(0, nblk, body, jnp.zeros((8, tb), F32))
        return jnp.sum(acc, axis=0, keepdims=True)

    kth = jnp.where(count(lambda k: k >= 0) >= topk, 0, INT_MIN).astype(I32)

    def bs(i, kth):
        cand = kth | lax.shift_left(jnp.int32(1), 30 - i)
        return jnp.where(count(lambda k: k >= cand) >= topk, cand, kth)

    kth = lax.fori_loop(0, 31, bs, kth)
    need = topk - count(lambda k: k > kth)

    lower = jnp.where(lax.broadcasted_iota(I32, (tb, tb), 1) <= lax.broadcasted_iota(I32, (tb, tb), 0),
                      1.0, 0.0).astype(BF16)

    def mask_block(j, seen, masked):
        key = keys_ref[j]
        eq = key == kth
        pre = jnp.dot(lower, jnp.where(eq, 1.0, 0.0).astype(BF16), preferred_element_type=F32)
        sel = (key > kth) | (eq & (pre + seen <= need))
        if masked:
            sel = sel & keep
        selb_ref[j] = jnp.where(sel, 0.0, NEG_INF)
        return seen + pre[tb - 1:tb, :]

    seen = lax.fori_loop(0, qb, lambda j, s: mask_block(j, s, False), jnp.zeros((1, tb), F32))
    mask_block(qb, seen, True)

    cfar = [rb_ref[REL_BUCKETS - 1, h] for h in range(A_HEADS)]
    acc_ref[...] = jnp.zeros_like(acc_ref)

    def step(j, ms, kind):
        kvb = ckv_ref[pl.ds(j * tb, tb), :]
        scores = [jnp.dot(kvb, qlat_ref[h], preferred_element_type=F32) for h in range(A_HEADS)]
        sb = selb_ref[j]
        out = []
        for h in range(A_HEADS):
            if kind == "far":
                s, shift = scores[h] + sb, cfar[h]
            else:
                s, shift = scores[h] + (tiles_ref[h, 0 if kind == "diag" else 1] + sb), 0.0
            m_new, acc = _flash_update(s, ms[h], shift, acc_ref[h], vaug_ref[j])
            acc_ref[h] = acc
            out.append(m_new)
        return tuple(out)

    def far2(i, ms):
        ja = 2 * i
        kva = ckv_ref[pl.ds(ja * tb, tb), :]
        kvb = ckv_ref[pl.ds((ja + 1) * tb, tb), :]
        sa = [jnp.dot(kva, qlat_ref[h], preferred_element_type=F32) for h in range(A_HEADS)]
        sb = [jnp.dot(kvb, qlat_ref[h], preferred_element_type=F32) for h in range(A_HEADS)]
        ba = selb_ref[ja]
        bb = selb_ref[ja + 1]
        out = []
        for h in range(A_HEADS):
            m, acc = _flash_update(sa[h] + ba, ms[h], cfar[h], acc_ref[h], vaug_ref[ja])
            m, acc = _flash_update(sb[h] + bb, m, cfar[h], acc, vaug_ref[ja + 1])
            acc_ref[h] = acc
            out.append(m)
        return tuple(out)

    nfar = jnp.maximum(qb - 1, 0)
    ms = tuple(jnp.full((1, tb), NEG_INF, F32) for _ in range(A_HEADS))
    ms = lax.fori_loop(0, nfar // 2, far2, ms)
    ms = lax.cond(nfar % 2 == 1, lambda c: step(nfar - 1, c, "far"), lambda c: c, ms)
    ms = lax.cond(qb >= 1, lambda c: step(qb - 1, c, "near"), lambda c: c, ms)
    step(qb, ms, "diag")

    outs = []
    for h in range(A_HEADS):
        a = acc_ref[h]
        o_lat = (a[0:A_LATENT] / a[A_LATENT:A_LATENT + 1]).astype(BF16)
        outs.append(jnp.dot(wuvt_ref[h], o_lat, preferred_element_type=F32))
    o_ref[...] = jnp.concatenate(outs, axis=0).T.astype(o_ref.dtype)


def _dsa_attention(rel_bias, qat, qit, wit, kw, ckv, ckvt, wuk, wuvt, tiles, batch, seq):
    tb = ATT_BLOCK
    nq = seq // tb
    topk = min(TOPK_MAX, seq // 4)
    nqa = A_HEADS * HEAD_DIM
    nqi = IDX_HEADS * IDX_DIM
    aug = A_LATENT + BF16_ROWS
    return pl.pallas_call(
        functools.partial(_dsa_kernel, tb=tb, topk=topk, nq=nq),
        grid=(batch, nq),
        in_specs=[pl.BlockSpec(memory_space=pltpu.SMEM),
                  pl.BlockSpec((1, nqa, tb), lambda b, i: (b * nq + i, 0, 0)),
                  pl.BlockSpec((1, nqi, tb), lambda b, i: (b * nq + i, 0, 0)),
                  pl.BlockSpec((1, LANES, tb), lambda b, i: (b * nq + i, 0, 0)),
                  pl.BlockSpec((seq, LANES), lambda b, i: (b, 0)),
                  pl.BlockSpec((seq, A_LATENT), lambda b, i: (b, 0)),
                  pl.BlockSpec((nq, A_LATENT, tb), lambda b, i: (b, 0, 0)),
                  pl.BlockSpec(wuk.shape, lambda b, i: (0, 0, 0)),
                  pl.BlockSpec(wuvt.shape, lambda b, i: (0, 0, 0)),
                  pl.BlockSpec((A_HEADS, 2, tb, tb), lambda b, i: (0, 0, 0, 0))],
        out_specs=pl.BlockSpec((tb, nqa), lambda b, i: (b * nq + i, 0)),
        out_shape=jax.ShapeDtypeStruct((batch * seq, nqa), BF16),
        scratch_shapes=[pltpu.VMEM((nq, tb, tb), I32), pltpu.VMEM((nq, tb, tb), F32),
                        pltpu.VMEM((A_HEADS, A_LATENT, tb), BF16),
                        pltpu.VMEM((nq, aug, tb), BF16),
                        pltpu.VMEM((A_HEADS, aug, tb), F32)],
        compiler_params=_params(("arbitrary", "arbitrary")), name="dsa_attention",
    )(rel_bias, qat, qit, wit, kw, ckv, ckvt, wuk, wuvt, tiles)


RANK_BASE = 1e30
RANK_STEP = 1e28


def _rank_mark(r):
    return -(RANK_BASE + r * RANK_STEP)


def _top_sorted(x, k):
    rows = lax.broadcasted_iota(I32, (k, x.shape[1]), 0)
    out = jnp.zeros((k, x.shape[1]), F32)
    for r in range(k):
        m = jnp.max(x, axis=0, keepdims=True)
        out = jnp.where(rows == r, m, out)
        x = jnp.where(x == m, _rank_mark(r), x)
    return out, x


def _router_kernel(q_ref, keys_ref, c1_ref, a_ref, r2_ref, w_ref, *, tr):
    k = PEER_TOPK
    nk = keys_ref.shape[1]
    row8 = lax.broadcasted_iota(I32, (8, LANES), 0)
    for h in range(PEER_HEADS):
        for tc in range(tr // LANES):
            tok = slice(tc * LANES, (tc + 1) * LANES)
            qh = q_ref[tok, :]
            s1 = lax.dot_general(keys_ref[0], qh[:, (2 * h) * nk:(2 * h + 1) * nk], _NT,
                                 preferred_element_type=F32)
            s2 = lax.dot_general(keys_ref[1], qh[:, (2 * h + 1) * nk:(2 * h + 2) * nk], _NT,
                                 preferred_element_type=F32)
            a, marked1 = _top_sorted(s1, k)
            b, marked2 = _top_sorted(s2, k)
            b8 = b[0:8]
            parts = [a[0:1] + b, a[1:2] + b8]
            for i, lim in ((2, 5), (3, 4), (4, 3), (5, 2), (6, 2), (7, 2)):
                parts.append(jnp.where(row8 < lim, a[i:i + 1] + b8, -jnp.inf))
            parts.append(a[8:16] + b[0:1])
            cand = jnp.concatenate(parts, axis=0)
            x = cand
            thr = None
            for _ in range(k):
                thr = jnp.max(x, axis=0, keepdims=True)
                x = jnp.where(x == thr, -jnp.inf, x)
            mx = a[0:1] + b[0:1]
            z = jnp.sum(jnp.where(cand >= thr, jnp.exp(cand - mx), 0.0), axis=0, keepdims=True)
            c1 = jnp.zeros((nk, LANES), F32)
            for r in range(k):
                cnt = jnp.sum(jnp.where(a[r:r + 1] + b >= thr, 1.0, 0.0), axis=0, keepdims=True)
                c1 = jnp.where(marked1 == _rank_mark(r), cnt, c1)
            rank2 = jnp.where(marked2 <= -RANK_BASE,
                              jnp.floor((-marked2 - RANK_BASE) * (1.0 / RANK_STEP) + 0.5), 127.0)
            c1_ref[h, :, tok] = c1
            a_ref[h, :, tok] = jnp.exp(s1 - a[0:1]) / z
            r2_ref[h, :, tok] = rank2.astype(BF16)
            w_ref[h, :, tok] = jnp.exp(s2 - b[0:1]).astype(BF16)


def _router(q, sub_keys):
    t = q.shape[0]
    tr = ROUTER_TILE
    nk = sub_keys.shape[1]
    shp32 = jax.ShapeDtypeStruct((PEER_HEADS, nk, t), F32)
    shp16 = jax.ShapeDtypeStruct((PEER_HEADS, nk, t), BF16)
    spec = pl.BlockSpec((PEER_HEADS, nk, tr), lambda i: (0, 0, i))
    return pl.pallas_call(
        functools.partial(_router_kernel, tr=tr),
        grid=(t // tr,),
        in_specs=[pl.BlockSpec((tr, q.shape[1]), lambda i: (i, 0)),
                  pl.BlockSpec(sub_keys.shape, lambda i: (0, 0, 0))],
        out_specs=[spec] * 4, out_shape=[shp32, shp32, shp16, shp16],
        compiler_params=_params(("arbitrary",)), name="peer_router",
    )(q, sub_keys)


def _peer_kernel(h_ref, u_ref, vt_ref, c1_ref, a_ref, r2_ref, w_ref, x_ref, g_ref, *rest,
                 tm, te, nk, ne, final):
    if final:
        gf_ref, o_ref, act0_ref, act1_ref, p_ref, acc_ref, r2s_ref, ws_ref = rest
    else:
        o_ref, act0_ref, act1_ref, p_ref, acc_ref, r2s_ref, ws_ref = rest
    s = pl.program_id(0)
    e0 = jnp.maximum(s - 1, 0) % ne
    nsub = nk // BF16_ROWS

    @pl.when(s == 0)
    def _():
        act1_ref[...] = jnp.zeros_like(act1_ref)

    @pl.when(e0 == 0)
    def _():
        acc_ref[...] = jnp.zeros_like(acc_ref)
        r2s_ref[...] = r2_ref[...]
        ws_ref[...] = w_ref[...]

    def main(cur_ref, prev_ref):
        tile = 2 * LANES

        def gate_piece(ii_list, tc_list):
            for tc in tc_list:
                tok = slice(tc * LANES, (tc + 1) * LANES)
                gates = [[None] * nsub for _ in ii_list]
                for h in range(PEER_HEADS):
                    r2 = [r2s_ref[h, k * BF16_ROWS:(k + 1) * BF16_ROWS, tok] for k in range(nsub)]
                    w2 = [ws_ref[h, k * BF16_ROWS:(k + 1) * BF16_ROWS, tok] for k in range(nsub)]
                    for n, ii in enumerate(ii_list):
                        c = jnp.broadcast_to(c1_ref[h, ii:ii + 1, tok], (BF16_ROWS, LANES)).astype(BF16)
                        a = jnp.broadcast_to(a_ref[h, ii:ii + 1, tok], (BF16_ROWS, LANES)).astype(BF16)
                        for k in range(nsub):
                            t = jnp.where(r2[k] < c, w2[k] * a, 0)
                            gates[n][k] = t if gates[n][k] is None else gates[n][k] + t
                for n, ii in enumerate(ii_list):
                    for k in range(nsub):
                        rows = slice(ii * nk + k * BF16_ROWS, ii * nk + (k + 1) * BF16_ROWS)
                        p_ref[rows, tok] = gates[n][k] * jax.nn.gelu(prev_ref[rows, tok])

        ii_per = tile // nk
        tc_per = tile // LANES
        for nt in range(tm // tile):
            cols = slice(nt * tile, (nt + 1) * tile)
            for kt in range(te // tile):
                rows = slice(kt * tile, (kt + 1) * tile)
                gate_piece(list(range(kt * ii_per, (kt + 1) * ii_per)),
                           list(range(nt * tc_per, (nt + 1) * tc_per)))
                acc_ref[:, cols] += jnp.dot(vt_ref[:, rows], p_ref[rows, cols], preferred_element_type=F32)
                if kt % 2 == 1:
                    urows = slice((kt // 2) * 2 * tile, (kt // 2 + 1) * 2 * tile)
                    cur_ref[urows, cols] = lax.dot_general(u_ref[urows, :], h_ref[cols, :], _NT,
                                                           preferred_element_type=F32).astype(BF16)

    @pl.when(s % 2 == 0)
    def _():
        main(act0_ref, act1_ref)

    @pl.when(s % 2 == 1)
    def _():
        main(act1_ref, act0_ref)

    @pl.when((e0 == ne - 1) & (s > 0))
    def _():
        xn = x_ref[...] + g_ref[0] * acc_ref[...].T
        if final:
            xn = _rms(xn, gf_ref[...])
        o_ref[...] = xn


def _peer(h, u, vt, layer, c1, a, r2, w, x, g2, seq, final_gain=None):
    t, d = x.shape
    tm, te = PEER_TOKEN_TILE, PEER_EXPERT_TILE
    ne = u.shape[1] // te
    nk = r2.shape[1]
    rows_per = te // nk
    per_b = seq // tm
    final = final_gain is not None
    steps = (t // tm) * ne + 1

    def cur(s):
        s1 = jnp.minimum(s, steps - 2)
        return s1 // ne, s1 % ne

    def prev(s):
        s0 = jnp.maximum(s - 1, 0)
        return s0 // ne, s0 % ne

    in_specs = [pl.BlockSpec((tm, d), lambda s: (cur(s)[0], 0)),
                pl.BlockSpec((None, te, d), lambda s: (layer, cur(s)[1], 0)),
                pl.BlockSpec((None, d, te), lambda s: (layer, 0, prev(s)[1])),
                pl.BlockSpec((PEER_HEADS, rows_per, tm), lambda s: (0, prev(s)[1], prev(s)[0])),
                pl.BlockSpec((PEER_HEADS, rows_per, tm), lambda s: (0, prev(s)[1], prev(s)[0])),
                pl.BlockSpec((PEER_HEADS, nk, tm), lambda s: (0, 0, prev(s)[0])),
                pl.BlockSpec((PEER_HEADS, nk, tm), lambda s: (0, 0, prev(s)[0])),
                pl.BlockSpec((tm, d), lambda s: (prev(s)[0], 0)),
                pl.BlockSpec((1, 1, d), lambda s: (prev(s)[0] // per_b, 0, 0))]
    args = [h, u, vt, c1, a, r2, w, x, g2]
    if final:
        in_specs.append(pl.BlockSpec((1, d), lambda s: (0, 0)))
        args.append(final_gain.reshape(1, d))
    return pl.pallas_call(
        functools.partial(_peer_kernel, tm=tm, te=te, nk=nk, ne=ne, final=final),
        grid=(steps,),
        in_specs=in_specs,
        out_specs=pl.BlockSpec((tm, d), lambda s: (prev(s)[0], 0)),
        out_shape=jax.ShapeDtypeStruct((t, d), F32),
        scratch_shapes=[pltpu.VMEM((te, tm), BF16), pltpu.VMEM((te, tm), BF16),
                        pltpu.VMEM((te, tm), BF16), pltpu.VMEM((d, tm), F32),
                        pltpu.VMEM((PEER_HEADS, nk, tm), BF16), pltpu.VMEM((PEER_HEADS, nk, tm), BF16)],
        compiler_params=_params(("arbitrary",)), name="peer_experts",
    )(*args)


def _pad_cols(w, n):
    return jnp.pad(w, ((0, 0), (0, n - w.shape[1])))


def _even_w_in(w):
    na, ni, nb = A_HEADS * HEAD_DIM, IDX_HEADS * IDX_DIM, B_HEADS * 2 * HEAD_DIM
    o = np.cumsum([0, na, A_LATENT, ni, IDX_DIM, IDX_HEADS, nb, nb, nb])
    kw = _pad_cols(w[:, o[3]:o[5]], LANES)
    return jnp.concatenate([w[:, o[0]:o[3]], kw, w[:, o[5]:o[8]]], axis=1).astype(BF16)


def kernel(x, c, rel_bias, ada_w, ada_b, norm_mix, norm_ffn, norm_final, even_w_in, even_w_out,
           a_kv_norm, a_w_uk, a_w_uv, b_lambda, b_subln, odd_w_in, odd_b_forget, odd_w_out,
           peer_w_q, peer_sub_keys, peer_u, peer_v):
    batch, seq, d = x.shape
    depth = ada_w.shape[0]
    t = batch * seq
    assert seq % ROW_TILE == 0 and seq % ATT_BLOCK == 0 and t % PEER_TOKEN_TILE == 0

    mod = _ada(c, ada_w, ada_b)
    tiles = _bias_tiles(rel_bias)

    na, ni, nb = A_HEADS * HEAD_DIM, IDX_HEADS * IDX_DIM, B_HEADS * 2 * HEAD_DIM
    nc = C_HEADS * HEAD_DIM
    o = np.cumsum([0, na, A_LATENT, ni, LANES, nb, nb, nb])
    even_segs = [(o[0], o[1], "Tq", [BF16]), (o[1], o[2], "kvnorm", [BF16]), (o[2], o[3], "Tq", [BF16]),
                 (o[3], o[4], None, [BF16]), (o[3], o[4], "T", [F32]), (o[4], o[5], "Tq", [BF16]),
                 (o[5], o[6], None, [BF16]), (o[6], o[7], "T", [BF16])]
    odd_segs = [(0, nc, "Tq", [BF16]), (nc, 2 * nc, None, [BF16]), (2 * nc, 3 * nc, "T", [BF16]),
                (3 * nc, 3 * nc + LANES, None, [F32])]
    nq_peer = peer_w_q.shape[2]
    peer_segs = [(0, nq_peer, None, [BF16])]

    u_all = peer_u.astype(BF16)
    vt_all = jnp.transpose(peer_v, (0, 2, 1)).astype(BF16)
    xt = x.reshape(t, d)
    for layer in range(depth):
        m6 = mod[layer].reshape(batch, 6, 1, d)
        sh1, sc1, g1, sh2, sc2, g2 = [m6[:, i] for i in range(6)]
        if layer % 2 == 0:
            e = layer // 2
            lam_init = 0.8 - 0.6 * math.exp(-0.3 * layer)
            w_in = _even_w_in(even_w_in[e])
            qat, ckv, ckvt, qit, kw, wit, qbt, k_b, vbt = _normproj(
                xt, norm_mix[layer], sc1, sh1, w_in, even_segs, seq, kvn=a_kv_norm[e], wt=w_in.T,
                name="even_in_proj")
            wuk = jnp.transpose(a_w_uk[e], (1, 0, 2)).astype(BF16)
            wuvt = jnp.transpose(a_w_uv[e], (1, 2, 0)).astype(BF16)
            o_a = _dsa_attention(rel_bias, qat, qit, wit, kw, ckv, ckvt, wuk, wuvt, tiles, batch, seq)
            o_b = _diff_attention(rel_bias, qbt, k_b, vbt, tiles, b_lambda[e], b_subln[e], batch, seq, lam_init)
            mixes = [o_a, o_b]
            w_out = even_w_out[e].astype(BF16)
        else:
            od = layer // 2
            w_in = _pad_cols(odd_w_in[od], 3 * nc + LANES).astype(BF16)
            qt, k, vt, f = _normproj(xt, norm_mix[layer], sc1, sh1, w_in, odd_segs, seq, wt=w_in.T,
                                     name="odd_in_proj")
            kf, qft = _foxgate(f, odd_b_forget[od], batch, seq)
            mixes = [_fox_attention(qt, qft, k, kf, vt, batch, seq)]
            w_out = odd_w_out[od].astype(BF16)
        xt, h2, qp = _resproj(xt, mixes, w_out, g1, norm_ffn[layer], sc2, sh2,
                              peer_w_q[layer].astype(BF16), peer_segs, seq)
        c1, a, r2, w = _router(qp, peer_sub_keys[layer].astype(BF16))
        xt = _peer(h2, u_all, vt_all, layer, c1, a, r2, w,
                   xt, g2, seq, final_gain=norm_final if layer == depth - 1 else None)
    return xt.reshape(batch, seq, d)
```
